```python
import math
import jax, jax.numpy as jnp
from jax import lax
import numpy as np

D_MODEL = 1024
BATCH = 4
SEQ = 8192
DEPTH = 2

CHUNK = 64
N_MEM = 256
BRANCH_W = 256
N_BRANCH = 4
D_A = BRANCH_W
CONV_W = 3
D_B = BRANCH_W
SG_BLOCK = 128
SG_GROUPS = 4
RW_HEADS = 4
RW_HEAD_DIM = 64
D_C = RW_HEADS * RW_HEAD_DIM
RW_DECAY_LORA = 64
RW_AAA_LORA = 64
RW_GATE_LORA = 128
RW_IN = 3 * D_C + RW_DECAY_LORA + RW_AAA_LORA + RW_GATE_LORA
RW_LN_EPS = 64e-5
D_S = BRANCH_W
S5_CH = 16
S5_GROUPS = D_S // S5_CH
S5_STATE = 64
OFF_B = 3 * D_A
OFF_C = OFF_B + 2 * D_B
OFF_D = OFF_C + RW_IN
OFF_G = OFF_D + D_S
N_IN = OFF_G + N_BRANCH * D_MODEL
XA_HEADS = 4
XA_HEAD_DIM = D_MODEL // XA_HEADS
N_EXPERTS = 32
TOP_K = 4
D_FF = D_MODEL
SWIGLU_LIMIT = 7.0
SWIGLU_ALPHA = 1.702
MOE_BLOCK = 512
LN_EPS = 1e-5
DN_ALPHA = (2 * DEPTH) ** 0.25
DN_BETA = (8 * DEPTH) ** -0.25

kernel_name = "hybrid_chunk_causal_gated_branch_moe_encoder"


def layer_norm(x, g, b, eps=LN_EPS):
    xf = x.astype(jnp.float32)
    mu = xf.mean(-1, keepdims=True)
    var = jnp.square(xf - mu).mean(-1, keepdims=True)
    y = (xf - mu) * lax.rsqrt(var + eps)
    return (y * g.astype(jnp.float32) + b.astype(jnp.float32)).astype(x.dtype)


def shift_time(z, n=1):
    return jnp.pad(z, ((0, 0), (n, 0), (0, 0)))[:, :z.shape[1]]


def short_conv_mixer(z, conv_w):
    b_gate, c_gate, hin = jnp.split(z, 3, axis=-1)
    ch = c_gate * hin
    y = conv_w[CONV_W - 1] * ch
    for i in range(CONV_W - 1):
        y = y + conv_w[i] * shift_time(ch, CONV_W - 1 - i)
    return b_gate * y


def spatial_gating_mixer(z, norm_g, norm_b, w_s, b_s):
    u, v = jnp.split(z, 2, axis=-1)
    v = layer_norm(v, norm_g, norm_b)
    B_, S_, _ = v.shape
    nb = S_ // SG_BLOCK
    v = v.reshape(B_, nb, SG_BLOCK, SG_GROUPS, D_B // SG_GROUPS)
    pos = jnp.arange(SG_BLOCK)
    mask = (pos[None, :] // CHUNK) <= (pos[:, None] // CHUNK)
    w = jnp.where(mask[None], w_s, 0)
    sv = jnp.einsum('gij,bnjgc->bnigc', w, v) + b_s.T[None, None, :, :, None]
    return u * sv.reshape(B_, S_, D_B)


def rwkv7_mixer(z, mu, w0, w_up, a0, a_up, g_up, k_k, k_a, r_k, ln_g, ln_b):
    B_, S_, _ = z.shape
    f32 = jnp.float32
    z = z + (shift_time(z) - z) * mu
    r, k, v, xw, xa, xg = jnp.split(
        z, [D_C, 2 * D_C, 3 * D_C, 3 * D_C + RW_DECAY_LORA,
            3 * D_C + RW_DECAY_LORA + RW_AAA_LORA], axis=-1)
    w = -jax.nn.softplus(-(w0 + jnp.tanh(xw) @ w_up)) - 0.5
    decay = jnp.exp(-jnp.exp(w.astype(f32)))
    a = jax.nn.sigmoid(a0 + xa @ a_up)
    g = jax.nn.sigmoid(xg) @ g_up
    kk = k * k_k
    k = k * (1.0 + (a - 1.0) * k_a)

    def heads(t):
        return t.astype(f32).reshape(B_, S_, RW_HEADS, RW_HEAD_DIM)

    r, k, v, kk, a, decay = (heads(t) for t in (r, k, v, kk, a, decay))
    kk = kk / jnp.maximum(jnp.linalg.norm(kk, axis=-1, keepdims=True), 1e-12)

    def step(state, inp):
        r_t, w_t, k_t, v_t, kk_t, a_t = inp
        sa = jnp.einsum('bhvk,bhk->bhv', state, -kk_t)
        state = (state * w_t[:, :, None, :]
                 + sa[..., None] * (kk_t * a_t)[:, :, None, :]
                 + v_t[..., None] * k_t[:, :, None, :])
        y_t = jnp.einsum('bhvk,bhk->bhv', state, r_t)
        return state, y_t

    xs = tuple(jnp.moveaxis(t, 1, 0) for t in (r, decay, k, v, kk, a))
    state0 = jnp.zeros((B_, RW_HEADS, RW_HEAD_DIM, RW_HEAD_DIM), f32)
    _, y = lax.scan(step, state0, xs)
    y = jnp.moveaxis(y, 0, 1)
    m = y.mean(-1, keepdims=True)
    var = jnp.square(y - m).mean(-1, keepdims=True)
    yn = ((y - m) * lax.rsqrt(var + RW_LN_EPS)).reshape(B_, S_, D_C)
    yn = yn * ln_g.astype(f32) + ln_b.astype(f32)
    bonus = (jnp.sum(r * k * r_k.astype(f32), -1, keepdims=True) * v).reshape(B_, S_, D_C)
    return ((yn + bonus) * g.astype(f32)).astype(z.dtype)


def _complex_affine_combine(e1, e2):
    a1r, a1i, b1r, b1i = e1
    a2r, a2i, b2r, b2i = e2
    return (a2r * a1r - a2i * a1i,
            a2r * a1i + a2i * a1r,
            a2r * b1r - a2i * b1i + b2r,
            a2r * b1i + a2i * b1r + b2i)


def s5_mixer(u, a_re, a_im, b_re, b_im, c_re, c_im, d, log_dt, glu_w, glu_b):
    B_, S_, _ = u.shape
    f32 = jnp.float32
    a_re, a_im, b_re, b_im, c_re, c_im = (p.astype(f32) for p in (a_re, a_im, b_re, b_im, c_re, c_im))
    uf = u.astype(f32)
    dt = jnp.exp(log_dt.astype(f32))[:, None]
    mag = jnp.exp(a_re * dt)
    abar_re = mag * jnp.cos(a_im * dt)
    abar_im = mag * jnp.sin(a_im * dt)
    den = a_re * a_re + a_im * a_im
    num_re = abar_re - 1.0
    coef_re = (num_re * a_re + abar_im * a_im) / den
    coef_im = (abar_im * a_re - num_re * a_im) / den
    bbar_re = coef_re[..., None] * b_re - coef_im[..., None] * b_im
    bbar_im = coef_re[..., None] * b_im + coef_im[..., None] * b_re
    ug = uf.reshape(B_, S_, S5_GROUPS, S5_CH)
    bu_re = jnp.einsum('gpc,bsgc->bsgp', bbar_re, ug)
    bu_im = jnp.einsum('gpc,bsgc->bsgp', bbar_im, ug)
    shp = (1, S_, S5_GROUPS, S5_STATE)
    elems = (jnp.broadcast_to(abar_re, shp), jnp.broadcast_to(abar_im, shp), bu_re, bu_im)
    _, _, x_re, x_im = lax.associative_scan(_complex_affine_combine, elems, axis=1)
    y = jnp.einsum('gcp,bsgp->bsgc', c_re, x_re) - jnp.einsum('gcp,bsgp->bsgc', c_im, x_im)
    y = y.reshape(B_, S_, D_S) + d.astype(f32) * uf
    y = jax.nn.gelu(y)
    y = y * jax.nn.sigmoid(y @ glu_w.astype(f32) + glu_b.astype(f32))
    return y.astype(u.dtype)


def memory_cross_attention(h, mem, wq, wk, wv, wo):
    B_, S_, _ = h.shape
    M_ = mem.shape[1]
    q = (h @ wq).reshape(B_, S_, XA_HEADS, XA_HEAD_DIM)
    k = (mem @ wk).reshape(B_, M_, XA_HEADS, XA_HEAD_DIM)
    v = (mem @ wv).reshape(B_, M_, XA_HEADS, XA_HEAD_DIM)
    s = jnp.einsum('bshd,bmhd->bhsm', q.astype(jnp.float32), k.astype(jnp.float32))
    p = jax.nn.softmax(s * (XA_HEAD_DIM ** -0.5), axis=-1).astype(v.dtype)
    o = jnp.einsum('bhsm,bmhd->bshd', p, v).reshape(B_, S_, D_MODEL)
    return o @ wo


def moe_ffn(h, router_w, router_b, w1, b1, w2, b2):
    B_, S_, D_ = h.shape
    t = h.reshape(B_ * S_, D_)
    n_tok = t.shape[0]
    n_asg = n_tok * TOP_K
    logits = (t @ router_w + router_b).astype(jnp.float32)
    top_val, top_idx = lax.top_k(logits, TOP_K)
    top_w = jax.nn.softmax(top_val, axis=-1)
    flat_e = top_idx.reshape(-1)
    order = jnp.argsort(flat_e)
    sorted_e = flat_e[order]
    counts = jnp.bincount(flat_e, length=N_EXPERTS)
    padded = (counts + MOE_BLOCK - 1) // MOE_BLOCK * MOE_BLOCK
    pad_end = jnp.cumsum(padded)
    pad_start = pad_end - padded
    grp_start = jnp.cumsum(counts) - counts
    dest = pad_start[sorted_e] + jnp.arange(n_asg) - grp_start[sorted_e]
    n_blocks = -(-n_asg // MOE_BLOCK) + N_EXPERTS
    n_slots = n_blocks * MOE_BLOCK
    slot_tok = jnp.full((n_slots,), n_tok, jnp.int32).at[dest].set((order // TOP_K).astype(jnp.int32))
    slot_w = jnp.zeros((n_slots,), jnp.float32).at[dest].set(top_w.reshape(-1)[order])
    blk_e = jnp.minimum(jnp.searchsorted(pad_end, jnp.arange(n_blocks) * MOE_BLOCK, side='right'),
                        N_EXPERTS - 1)
    t_pad = jnp.concatenate([t, jnp.zeros((1, D_), t.dtype)], axis=0)
    xb = t_pad[slot_tok].reshape(n_blocks, MOE_BLOCK, D_)

    def expert_block(args):
        xblk, e = args
        hid = xblk @ w1[e] + b1[e]
        gate = jnp.minimum(hid[:, 0::2], SWIGLU_LIMIT)
        up = jnp.clip(hid[:, 1::2], -SWIGLU_LIMIT, SWIGLU_LIMIT)
        act = (up + 1.0) * gate * jax.nn.sigmoid(SWIGLU_ALPHA * gate)
        return act @ w2[e] + b2[e]

    yb = lax.map(expert_block, (xb, blk_e))
    y = (yb.reshape(n_slots, D_).astype(jnp.float32) * slot_w[:, None]).astype(t.dtype)
    out = jnp.zeros((n_tok + 1, D_), t.dtype).at[slot_tok].add(y)[:n_tok]
    return out.reshape(B_, S_, D_)


def setup_inputs(seed: int = 0) -> dict:
    key = jax.random.key(seed)
    keys = jax.random.split(key, 64)
    counter = [0]

    def nk():
        k = keys[counter[0]]
        counter[0] += 1
        return k

    def nrm(shape, scale):
        return jax.random.normal(nk(), shape, jnp.float32) * scale

    def unif(shape, lo, hi):
        return jax.random.uniform(nk(), shape, jnp.float32, lo, hi)

    L, D = DEPTH, D_MODEL
    inp = {}
    inp["x"] = nrm((BATCH, SEQ, D), 1.0)
    inp["mem"] = nrm((BATCH, N_MEM, D), 1.0)
    inp["ln_in_g"] = 1.0 + nrm((D,), 0.02)
    inp["ln_in_b"] = nrm((D,), 0.02)
    inp["w_in"] = nrm((L, D, N_IN), D ** -0.5)
    inp["conv_w"] = nrm((L, CONV_W, D_A), CONV_W ** -0.5)
    inp["sg_norm_g"] = 1.0 + nrm((L, D_B), 0.02)
    inp["sg_norm_b"] = nrm((L, D_B), 0.02)
    inp["sg_w"] = nrm((L, SG_GROUPS, SG_BLOCK, SG_BLOCK), SG_BLOCK ** -0.5)
    inp["sg_b"] = 1.0 + nrm((L, SG_GROUPS, SG_BLOCK), 0.02)
    inp["rw_mu"] = unif((L, RW_IN), 0.0, 1.0)
    inp["rw_w0"] = unif((L, D_C), -6.5, -1.0)
    inp["rw_w_up"] = nrm((L, RW_DECAY_LORA, D_C), 0.1)
    inp["rw_a0"] = nrm((L, D_C), 0.1)
    inp["rw_a_up"] = nrm((L, RW_AAA_LORA, D_C), 0.1)
    inp["rw_g_up"] = nrm((L, RW_GATE_LORA, D_C), RW_GATE_LORA ** -0.5)
    inp["rw_k_k"] = 0.85 + nrm((L, D_C), 0.02)
    inp["rw_k_a"] = 1.0 + nrm((L, D_C), 0.02)
    inp["rw_r_k"] = nrm((L, RW_HEADS, RW_HEAD_DIM), 0.1)
    inp["rw_ln_g"] = 1.0 + nrm((L, D_C), 0.02)
    inp["rw_ln_b"] = nrm((L, D_C), 0.02)
    inp["s5_a_re"] = -0.5 + nrm((L, S5_GROUPS, S5_STATE), 0.01)
    inp["s5_a_im"] = (jnp.pi * jnp.arange(S5_STATE, dtype=jnp.float32))[None, None, :] + nrm((L, S5_GROUPS, S5_STATE), 0.01)
    inp["s5_b_re"] = nrm((L, S5_GROUPS, S5_STATE, S5_CH), (2 * S5_CH) ** -0.5)
    inp["s5_b_im"] = nrm((L, S5_GROUPS, S5_STATE, S5_CH), (2 * S5_CH) ** -0.5)
    inp["s5_c_re"] = nrm((L, S5_GROUPS, S5_CH, S5_STATE), (2 * S5_STATE) ** -0.5)
    inp["s5_c_im"] = nrm((L, S5_GROUPS, S5_CH, S5_STATE), (2 * S5_STATE) ** -0.5)
    inp["s5_d"] = nrm((L, D_S), 1.0)
    inp["s5_log_dt"] = unif((L, S5_GROUPS), math.log(1e-3), math.log(1e-1))
    inp["s5_glu_w"] = nrm((L, D_S, D_S), D_S ** -0.5)
    inp["s5_glu_b"] = nrm((L, D_S), 0.02)
    inp["br_proj"] = nrm((L, N_BRANCH, BRANCH_W, D), BRANCH_W ** -0.5)
    inp["gate_b"] = nrm((L, N_BRANCH, D), 0.02)
    inp["w_out"] = nrm((L, D, D), D ** -0.5 * DN_BETA)
    inp["ln1_g"] = 1.0 + nrm((L, D), 0.02)
    inp["ln1_b"] = nrm((L, D), 0.02)
    inp["xa_wq"] = nrm((L, D, D), D ** -0.5)
    inp["xa_wk"] = nrm((L, D, D), D ** -0.5)
    inp["xa_wv"] = nrm((L, D, D), D ** -0.5)
    inp["xa_wo"] = nrm((L, D, D), D ** -0.5 * DN_BETA)
    inp["ln2_g"] = 1.0 + nrm((L, D), 0.02)
    inp["ln2_b"] = nrm((L, D), 0.02)
    inp["router_w"] = nrm((L, D, N_EXPERTS), D ** -0.5)
    inp["router_b"] = nrm((L, N_EXPERTS), 0.01)
    inp["ex_w1"] = nrm((L, N_EXPERTS, D, 2 * D_FF), D ** -0.5)
    inp["ex_b1"] = nrm((L, N_EXPERTS, 2 * D_FF), 0.01)
    inp["ex_w2"] = nrm((L, N_EXPERTS, D_FF, D), D_FF ** -0.5 * DN_BETA)
    inp["ex_b2"] = nrm((L, N_EXPERTS, D), 0.01)
    inp["ln3_g"] = 1.0 + nrm((L, D), 0.02)
    inp["ln3_b"] = nrm((L, D), 0.02)
    return inp


def reference(x, mem, ln_in_g, ln_in_b, w_in, conv_w, sg_norm_g, sg_norm_b, sg_w, sg_b,
              rw_mu, rw_w0, rw_w_up, rw_a0, rw_a_up, rw_g_up, rw_k_k, rw_k_a, rw_r_k,
              rw_ln_g, rw_ln_b, s5_a_re, s5_a_im, s5_b_re, s5_b_im, s5_c_re, s5_c_im,
              s5_d, s5_log_dt, s5_glu_w, s5_glu_b, br_proj, gate_b, w_out, ln1_g, ln1_b,
              xa_wq, xa_wk, xa_wv, xa_wo, ln2_g, ln2_b, router_w, router_b,
              ex_w1, ex_b1, ex_w2, ex_b2, ln3_g, ln3_b):
    h = layer_norm(x, ln_in_g, ln_in_b)
    for l in range(DEPTH):
        z = h @ w_in[l]
        za, zb, zc, zd, zg = jnp.split(z, [OFF_B, OFF_C, OFF_D, OFF_G], axis=-1)
        o_a = short_conv_mixer(za, conv_w[l])
        o_b = spatial_gating_mixer(zb, sg_norm_g[l], sg_norm_b[l], sg_w[l], sg_b[l])
        o_c = rwkv7_mixer(zc, rw_mu[l], rw_w0[l], rw_w_up[l], rw_a0[l], rw_a_up[l],
                          rw_g_up[l], rw_k_k[l], rw_k_a[l], rw_r_k[l], rw_ln_g[l], rw_ln_b[l])
        o_d = s5_mixer(zd, s5_a_re[l], s5_a_im[l], s5_b_re[l], s5_b_im[l], s5_c_re[l],
                       s5_c_im[l], s5_d[l], s5_log_dt[l], s5_glu_w[l], s5_glu_b[l])
        branches = (o_a, o_b, o_c, o_d)
        gate_pre = jnp.split(zg, N_BRANCH, axis=-1)
        merged = None
        for i in range(N_BRANCH):
            term = jax.nn.sigmoid(gate_pre[i] + gate_b[l, i]) * (branches[i] @ br_proj[l, i])
            merged = term if merged is None else merged + term
        h = layer_norm(DN_ALPHA * h + merged @ w_out[l], ln1_g[l], ln1_b[l])
        h = layer_norm(DN_ALPHA * h + memory_cross_attention(h, mem, xa_wq[l], xa_wk[l],
                                                             xa_wv[l], xa_wo[l]),
                       ln2_g[l], ln2_b[l])
        h = layer_norm(DN_ALPHA * h + moe_ffn(h, router_w[l], router_b[l], ex_w1[l], ex_b1[l],
                                              ex_w2[l], ex_b2[l]),
                       ln3_g[l], ln3_b[l])
    return h
```

```python
import functools
import math

import jax
import jax.numpy as jnp
from jax import lax
from jax.experimental import pallas as pl
from jax.experimental.pallas import tpu as pltpu

F32 = jnp.float32
BF16 = jnp.bfloat16
HIGHEST = lax.Precision.HIGHEST

D_MODEL = 1024
DEPTH = 2
CHUNK = 64
BRANCH_W = 256
N_BRANCH = 4
CONV_W = 3
SG_BLOCK = 128
SG_GROUPS = 4
RW_HEADS = 4
RW_HEAD_DIM = 64
RW_LORA = 64
RW_GATE_LORA = 128
RW_IN = 3 * BRANCH_W + 2 * RW_LORA + RW_GATE_LORA
RW_LN_EPS = 64e-5
S5_CH = 16
S5_GROUPS = BRANCH_W // S5_CH
S5_STATE = 64
S5_N = S5_GROUPS * S5_STATE
OFF_B = 3 * BRANCH_W
OFF_C = OFF_B + 2 * BRANCH_W
OFF_D = OFF_C + RW_IN
OFF_G = OFF_D + BRANCH_W
XA_HEADS = 4
XA_HEAD_DIM = D_MODEL // XA_HEADS
N_EXPERTS = 32
TOP_K = 4
D_FF = D_MODEL
SWIGLU_LIMIT = 7.0
SWIGLU_ALPHA = 1.702
LN_EPS = 1e-5
DN_ALPHA = (2 * DEPTH) ** 0.25

V7X_LANES = 128
V7X_SUBLANES = 8
V7X_VMEM_LIMIT_BYTES = 56 * 1024 * 1024
TOKEN_TILE = 256
SEQ_TILE = 256
RW_CHUNK = 64
MOE_TILE = 256
ROUTE_LANES = 128


def _cparams(*sem):
    return pltpu.CompilerParams(dimension_semantics=sem,
                                vmem_limit_bytes=V7X_VMEM_LIMIT_BYTES)


def _full(shape):
    nd = len(shape)
    return pl.BlockSpec(shape, lambda *_: (0,) * nd)


def _layer_norm(x, g, b, eps=LN_EPS):
    mu = jnp.mean(x, axis=-1, keepdims=True)
    xc = x - mu
    var = jnp.mean(xc * xc, axis=-1, keepdims=True)
    return xc * lax.rsqrt(var + eps) * g + b


def _dot(a, b, **kw):
    return jnp.dot(a, b, preferred_element_type=F32, **kw)


def _dot_nt(a, b, **kw):
    return lax.dot_general(a, b, (((1,), (1,)), ((), ())),
                           preferred_element_type=F32, **kw)


def _dot_tn(a, b, **kw):
    return lax.dot_general(a, b, (((0,), (0,)), ((), ())),
                           preferred_element_type=F32, **kw)


def _shift_rows(x, n, tail):
    rolled = pltpu.roll(x, n, 0)
    row = lax.broadcasted_iota(jnp.int32, x.shape, 0)
    out = rolled
    for i in range(n):
        src = tail[V7X_SUBLANES - n + i:V7X_SUBLANES - n + i + 1, :]
        out = jnp.where(row == i, src, out)
    return out


def _ln_kernel(x_ref, g_ref, b_ref, o_ref):
    o_ref[...] = _layer_norm(x_ref[...], g_ref[...], b_ref[...])


def _input_ln(x2, g, b):
    T, D = x2.shape
    tm = TOKEN_TILE * 2
    return pl.pallas_call(
        _ln_kernel,
        grid=(T // tm,),
        in_specs=[pl.BlockSpec((tm, D), lambda i: (i, 0)), _full((1, D)), _full((1, D))],
        out_specs=pl.BlockSpec((tm, D), lambda i: (i, 0)),
        out_shape=jax.ShapeDtypeStruct((T, D), F32),
        compiler_params=_cparams("parallel"),
    )(x2, g.reshape(1, D), b.reshape(1, D))


def _ab_kernel(h_ref, w_ref, cw_ref, ng_ref, nb_ref, sw_ref, sb_ref,
               oa_ref, ob_ref, tail_ref):
    @pl.when(pl.program_id(1) == 0)
    def _():
        tail_ref[...] = jnp.zeros_like(tail_ref)

    tm = h_ref.shape[0]
    W = BRANCH_W
    z = _dot(h_ref[...].astype(BF16), w_ref[...])
    ch = z[:, W:2 * W] * z[:, 2 * W:3 * W]
    tail = tail_ref[...]
    s1 = _shift_rows(ch, 1, tail)
    s2 = _shift_rows(ch, 2, tail)
    tail_ref[...] = ch[tm - V7X_SUBLANES:, :]
    cw = cw_ref[...]
    y = cw[2:3, :] * ch + cw[1:2, :] * s1 + cw[0:1, :] * s2
    oa_ref[...] = (z[:, 0:W] * y).astype(oa_ref.dtype)
    u = z[:, 3 * W:4 * W]
    v = _layer_norm(z[:, 4 * W:5 * W], ng_ref[...], nb_ref[...]).astype(BF16)
    gw = W // SG_GROUPS
    grp = lax.broadcasted_iota(jnp.int32, (SG_BLOCK, W), 1) // gw
    for blk in range(tm // SG_BLOCK):
        rows = slice(blk * SG_BLOCK, (blk + 1) * SG_BLOCK)
        vb = v[rows, :]
        sv = sb_ref[...]
        for g in range(SG_GROUPS):
            sv = sv + jnp.where(grp == g, _dot(sw_ref[g], vb), 0.0)
        ob_ref[rows, :] = (u[rows, :] * sv).astype(ob_ref.dtype)


def _mixers_ab(h, w_ab, conv_w, ng, nb, sg_wm, sg_bias):
    B, S, D = h.shape
    tm = SEQ_TILE
    W = BRANCH_W
    blk = lambda b, j: (b, j, 0)
    return pl.pallas_call(
        _ab_kernel,
        grid=(B, S // tm),
        in_specs=[pl.BlockSpec((None, tm, D), blk),
                  _full((D, 5 * W)), _full((CONV_W, W)), _full((1, W)), _full((1, W)),
                  _full((SG_GROUPS, SG_BLOCK, SG_BLOCK)), _full((SG_BLOCK, W))],
        out_specs=[pl.BlockSpec((None, tm, W), blk), pl.BlockSpec((None, tm, W), blk)],
        out_shape=[jax.ShapeDtypeStruct((B, S, W), BF16)] * 2,
        scratch_shapes=[pltpu.VMEM((V7X_SUBLANES, W), F32)],
        compiler_params=_cparams("parallel", "arbitrary"),
    )(h, w_ab, conv_w, ng, nb, sg_wm, sg_bias)


def _softplus(x):
    return jnp.maximum(x, 0.0) + jnp.log(1.0 + jnp.exp(-jnp.abs(x)))


def _rwkv_kernel(h_ref, w_ref, mu_ref, w0_ref, a0_ref, lora_ref, gup_ref, kk_ref, ka_ref,
                 rk_ref, lng_ref, lnb_ref, ones_ref, o_ref,
                 tail_ref, state_ref, r_s, k_s, v_s, al_s, be_s, ld_s, y_s):
    @pl.when(pl.program_id(1) == 0)
    def _():
        tail_ref[...] = jnp.zeros_like(tail_ref)
        state_ref[...] = jnp.zeros_like(state_ref)

    tm = h_ref.shape[0]
    W = BRANCH_W
    N = RW_HEAD_DIM
    L = RW_CHUNK
    z = _dot(h_ref[...].astype(BF16), w_ref[...])
    zprev = _shift_rows(z, 1, tail_ref[...])
    tail_ref[...] = z[tm - V7X_SUBLANES:, :]
    z = z + (zprev - z) * mu_ref[...]
    r = z[:, 0:W]
    k = z[:, W:2 * W]
    v = z[:, 2 * W:3 * W]
    xwa = z[:, 3 * W:3 * W + 2 * RW_LORA]
    xg = z[:, 3 * W + 2 * RW_LORA:]
    lane = lax.broadcasted_iota(jnp.int32, xwa.shape, 1)
    lora_in = jnp.where(lane < RW_LORA, jnp.tanh(xwa), xwa)
    lora = _dot(lora_in, lora_ref[...], precision=HIGHEST)
    w = -_softplus(-(w0_ref[...] + lora[:, 0:W])) - 0.5
    log_decay = -jnp.exp(w)
    a = jax.nn.sigmoid(a0_ref[...] + lora[:, W:2 * W])
    g = _dot(jax.nn.sigmoid(xg), gup_ref[...], precision=HIGHEST)
    ones = ones_ref[...]
    kk = k * kk_ref[...]
    kk = kk / jnp.maximum(jnp.sqrt(_dot(kk * kk, ones, precision=HIGHEST)), 1e-12)
    k = k * (1.0 + (a - 1.0) * ka_ref[...])
    bonus = _dot(r * k * rk_ref[...], ones, precision=HIGHEST) * v

    r_s[...] = r
    k_s[...] = k
    v_s[...] = v
    al_s[...] = -kk
    be_s[...] = kk * a
    ld_s[...] = log_decay

    ti = lax.broadcasted_iota(jnp.int32, (L, L), 0)
    tj = lax.broadcasted_iota(jnp.int32, (L, L), 1)
    incl = ti >= tj
    strict = ti > tj
    tri = incl.astype(F32)
    eye = (ti == tj).astype(F32)
    hp = dict(precision=HIGHEST)

    def chunk(c, carry):
        rows = pl.ds(pl.multiple_of(c * L, L), L)
        for hd in range(RW_HEADS):
            cols = slice(hd * N, (hd + 1) * N)
            ld = ld_s[rows, cols]
            cum = _dot(tri, ld, **hp)
            gam = jnp.exp(cum)
            gam_inv = jnp.exp(-cum)
            abar = al_s[rows, cols] * jnp.exp(cum - ld)
            rbar = r_s[rows, cols] * gam
            bt = be_s[rows, cols] * gam_inv
            kt = k_s[rows, cols] * gam_inv
            vv = v_s[rows, cols]
            st = state_ref[hd]
            a_ab = jnp.where(strict, _dot_nt(abar, bt, **hp), 0.0)
            a_ak = jnp.where(strict, _dot_nt(abar, kt, **hp), 0.0)
            r_b = jnp.where(incl, _dot_nt(rbar, bt, **hp), 0.0)
            r_k = jnp.where(incl, _dot_nt(rbar, kt, **hp), 0.0)
            inv = eye + a_ab
            p = a_ab
            for _ in range(int(math.log2(L)) - 1):
                p = _dot(p, p, **hp)
                inv = inv + _dot(inv, p, **hp)
            u = _dot(inv, _dot_nt(abar, st, **hp) + _dot(a_ak, vv, **hp), **hp)
            y = _dot_nt(rbar, st, **hp) + _dot(r_b, u, **hp) + _dot(r_k, vv, **hp)
            y_s[rows, cols] = y
            st = st + _dot_tn(u, bt, **hp) + _dot_tn(vv, kt, **hp)
            state_ref[hd] = st * gam[L - 1:L, :]
        return carry

    lax.fori_loop(0, tm // L, chunk, 0)

    y = y_s[...]
    inv_n = 1.0 / N
    m = _dot(y, ones, precision=HIGHEST) * inv_n
    yc = y - m
    var = _dot(yc * yc, ones, precision=HIGHEST) * inv_n
    yn = yc * lax.rsqrt(var + RW_LN_EPS) * lng_ref[...] + lnb_ref[...]
    o_ref[...] = ((yn + bonus) * g).astype(o_ref.dtype)


def _mixer_rwkv(h, w_c, mu, w0, a0, lora_w, g_up, k_k, k_a, r_k, ln_g, ln_b, head_ones):
    B, S, D = h.shape
    tm = SEQ_TILE
    W = BRANCH_W
    blk = lambda b, j: (b, j, 0)
    row = lambda n: _full((1, n))
    return pl.pallas_call(
        _rwkv_kernel,
        grid=(B, S // tm),
        in_specs=[pl.BlockSpec((None, tm, D), blk), _full((D, RW_IN)), row(RW_IN), row(W), row(W),
                  _full((2 * RW_LORA, 2 * W)), _full((RW_GATE_LORA, W)), row(W), row(W), row(W),
                  row(W), row(W), _full((W, W))],
        out_specs=pl.BlockSpec((None, tm, W), blk),
        out_shape=jax.ShapeDtypeStruct((B, S, W), BF16),
        scratch_shapes=[pltpu.VMEM((V7X_SUBLANES, RW_IN), F32),
                        pltpu.VMEM((RW_HEADS, RW_HEAD_DIM, RW_HEAD_DIM), F32)]
                       + [pltpu.VMEM((tm, W), F32)] * 7,
        compiler_params=_cparams("parallel", "arbitrary"),
    )(h, w_c, mu, w0, a0, lora_w, g_up, k_k, k_a, r_k, ln_g, ln_b, head_ones)


def _s5_kernel(h_ref, w_ref, bre_ref, bim_ref, are_ref, aim_ref, cre_ref, cim_ref,
               d_ref, gw_ref, gb_ref, o_ref, xr_s, xi_s, st_s):
    @pl.when(pl.program_id(1) == 0)
    def _():
        st_s[...] = jnp.zeros_like(st_s)

    tm = h_ref.shape[0]
    u = _dot(h_ref[...].astype(BF16), w_ref[...])
    ub = u.astype(BF16)
    xr_s[...] = _dot(ub, bre_ref[...])
    xi_s[...] = _dot(ub, bim_ref[...])
    ar = are_ref[...]
    ai = aim_ref[...]

    def step(t, carry):
        xr, xi = carry
        row = pl.ds(t, 1)
        nr = ar * xr - ai * xi + xr_s[row, :]
        ni = ar * xi + ai * xr + xi_s[row, :]
        xr_s[row, :] = nr
        xi_s[row, :] = ni
        return nr, ni

    xr, xi = lax.fori_loop(0, tm, step, (st_s[0:1, :], st_s[1:2, :]), unroll=8)
    st_s[0:1, :] = xr
    st_s[1:2, :] = xi
    y = (_dot(xr_s[...].astype(BF16), cre_ref[...]) - _dot(xi_s[...].astype(BF16), cim_ref[...])
         + d_ref[...] * u)
    y = jax.nn.gelu(y)
    gate = jax.nn.sigmoid(_dot(y.astype(BF16), gw_ref[...]) + gb_ref[...])
    o_ref[...] = (y * gate).astype(o_ref.dtype)


def _mixer_s5(h, w_d, bre, bim, are, aim, cre, cim, d, glu_w, glu_b):
    B, S, D = h.shape
    tm = SEQ_TILE
    W = BRANCH_W
    blk = lambda b, j: (b, j, 0)
    return pl.pallas_call(
        _s5_kernel,
        grid=(B, S // tm),
        in_specs=[pl.BlockSpec((None, tm, D), blk), _full((D, W)),
                  _full((W, S5_N)), _full((W, S5_N)), _full((1, S5_N)), _full((1, S5_N)),
                  _full((S5_N, W)), _full((S5_N, W)), _full((1, W)), _full((W, W)), _full((1, W))],
        out_specs=pl.BlockSpec((None, tm, W), blk),
        out_shape=jax.ShapeDtypeStruct((B, S, W), BF16),
        scratch_shapes=[pltpu.VMEM((tm, S5_N), F32), pltpu.VMEM((tm, S5_N), F32),
                        pltpu.VMEM((V7X_SUBLANES, S5_N), F32)],
        compiler_params=_cparams("parallel", "arbitrary"),
    )(h, w_d, bre, bim, are, aim, cre, cim, d, glu_w, glu_b)


def _merge_kernel(h_ref, oa_ref, ob_ref, oc_ref, od_ref, wg_ref, gb_ref, br_ref, wo_ref,
                  g_ref, b_ref, o_ref):
    D = D_MODEL
    h = h_ref[...]
    hb = h.astype(BF16)
    merged = None
    for i, br_in in enumerate((oa_ref, ob_ref, oc_ref, od_ref)):
        gate = jax.nn.sigmoid(_dot(hb, wg_ref[:, i * D:(i + 1) * D]) + gb_ref[i:i + 1, :])
        term = gate * _dot(br_in[...], br_ref[i])
        merged = term if merged is None else merged + term
    y = _dot(merged.astype(BF16), wo_ref[...])
    o_ref[...] = _layer_norm(DN_ALPHA * h + y, g_ref[...], b_ref[...])


def _merge(h2, oa, ob, oc, od, w_g, gate_b, br_proj, w_out, g, b):
    T, D = h2.shape
    tm = TOKEN_TILE
    W = BRANCH_W
    tok = lambda n: pl.BlockSpec((tm, n), lambda i: (i, 0))
    return pl.pallas_call(
        _merge_kernel,
        grid=(T // tm,),
        in_specs=[tok(D), tok(W), tok(W), tok(W), tok(W),
                  _full((D, N_BRANCH * D)), _full((N_BRANCH, D)), _full((N_BRANCH, W, D)),
                  _full((D, D)), _full((1, D)), _full((1, D))],
        out_specs=tok(D),
        out_shape=jax.ShapeDtypeStruct((T, D), F32),
        compiler_params=_cparams("parallel"),
    )(h2, oa, ob, oc, od, w_g, gate_b, br_proj, w_out, g, b)


def _kv_kernel(m_ref, wk_ref, wv_ref, k_ref, v_ref):
    mb = m_ref[...].astype(BF16)
    k_ref[...] = _dot(mb, wk_ref[...]).astype(k_ref.dtype)
    v_ref[...] = _dot(mb, wv_ref[...]).astype(v_ref.dtype)


def _kv_proj(mem2, wk, wv):
    R, D = mem2.shape
    tm = TOKEN_TILE
    tok = pl.BlockSpec((tm, D), lambda i: (i, 0))
    return pl.pallas_call(
        _kv_kernel,
        grid=(R // tm,),
        in_specs=[tok, _full((D, D)), _full((D, D))],
        out_specs=[tok, tok],
        out_shape=[jax.ShapeDtypeStruct((R, D), BF16)] * 2,
        compiler_params=_cparams("parallel"),
    )(mem2, wk, wv)


def _attn_kernel(h_ref, wq_ref, k_ref, v_ref, wo_ref, g_ref, b_ref, o_ref):
    h = h_ref[...]
    q = _dot(h.astype(BF16), wq_ref[...]).astype(BF16)
    scale = XA_HEAD_DIM ** -0.5
    outs = []
    for hd in range(XA_HEADS):
        cols = slice(hd * XA_HEAD_DIM, (hd + 1) * XA_HEAD_DIM)
        s = _dot_nt(q[:, cols], k_ref[:, cols]) * scale
        s = s - jnp.max(s, axis=-1, keepdims=True)
        p = jnp.exp(s)
        p = p / jnp.sum(p, axis=-1, keepdims=True)
        outs.append(_dot(p.astype(BF16), v_ref[:, cols]))
    o = jnp.concatenate(outs, axis=-1).astype(BF16)
    y = _dot(o, wo_ref[...])
    o_ref[...] = _layer_norm(DN_ALPHA * h + y, g_ref[...], b_ref[...])


def _cross_attention(h, kmem, vmem, wq, wo, g, b):
    B, S, D = h.shape
    M = kmem.shape[1]
    tm = TOKEN_TILE
    blk = lambda bi, j: (bi, j, 0)
    mem = lambda bi, j: (bi, 0, 0)
    return pl.pallas_call(
        _attn_kernel,
        grid=(B, S // tm),
        in_specs=[pl.BlockSpec((None, tm, D), blk), _full((D, D)),
                  pl.BlockSpec((None, M, D), mem), pl.BlockSpec((None, M, D), mem),
                  _full((D, D)), _full((1, D)), _full((1, D))],
        out_specs=pl.BlockSpec((None, tm, D), blk),
        out_shape=jax.ShapeDtypeStruct((B, S, D), F32),
        compiler_params=_cparams("parallel", "parallel"),
    )(h, wq, kmem, vmem, wo, g, b)


def _router_kernel(h_ref, w_ref, b_ref, route_ref, rw_ref, cnt_ref, carry_ref):
    @pl.when(pl.program_id(0) == 0)
    def _():
        carry_ref[...] = jnp.zeros_like(carry_ref)

    tm = h_ref.shape[0]
    E = N_EXPERTS
    logits = _dot(h_ref[...], w_ref[...], precision=HIGHEST) + b_ref[...]
    lane = lax.broadcasted_iota(jnp.int32, (tm, E), 1)
    work = logits
    vals, hots, idxs = [], [], []
    for _ in range(TOP_K):
        m = jnp.max(work, axis=-1, keepdims=True)
        idx = jnp.min(jnp.where(work == m, lane, E), axis=-1, keepdims=True)
        hot = lane == idx
        work = jnp.where(hot, -jnp.inf, work)
        vals.append(m)
        hots.append(hot)
        idxs.append(idx)
    exps = [jnp.exp(v - vals[0]) for v in vals]
    denom = exps[0] + exps[1] + exps[2] + exps[3]
    multi = (hots[0] | hots[1] | hots[2] | hots[3]).astype(F32)
    ti = lax.broadcasted_iota(jnp.int32, (tm, tm), 0)
    tj = lax.broadcasted_iota(jnp.int32, (tm, tm), 1)
    before = (ti > tj).astype(BF16)
    prefix = _dot(before, multi.astype(BF16)) + carry_ref[0:1, :]
    out_lane = lax.broadcasted_iota(jnp.int32, (tm, ROUTE_LANES), 1)
    route = jnp.zeros((tm, ROUTE_LANES), jnp.int32)
    rw = jnp.zeros((tm, ROUTE_LANES), F32)
    for kk in range(TOP_K):
        rank = jnp.sum(jnp.where(hots[kk], prefix, 0.0), axis=-1, keepdims=True).astype(jnp.int32)
        route = jnp.where(out_lane == kk, idxs[kk], route)
        route = jnp.where(out_lane == TOP_K + kk, rank, route)
        rw = jnp.where(out_lane == kk, exps[kk] / denom, rw)
    route_ref[...] = route
    rw_ref[...] = rw
    carry_ref[0:1, :] = carry_ref[0:1, :] + jnp.sum(multi, axis=0, keepdims=True)
    cnt_ref[...] = carry_ref[...]


def _router(h2, router_w, router_b):
    T, D = h2.shape
    tm = TOKEN_TILE
    E = N_EXPERTS
    return pl.pallas_call(
        _router_kernel,
        grid=(T // tm,),
        in_specs=[pl.BlockSpec((tm, D), lambda i: (i, 0)), _full((D, E)), _full((1, E))],
        out_specs=[pl.BlockSpec((tm, ROUTE_LANES), lambda i: (i, 0)),
                   pl.BlockSpec((tm, ROUTE_LANES), lambda i: (i, 0)),
                   _full((V7X_SUBLANES, E))],
        out_shape=[jax.ShapeDtypeStruct((T, ROUTE_LANES), jnp.int32),
                   jax.ShapeDtypeStruct((T, ROUTE_LANES), F32),
                   jax.ShapeDtypeStruct((V7X_SUBLANES, E), F32)],
        scratch_shapes=[pltpu.VMEM((V7X_SUBLANES, E), F32)],
        compiler_params=_cparams("arbitrary"),
    )(h2, router_w, router_b)


def _dest_kernel(route_ref, start_ref, dest_ref):
    route = route_ref[...]
    tm = route.shape[0]
    lane = lax.broadcasted_iota(jnp.int32, (tm, N_EXPERTS), 1)
    out_lane = lax.broadcasted_iota(jnp.int32, (tm, ROUTE_LANES), 1)
    dest = jnp.zeros((tm, ROUTE_LANES), jnp.int32)
    start = start_ref[...]
    for kk in range(TOP_K):
        idx = route[:, kk:kk + 1]
        base = jnp.sum(jnp.where(lane == idx, start, 0), axis=-1, keepdims=True)
        dest = jnp.where(out_lane == kk, base + route[:, TOP_K + kk:TOP_K + kk + 1], dest)
    dest_ref[...] = dest


def _dest_slots(route, pad_start):
    T = route.shape[0]
    tm = TOKEN_TILE * 2
    return pl.pallas_call(
        _dest_kernel,
        grid=(T // tm,),
        in_specs=[pl.BlockSpec((tm, ROUTE_LANES), lambda i: (i, 0)), _full((1, N_EXPERTS))],
        out_specs=pl.BlockSpec((tm, ROUTE_LANES), lambda i: (i, 0)),
        out_shape=jax.ShapeDtypeStruct((T, ROUTE_LANES), jnp.int32),
        compiler_params=_cparams("parallel"),
    )(route, pad_start.reshape(1, N_EXPERTS))


def _row_copy_wait(src_rows, dst_hbm, sem):
    n = src_rows.shape[0]
    pltpu.make_async_copy(src_rows, dst_hbm.at[pl.ds(0, n)], sem).wait()


def _dispatch_kernel(dest_hbm, h_ref, xb_in, xb_hbm, dest_s, sem_idx, sem):
    del xb_in
    i = pl.program_id(0)
    tm = h_ref.shape[0]
    idx_copy = pltpu.make_async_copy(dest_hbm.at[i], dest_s, sem_idx)
    idx_copy.start()
    idx_copy.wait()

    def body(t, carry):
        for kk in range(TOP_K):
            slot = dest_s[t * TOP_K + kk]
            pltpu.make_async_copy(h_ref.at[pl.ds(t, 1)], xb_hbm.at[pl.ds(slot, 1)], sem).start()
        return carry

    lax.fori_loop(0, tm, body, 0)
    for _ in range(TOP_K):
        _row_copy_wait(h_ref, xb_hbm, sem)


def _dispatch(dest_tiles, h2, xb_zero):
    T, D = h2.shape
    tm = TOKEN_TILE
    return pl.pallas_call(
        _dispatch_kernel,
        grid=(T // tm,),
        in_specs=[pl.BlockSpec(memory_space=pl.ANY),
                  pl.BlockSpec((tm, D), lambda i: (i, 0)),
                  pl.BlockSpec(memory_space=pl.ANY)],
        out_specs=pl.BlockSpec(memory_space=pl.ANY),
        out_shape=jax.ShapeDtypeStruct(xb_zero.shape, xb_zero.dtype),
        scratch_shapes=[pltpu.SMEM((tm * TOP_K,), jnp.int32),
                        pltpu.SemaphoreType.DMA, pltpu.SemaphoreType.DMA],
        input_output_aliases={2: 0},
        compiler_params=_cparams("arbitrary"),
    )(dest_tiles, h2, xb_zero)


def _expert_kernel(blk_e_ref, n_used_ref, x_ref, w1g_ref, w1u_ref, b1g_ref, b1u_ref,
                   w2_ref, b2_ref, o_ref):
    del blk_e_ref
    i = pl.program_id(0)

    @pl.when(i < n_used_ref[0])
    def _():
        x = x_ref[...].astype(BF16)
        gate = jnp.minimum(_dot(x, w1g_ref[...]) + b1g_ref[...], SWIGLU_LIMIT)
        up = jnp.clip(_dot(x, w1u_ref[...]) + b1u_ref[...], -SWIGLU_LIMIT, SWIGLU_LIMIT)
        act = (up + 1.0) * gate * jax.nn.sigmoid(SWIGLU_ALPHA * gate)
        o_ref[...] = _dot(act.astype(BF16), w2_ref[...]) + b2_ref[...]

    @pl.when(i >= n_used_ref[0])
    def _():
        o_ref[...] = jnp.zeros_like(o_ref)


def _experts(blk_e, n_used, xb, w1g, w1u, b1g, b1u, w2, b2):
    n_slots, D = xb.shape
    tm = MOE_TILE
    F = D_FF
    tile = pl.BlockSpec((tm, D), lambda i, be, nu: (i, 0))
    per_e = lambda shape: pl.BlockSpec((None,) + shape, lambda i, be, nu: (be[i], 0, 0))
    return pl.pallas_call(
        _expert_kernel,
        grid_spec=pltpu.PrefetchScalarGridSpec(
            num_scalar_prefetch=2,
            grid=(n_slots // tm,),
            in_specs=[tile, per_e((D, F)), per_e((D, F)), per_e((1, F)), per_e((1, F)),
                      per_e((F, D)), per_e((1, D))],
            out_specs=tile),
        out_shape=jax.ShapeDtypeStruct((n_slots, D), F32),
        compiler_params=_cparams("arbitrary"),
    )(blk_e, n_used, xb, w1g, w1u, b1g, b1u, w2, b2)


def _combine_kernel(dest_hbm, yb_hbm, h_ref, rw_ref, g_ref, b_ref, o_ref,
                    dest_s, ybuf, sem_idx, sem):
    i = pl.program_id(0)
    tm = h_ref.shape[0]
    idx_copy = pltpu.make_async_copy(dest_hbm.at[i], dest_s, sem_idx)
    idx_copy.start()
    idx_copy.wait()

    def body(t, carry):
        for kk in range(TOP_K):
            slot = dest_s[t * TOP_K + kk]
            pltpu.make_async_copy(yb_hbm.at[pl.ds(slot, 1)], ybuf.at[kk, pl.ds(t, 1)], sem).start()
        return carry

    lax.fori_loop(0, tm, body, 0)
    for kk in range(TOP_K):
        pltpu.make_async_copy(yb_hbm.at[pl.ds(0, tm)], ybuf.at[kk], sem).wait()
    rw = rw_ref[...]
    moe = rw[:, 0:1] * ybuf[0]
    for kk in range(1, TOP_K):
        moe = moe + rw[:, kk:kk + 1] * ybuf[kk]
    o_ref[...] = _layer_norm(DN_ALPHA * h_ref[...] + moe, g_ref[...], b_ref[...])


def _combine(dest_tiles, yb, h2, rw, g, b):
    T, D = h2.shape
    tm = TOKEN_TILE
    return pl.pallas_call(
        _combine_kernel,
        grid=(T // tm,),
        in_specs=[pl.BlockSpec(memory_space=pl.ANY), pl.BlockSpec(memory_space=pl.ANY),
                  pl.BlockSpec((tm, D), lambda i: (i, 0)),
                  pl.BlockSpec((tm, ROUTE_LANES), lambda i: (i, 0)),
                  _full((1, D)), _full((1, D))],
        out_specs=pl.BlockSpec((tm, D), lambda i: (i, 0)),
        out_shape=jax.ShapeDtypeStruct((T, D), F32),
        scratch_shapes=[pltpu.SMEM((tm * TOP_K,), jnp.int32),
                        pltpu.VMEM((TOP_K, tm, D), F32),
                        pltpu.SemaphoreType.DMA, pltpu.SemaphoreType.DMA],
        compiler_params=_cparams("arbitrary"),
    )(dest_tiles, yb, h2, rw, g, b)


def _s5_discretise(a_re, a_im, b_re, b_im, c_re, c_im, log_dt):
    G, P, C = S5_GROUPS, S5_STATE, S5_CH
    dt = jnp.exp(log_dt)[:, None]
    mag = jnp.exp(a_re * dt)
    abar_re = mag * jnp.cos(a_im * dt)
    abar_im = mag * jnp.sin(a_im * dt)
    den = a_re * a_re + a_im * a_im
    num_re = abar_re - 1.0
    coef_re = (num_re * a_re + abar_im * a_im) / den
    coef_im = (abar_im * a_re - num_re * a_im) / den
    bbar_re = coef_re[..., None] * b_re - coef_im[..., None] * b_im
    bbar_im = coef_re[..., None] * b_im + coef_im[..., None] * b_re
    eye = jnp.eye(G, dtype=F32)
    to_in = lambda m: jnp.einsum('gpc,gh->gchp', m, eye).reshape(G * C, G * P)
    to_out = lambda m: jnp.einsum('gcp,gh->gphc', m, eye).reshape(G * P, G * C)
    return (to_in(bbar_re).astype(BF16), to_in(bbar_im).astype(BF16),
            abar_re.reshape(1, G * P), abar_im.reshape(1, G * P),
            to_out(c_re).astype(BF16), to_out(c_im).astype(BF16))


def _moe_layer(h2, router_w, router_b, w1, b1, w2, b2, g, b):
    T, D = h2.shape
    E = N_EXPERTS
    tm = TOKEN_TILE
    route, rw, counts = _router(h2, router_w, router_b.reshape(1, E))
    counts = counts[0].astype(jnp.int32)
    padded = (counts + MOE_TILE - 1) // MOE_TILE * MOE_TILE
    pad_end = jnp.cumsum(padded)
    pad_start = pad_end - padded
    n_tiles = T * TOP_K // MOE_TILE + E
    n_slots = n_tiles * MOE_TILE
    tile_start = jnp.arange(n_tiles, dtype=jnp.int32) * MOE_TILE
    blk_e = jnp.minimum(jnp.sum(tile_start[:, None] >= pad_end[None, :], axis=1), E - 1).astype(jnp.int32)
    n_used = (pad_end[E - 1:] // MOE_TILE).astype(jnp.int32)
    dest = _dest_slots(route, pad_start.astype(jnp.int32))
    dest_tiles = dest[:, :TOP_K].reshape(T // tm, tm * TOP_K)
    xb = _dispatch(dest_tiles, h2, jnp.zeros((n_slots, D), F32))
    yb = _experts(blk_e, n_used, xb,
                  w1[:, :, 0::2].astype(BF16), w1[:, :, 1::2].astype(BF16),
                  b1[:, None, 0::2], b1[:, None, 1::2], w2.astype(BF16), b2[:, None, :])
    return _combine(dest_tiles, yb, h2, rw, g.reshape(1, D), b.reshape(1, D))


def kernel(x, mem, ln_in_g, ln_in_b, w_in, conv_w, sg_norm_g, sg_norm_b, sg_w, sg_b, rw_mu, rw_w0, rw_w_up, rw_a0, rw_a_up, rw_g_up, rw_k_k, rw_k_a, rw_r_k, rw_ln_g, rw_ln_b, s5_a_re, s5_a_im, s5_b_re, s5_b_im, s5_c_re, s5_c_im, s5_d, s5_log_dt, s5_glu_w, s5_glu_b, br_proj, gate_b, w_out, ln1_g, ln1_b, xa_wq, xa_wk, xa_wv, xa_wo, ln2_g, ln2_b, router_w, router_b, ex_w1, ex_b1, ex_w2, ex_b2, ln3_g, ln3_b):
    B, S, D = x.shape
    M = mem.shape[1]
    T = B * S
    W = BRANCH_W
    row = lambda v: v.reshape(1, -1)
    pos = jnp.arange(SG_BLOCK)
    sg_mask = (pos[None, :] // CHUNK) <= (pos[:, None] // CHUNK)
    head_ones = jnp.kron(jnp.eye(RW_HEADS, dtype=F32), jnp.ones((RW_HEAD_DIM, RW_HEAD_DIM), F32))
    mem2 = mem.reshape(B * M, D)

    h = _input_ln(x.reshape(T, D), ln_in_g, ln_in_b)
    for l in range(DEPTH):
        w_l = w_in[l].astype(BF16)
        h3 = h.reshape(B, S, D)
        sg_wm = jnp.where(sg_mask[None], sg_w[l], 0.0).astype(BF16)
        sg_bias = jnp.repeat(sg_b[l].T, W // SG_GROUPS, axis=1)
        o_a, o_b = _mixers_ab(h3, w_l[:, :OFF_C], conv_w[l], row(sg_norm_g[l]), row(sg_norm_b[l]),
                              sg_wm, sg_bias)
        zero = jnp.zeros((RW_LORA, W), F32)
        lora_w = jnp.concatenate([jnp.concatenate([rw_w_up[l], zero], axis=1),
                                  jnp.concatenate([zero, rw_a_up[l]], axis=1)], axis=0)
        o_c = _mixer_rwkv(h3, w_l[:, OFF_C:OFF_D], row(rw_mu[l]), row(rw_w0[l]), row(rw_a0[l]),
                          lora_w, rw_g_up[l], row(rw_k_k[l]), row(rw_k_a[l]), row(rw_r_k[l]),
                          row(rw_ln_g[l]), row(rw_ln_b[l]), head_ones)
        s5p = _s5_discretise(s5_a_re[l], s5_a_im[l], s5_b_re[l], s5_b_im[l], s5_c_re[l],
                             s5_c_im[l], s5_log_dt[l])
        o_d = _mixer_s5(h3, w_l[:, OFF_D:OFF_G], *s5p, row(s5_d[l]),
                        s5_glu_w[l].astype(BF16), row(s5_glu_b[l]))
        flat = lambda o: o.reshape(T, W)
        h = _merge(h, flat(o_a), flat(o_b), flat(o_c), flat(o_d), w_l[:, OFF_G:], gate_b[l],
                   br_proj[l].astype(BF16), w_out[l].astype(BF16), row(ln1_g[l]), row(ln1_b[l]))
        kmem, vmem = _kv_proj(mem2, xa_wk[l].astype(BF16), xa_wv[l].astype(BF16))
        h = _cross_attention(h.reshape(B, S, D), kmem.reshape(B, M, D), vmem.reshape(B, M, D),
                             xa_wq[l].astype(BF16), xa_wo[l].astype(BF16),
                             row(ln2_g[l]), row(ln2_b[l])).reshape(T, D)
        h = _moe_layer(h, router_w[l], router_b[l], ex_w1[l], ex_b1[l], ex_w2[l], ex_b2[l],
                       ln3_g[l], ln3_b[l])
    return h.reshape(B, S, D)
```

```python
import functools
import math

import jax
import jax.numpy as jnp
from jax import lax
from jax.experimental import pallas as pl
from jax.experimental.pallas import tpu as pltpu

F32 = jnp.float32
BF16 = jnp.bfloat16
HIGHEST = lax.Precision.HIGHEST

D_MODEL = 1024
DEPTH = 2
CHUNK = 64
BRANCH_W = 256
N_BRANCH = 4
CONV_W = 3
SG_BLOCK = 128
SG_GROUPS = 4
RW_HEADS = 4
RW_HEAD_DIM = 64
RW_LORA = 64
RW_GATE_LORA = 128
RW_IN = 3 * BRANCH_W + 2 * RW_LORA + RW_GATE_LORA
RW_LN_EPS = 64e-5
S5_CH = 16
S5_GROUPS = BRANCH_W // S5_CH
S5_STATE = 64
S5_N = S5_GROUPS * S5_STATE
OFF_B = 3 * BRANCH_W
OFF_C = OFF_B + 2 * BRANCH_W
OFF_D = OFF_C + RW_IN
OFF_G = OFF_D + BRANCH_W
XA_HEADS = 4
XA_HEAD_DIM = D_MODEL // XA_HEADS
N_EXPERTS = 32
TOP_K = 4
D_FF = D_MODEL
SWIGLU_LIMIT = 7.0
SWIGLU_ALPHA = 1.702
LN_EPS = 1e-5
DN_ALPHA = (2 * DEPTH) ** 0.25

V7X_LANES = 128
V7X_SUBLANES = 8
V7X_VMEM_LIMIT_BYTES = 56 * 1024 * 1024
TOKEN_TILE = 256
SEQ_TILE = 256
RW_CHUNK = 64
MOE_TILE = 256
ROUTE_LANES = 128


def _cparams(*sem):
    return pltpu.CompilerParams(dimension_semantics=sem,
                                vmem_limit_bytes=V7X_VMEM_LIMIT_BYTES)


def _full(shape):
    nd = len(shape)
    return pl.BlockSpec(shape, lambda *_: (0,) * nd)


def _layer_norm(x, g, b, eps=LN_EPS):
    mu = jnp.mean(x, axis=-1, keepdims=True)
    xc = x - mu
    var = jnp.mean(xc * xc, axis=-1, keepdims=True)
    return xc * lax.rsqrt(var + eps) * g + b


def _contract(a, b, dims, **kw):
    return lax.dot_general(a, b, (dims, ((), ())), preferred_element_type=F32, **kw)


def _dot(a, b, **kw):
    return _contract(a, b, ((1,), (0,)), **kw)


def _dot_nt(a, b, **kw):
    return _contract(a, b, ((1,), (1,)), **kw)


def _dot_tn(a, b, **kw):
    return _contract(a, b, ((0,), (0,)), **kw)


def _shift_rows(x, n, tail):
    rolled = pltpu.roll(x, n, 0)
    row = lax.broadcasted_iota(jnp.int32, x.shape, 0)
    out = rolled
    for i in range(n):
        src = tail[V7X_SUBLANES - n + i:V7X_SUBLANES - n + i + 1, :]
        out = jnp.where(row == i, src, out)
    return out


def _ln_kernel(x_ref, g_ref, b_ref, o_ref):
    o_ref[...] = _layer_norm(x_ref[...], g_ref[...], b_ref[...])


def _input_ln(x2, g, b):
    T, D = x2.shape
    tm = TOKEN_TILE * 2
    return pl.pallas_call(
        _ln_kernel,
        grid=(T // tm,),
        in_specs=[pl.BlockSpec((tm, D), lambda i: (i, 0)), _full((1, D)), _full((1, D))],
        out_specs=pl.BlockSpec((tm, D), lambda i: (i, 0)),
        out_shape=jax.ShapeDtypeStruct((T, D), F32),
        compiler_params=_cparams("parallel"),
    )(x2, g.reshape(1, D), b.reshape(1, D))


def _ab_kernel(h_ref, w_ref, cw_ref, ng_ref, nb_ref, sw_ref, sb_ref,
               oa_ref, ob_ref, tail_ref):
    @pl.when(pl.program_id(1) == 0)
    def _():
        tail_ref[...] = jnp.zeros_like(tail_ref)

    tm = h_ref.shape[0]
    W = BRANCH_W
    z = _dot(h_ref[...].astype(BF16), w_ref[...])
    ch = z[:, W:2 * W] * z[:, 2 * W:3 * W]
    tail = tail_ref[...]
    s1 = _shift_rows(ch, 1, tail)
    s2 = _shift_rows(ch, 2, tail)
    tail_ref[...] = ch[tm - V7X_SUBLANES:, :]
    cw = cw_ref[...]
    y = cw[2:3, :] * ch + cw[1:2, :] * s1 + cw[0:1, :] * s2
    oa_ref[...] = (z[:, 0:W] * y).astype(oa_ref.dtype)
    u = z[:, 3 * W:4 * W]
    v = _layer_norm(z[:, 4 * W:5 * W], ng_ref[...], nb_ref[...]).astype(BF16)
    gw = W // SG_GROUPS
    grp = lax.broadcasted_iota(jnp.int32, (SG_BLOCK, W), 1) // gw
    for blk in range(tm // SG_BLOCK):
        rows = slice(blk * SG_BLOCK, (blk + 1) * SG_BLOCK)
        vb = v[rows, :]
        sv = sb_ref[...]
        for g in range(SG_GROUPS):
            sv = sv + jnp.where(grp == g, _dot(sw_ref[g], vb), 0.0)
        ob_ref[rows, :] = (u[rows, :] * sv).astype(ob_ref.dtype)


def _mixers_ab(h, w_ab, conv_w, ng, nb, sg_wm, sg_bias):
    B, S, D = h.shape
    tm = SEQ_TILE
    W = BRANCH_W
    blk = lambda b, j: (b, j, 0)
    return pl.pallas_call(
        _ab_kernel,
        grid=(B, S // tm),
        in_specs=[pl.BlockSpec((None, tm, D), blk),
                  _full((D, 5 * W)), _full((CONV_W, W)), _full((1, W)), _full((1, W)),
                  _full((SG_GROUPS, SG_BLOCK, SG_BLOCK)), _full((SG_BLOCK, W))],
        out_specs=[pl.BlockSpec((None, tm, W), blk), pl.BlockSpec((None, tm, W), blk)],
        out_shape=[jax.ShapeDtypeStruct((B, S, W), BF16)] * 2,
        scratch_shapes=[pltpu.VMEM((V7X_SUBLANES, W), F32)],
        compiler_params=_cparams("parallel", "arbitrary"),
    )(h, w_ab, conv_w, ng, nb, sg_wm, sg_bias)


def _softplus(x):
    return jnp.maximum(x, 0.0) + jnp.log(1.0 + jnp.exp(-jnp.abs(x)))


def _rwkv_kernel(h_ref, w_ref, mu_ref, w0_ref, a0_ref, lora_ref, gup_ref, kk_ref, ka_ref,
                 rk_ref, lng_ref, lnb_ref, ones_ref, o_ref,
                 tail_ref, state_ref, v_s, y_s, gend_s, ar_s, bk_s, inv_s, rb_s, mk_s):
    @pl.when(pl.program_id(1) == 0)
    def _():
        tail_ref[...] = jnp.zeros_like(tail_ref)
        state_ref[...] = jnp.zeros_like(state_ref)

    tm = h_ref.shape[0]
    W = BRANCH_W
    N = RW_HEAD_DIM
    L = RW_CHUNK
    z = _dot(h_ref[...].astype(BF16), w_ref[...])
    zprev = _shift_rows(z, 1, tail_ref[...])
    tail_ref[...] = z[tm - V7X_SUBLANES:, :]
    z = z + (zprev - z) * mu_ref[...]
    r = z[:, 0:W]
    k = z[:, W:2 * W]
    v = z[:, 2 * W:3 * W]
    xwa = z[:, 3 * W:3 * W + 2 * RW_LORA]
    xg = z[:, 3 * W + 2 * RW_LORA:]
    lane = lax.broadcasted_iota(jnp.int32, xwa.shape, 1)
    lora_in = jnp.where(lane < RW_LORA, jnp.tanh(xwa), xwa)
    lora = _dot(lora_in, lora_ref[...], precision=HIGHEST)
    w = -_softplus(-(w0_ref[...] + lora[:, 0:W])) - 0.5
    log_decay = -jnp.exp(w)
    a = jax.nn.sigmoid(a0_ref[...] + lora[:, W:2 * W])
    g = _dot(jax.nn.sigmoid(xg), gup_ref[...], precision=HIGHEST)
    ones = ones_ref[...]
    kk = k * kk_ref[...]
    kk = kk / jnp.maximum(jnp.sqrt(_dot(kk * kk, ones, precision=HIGHEST)), 1e-12)
    k = k * (1.0 + (a - 1.0) * ka_ref[...])
    bonus = _dot(r * k * rk_ref[...], ones, precision=HIGHEST) * v

    ti = lax.broadcasted_iota(jnp.int32, (L, L), 0)
    tj = lax.broadcasted_iota(jnp.int32, (L, L), 1)
    tri = (ti >= tj).astype(F32)
    eye = (ti == tj).astype(F32)
    t2 = lax.broadcasted_iota(jnp.int32, (2 * L, L), 0)
    s2 = lax.broadcasted_iota(jnp.int32, (2 * L, L), 1)
    mask2 = ((t2 < L) & (t2 > s2)) | (t2 - L >= s2)
    n_chunks = tm // L

    alpha = -kk
    beta = kk * a
    v_s[...] = v.astype(BF16)
    for c in range(n_chunks):
        rows = slice(c * L, (c + 1) * L)
        ld = log_decay[rows, :]
        cum = _dot(tri, ld, precision=HIGHEST)
        gam = jnp.exp(cum)
        gam_inv = jnp.exp(-cum)
        gend_s[c * V7X_SUBLANES:(c + 1) * V7X_SUBLANES, :] = jnp.broadcast_to(
            gam[L - 1:L, :], (V7X_SUBLANES, W))
        ar_all = jnp.concatenate([alpha[rows, :] * jnp.exp(cum - ld), r[rows, :] * gam],
                                 axis=0).astype(BF16)
        bk_all = jnp.concatenate([beta[rows, :] * gam_inv, k[rows, :] * gam_inv],
                                 axis=0).astype(BF16)
        ar_s[c] = ar_all
        bk_s[c] = bk_all

    heads = [(c, hd) for c in range(n_chunks) for hd in range(RW_HEADS)]
    hcols = lambda hd: slice(hd * N, (hd + 1) * N)
    invs, ps = [], []
    for c, hd in heads:
        pair = _dot_nt(ar_s[c, :, hcols(hd)], bk_s[c, :, hcols(hd)])
        m_b = jnp.where(mask2, pair[:, :L], 0.0)
        mk_s[c * RW_HEADS + hd] = jnp.where(mask2, pair[:, L:], 0.0).astype(BF16)
        rb_s[c * RW_HEADS + hd] = m_b[L:].astype(BF16)
        invs.append(eye + m_b[:L])
        ps.append(m_b[:L].astype(BF16))
    ps = [_dot(pb, pb) for pb in ps]
    for _ in range(int(math.log2(L)) - 2):
        both = [_dot(jnp.concatenate([p.astype(BF16), inv.astype(BF16)], axis=0), p.astype(BF16))
                for p, inv in zip(ps, invs)]
        ps = [b2[:L] for b2 in both]
        invs = [inv + b2[L:] for inv, b2 in zip(invs, both)]
    for i, (inv, p) in enumerate(zip(invs, ps)):
        inv_s[i] = (inv + _dot(inv.astype(BF16), p.astype(BF16))).astype(BF16)

    def chunk(c, carry):
        rows = pl.ds(pl.multiple_of(c * L, L), L)
        ar_all = ar_s[c]
        bk_all = bk_s[c]
        v_all = v_s[rows, :]
        gend = gend_s[pl.ds(pl.multiple_of(c * V7X_SUBLANES, V7X_SUBLANES), 1), :]
        hs = range(RW_HEADS)
        sts = [state_ref[hd] for hd in hs]
        bases = [_dot_nt(ar_all[:, hcols(hd)], sts[hd].astype(BF16))
                 + _dot(mk_s[c * RW_HEADS + hd], v_all[:, hcols(hd)]) for hd in hs]
        us = [_dot(inv_s[c * RW_HEADS + hd], bases[hd][:L].astype(BF16)).astype(BF16) for hd in hs]
        for hd in hs:
            y_s[rows, hcols(hd)] = bases[hd][L:] + _dot(rb_s[c * RW_HEADS + hd], us[hd])
        for hd in hs:
            st = sts[hd] + _dot_tn(jnp.concatenate([us[hd], v_all[:, hcols(hd)]], axis=0),
                                   bk_all[:, hcols(hd)])
            state_ref[hd] = st * gend[:, hcols(hd)]
        return carry

    lax.fori_loop(0, n_chunks, chunk, 0)

    y = y_s[...]
    inv_n = 1.0 / N
    m = _dot(y, ones, precision=HIGHEST) * inv_n
    yc = y - m
    var = _dot(yc * yc, ones, precision=HIGHEST) * inv_n
    yn = yc * lax.rsqrt(var + RW_LN_EPS) * lng_ref[...] + lnb_ref[...]
    o_ref[...] = ((yn + bonus) * g).astype(o_ref.dtype)


def _mixer_rwkv(h, w_c, mu, w0, a0, lora_w, g_up, k_k, k_a, r_k, ln_g, ln_b, head_ones):
    B, S, D = h.shape
    tm = SEQ_TILE
    W = BRANCH_W
    L = RW_CHUNK
    nc = tm // L
    blk = lambda b, j: (b, j, 0)
    row = lambda n: _full((1, n))
    return pl.pallas_call(
        _rwkv_kernel,
        grid=(B, S // tm),
        in_specs=[pl.BlockSpec((None, tm, D), blk), _full((D, RW_IN)), row(RW_IN), row(W), row(W),
                  _full((2 * RW_LORA, 2 * W)), _full((RW_GATE_LORA, W)), row(W), row(W), row(W),
                  row(W), row(W), _full((W, W))],
        out_specs=pl.BlockSpec((None, tm, W), blk),
        out_shape=jax.ShapeDtypeStruct((B, S, W), BF16),
        scratch_shapes=[pltpu.VMEM((V7X_SUBLANES, RW_IN), F32),
                        pltpu.VMEM((RW_HEADS, RW_HEAD_DIM, RW_HEAD_DIM), F32),
                        pltpu.VMEM((tm, W), BF16),
                        pltpu.VMEM((tm, W), F32),
                        pltpu.VMEM((nc * V7X_SUBLANES, W), F32),
                        pltpu.VMEM((nc, 2 * L, W), BF16),
                        pltpu.VMEM((nc, 2 * L, W), BF16),
                        pltpu.VMEM((nc * RW_HEADS, L, L), BF16),
                        pltpu.VMEM((nc * RW_HEADS, L, L), BF16),
                        pltpu.VMEM((nc * RW_HEADS, 2 * L, L), BF16)],
        compiler_params=_cparams("parallel", "arbitrary"),
    )(h, w_c, mu, w0, a0, lora_w, g_up, k_k, k_a, r_k, ln_g, ln_b, head_ones)


def _s5_kernel(h_ref, w_ref, bre_ref, bim_ref, are_ref, aim_ref, cre_ref, cim_ref,
               d_ref, gw_ref, gb_ref, o_ref, xr_s, xi_s, st_s):
    @pl.when(pl.program_id(1) == 0)
    def _():
        st_s[...] = jnp.zeros_like(st_s)

    tm = h_ref.shape[0]
    u = _dot(h_ref[...].astype(BF16), w_ref[...])
    ub = u.astype(BF16)
    xr_s[...] = _dot(ub, bre_ref[...])
    xi_s[...] = _dot(ub, bim_ref[...])
    ar = are_ref[...]
    ai = aim_ref[...]

    def step(t, carry):
        xr, xi = carry
        row = pl.ds(t, 1)
        nr = ar * xr - ai * xi + xr_s[row, :]
        ni = ar * xi + ai * xr + xi_s[row, :]
        xr_s[row, :] = nr
        xi_s[row, :] = ni
        return nr, ni

    xr, xi = lax.fori_loop(0, tm, step, (st_s[0:1, :], st_s[1:2, :]), unroll=8)
    st_s[0:1, :] = xr
    st_s[1:2, :] = xi
    y = (_dot(xr_s[...].astype(BF16), cre_ref[...]) - _dot(xi_s[...].astype(BF16), cim_ref[...])
         + d_ref[...] * u)
    y = jax.nn.gelu(y)
    gate = jax.nn.sigmoid(_dot(y.astype(BF16), gw_ref[...]) + gb_ref[...])
    o_ref[...] = (y * gate).astype(o_ref.dtype)


def _mixer_s5(h, w_d, bre, bim, are, aim, cre, cim, d, glu_w, glu_b):
    B, S, D = h.shape
    tm = SEQ_TILE
    W = BRANCH_W
    blk = lambda b, j: (b, j, 0)
    return pl.pallas_call(
        _s5_kernel,
        grid=(B, S // tm),
        in_specs=[pl.BlockSpec((None, tm, D), blk), _full((D, W)),
                  _full((W, S5_N)), _full((W, S5_N)), _full((1, S5_N)), _full((1, S5_N)),
                  _full((S5_N, W)), _full((S5_N, W)), _full((1, W)), _full((W, W)), _full((1, W))],
        out_specs=pl.BlockSpec((None, tm, W), blk),
        out_shape=jax.ShapeDtypeStruct((B, S, W), BF16),
        scratch_shapes=[pltpu.VMEM((tm, S5_N), F32), pltpu.VMEM((tm, S5_N), F32),
                        pltpu.VMEM((V7X_SUBLANES, S5_N), F32)],
        compiler_params=_cparams("parallel", "arbitrary"),
    )(h, w_d, bre, bim, are, aim, cre, cim, d, glu_w, glu_b)


def _merge_kernel(h_ref, oa_ref, ob_ref, oc_ref, od_ref, wg_ref, gb_ref, br_ref, wo_ref,
                  g_ref, b_ref, o_ref):
    D = D_MODEL
    h = h_ref[...]
    hb = h.astype(BF16)
    merged = None
    for i, br_in in enumerate((oa_ref, ob_ref, oc_ref, od_ref)):
        gate = jax.nn.sigmoid(_dot(hb, wg_ref[:, i * D:(i + 1) * D]) + gb_ref[i:i + 1, :])
        term = gate * _dot(br_in[...], br_ref[i])
        merged = term if merged is None else merged + term
    y = _dot(merged.astype(BF16), wo_ref[...])
    o_ref[...] = _layer_norm(DN_ALPHA * h + y, g_ref[...], b_ref[...])


def _merge(h2, oa, ob, oc, od, w_g, gate_b, br_proj, w_out, g, b):
    T, D = h2.shape
    tm = TOKEN_TILE
    W = BRANCH_W
    tok = lambda n: pl.BlockSpec((tm, n), lambda i: (i, 0))
    return pl.pallas_call(
        _merge_kernel,
        grid=(T // tm,),
        in_specs=[tok(D), tok(W), tok(W), tok(W), tok(W),
                  _full((D, N_BRANCH * D)), _full((N_BRANCH, D)), _full((N_BRANCH, W, D)),
                  _full((D, D)), _full((1, D)), _full((1, D))],
        out_specs=tok(D),
        out_shape=jax.ShapeDtypeStruct((T, D), F32),
        compiler_params=_cparams("parallel"),
    )(h2, oa, ob, oc, od, w_g, gate_b, br_proj, w_out, g, b)


def _kv_kernel(m_ref, wk_ref, wv_ref, k_ref, v_ref):
    mb = m_ref[...].astype(BF16)
    k_ref[...] = _dot(mb, wk_ref[...]).astype(k_ref.dtype)
    v_ref[...] = _dot(mb, wv_ref[...]).astype(v_ref.dtype)


def _kv_proj(mem2, wk, wv):
    R, D = mem2.shape
    tm = TOKEN_TILE
    tok = pl.BlockSpec((tm, D), lambda i: (i, 0))
    return pl.pallas_call(
        _kv_kernel,
        grid=(R // tm,),
        in_specs=[tok, _full((D, D)), _full((D, D))],
        out_specs=[tok, tok],
        out_shape=[jax.ShapeDtypeStruct((R, D), BF16)] * 2,
        compiler_params=_cparams("parallel"),
    )(mem2, wk, wv)


def _attn_kernel(h_ref, wq_ref, k_ref, v_ref, wo_ref, g_ref, b_ref, o_ref):
    h = h_ref[...]
    q = _dot(h.astype(BF16), wq_ref[...]).astype(BF16)
    scale = XA_HEAD_DIM ** -0.5
    outs = []
    for hd in range(XA_HEADS):
        cols = slice(hd * XA_HEAD_DIM, (hd + 1) * XA_HEAD_DIM)
        s = _dot_nt(q[:, cols], k_ref[:, cols]) * scale
        s = s - jnp.max(s, axis=-1, keepdims=True)
        p = jnp.exp(s)
        p = p / jnp.sum(p, axis=-1, keepdims=True)
        outs.append(_dot(p.astype(BF16), v_ref[:, cols]))
    o = jnp.concatenate(outs, axis=-1).astype(BF16)
    y = _dot(o, wo_ref[...])
    o_ref[...] = _layer_norm(DN_ALPHA * h + y, g_ref[...], b_ref[...])


def _cross_attention(h, kmem, vmem, wq, wo, g, b):
    B, S, D = h.shape
    M = kmem.shape[1]
    tm = TOKEN_TILE
    blk = lambda bi, j: (bi, j, 0)
    mem = lambda bi, j: (bi, 0, 0)
    return pl.pallas_call(
        _attn_kernel,
        grid=(B, S // tm),
        in_specs=[pl.BlockSpec((None, tm, D), blk), _full((D, D)),
                  pl.BlockSpec((None, M, D), mem), pl.BlockSpec((None, M, D), mem),
                  _full((D, D)), _full((1, D)), _full((1, D))],
        out_specs=pl.BlockSpec((None, tm, D), blk),
        out_shape=jax.ShapeDtypeStruct((B, S, D), F32),
        compiler_params=_cparams("parallel", "parallel"),
    )(h, wq, kmem, vmem, wo, g, b)


def _router_kernel(h_ref, w_ref, b_ref, route_ref, rw_ref, cnt_ref, carry_ref):
    @pl.when(pl.program_id(0) == 0)
    def _():
        carry_ref[...] = jnp.zeros_like(carry_ref)

    tm = h_ref.shape[0]
    E = N_EXPERTS
    logits = _dot(h_ref[...], w_ref[...], precision=HIGHEST) + b_ref[...]
    lane = lax.broadcasted_iota(jnp.int32, (tm, E), 1)
    work = logits
    vals, hots, idxs = [], [], []
    for _ in range(TOP_K):
        m = jnp.max(work, axis=-1, keepdims=True)
        idx = jnp.min(jnp.where(work == m, lane, E), axis=-1, keepdims=True)
        hot = lane == idx
        work = jnp.where(hot, -jnp.inf, work)
        vals.append(m)
        hots.append(hot)
        idxs.append(idx)
    exps = [jnp.exp(v - vals[0]) for v in vals]
    denom = exps[0] + exps[1] + exps[2] + exps[3]
    multi = (hots[0] | hots[1] | hots[2] | hots[3]).astype(F32)
    ti = lax.broadcasted_iota(jnp.int32, (tm, tm), 0)
    tj = lax.broadcasted_iota(jnp.int32, (tm, tm), 1)
    before = (ti > tj).astype(BF16)
    prefix = _dot(before, multi.astype(BF16)) + carry_ref[0:1, :]
    out_lane = lax.broadcasted_iota(jnp.int32, (tm, ROUTE_LANES), 1)
    route = jnp.zeros((tm, ROUTE_LANES), jnp.int32)
    rw = jnp.zeros((tm, ROUTE_LANES), F32)
    for kk in range(TOP_K):
        rank = jnp.sum(jnp.where(hots[kk], prefix, 0.0), axis=-1, keepdims=True).astype(jnp.int32)
        route = jnp.where(out_lane == kk, idxs[kk], route)
        route = jnp.where(out_lane == TOP_K + kk, rank, route)
        rw = jnp.where(out_lane == kk, exps[kk] / denom, rw)
    route_ref[...] = route
    rw_ref[...] = rw
    carry_ref[0:1, :] = carry_ref[0:1, :] + jnp.sum(multi, axis=0, keepdims=True)
    cnt_ref[...] = carry_ref[...]


def _router(h2, router_w, router_b):
    T, D = h2.shape
    tm = TOKEN_TILE
    E = N_EXPERTS
    return pl.pallas_call(
        _router_kernel,
        grid=(T // tm,),
        in_specs=[pl.BlockSpec((tm, D), lambda i: (i, 0)), _full((D, E)), _full((1, E))],
        out_specs=[pl.BlockSpec((tm, ROUTE_LANES), lambda i: (i, 0)),
                   pl.BlockSpec((tm, ROUTE_LANES), lambda i: (i, 0)),
                   _full((V7X_SUBLANES, E))],
        out_shape=[jax.ShapeDtypeStruct((T, ROUTE_LANES), jnp.int32),
                   jax.ShapeDtypeStruct((T, ROUTE_LANES), F32),
                   jax.ShapeDtypeStruct((V7X_SUBLANES, E), F32)],
        scratch_shapes=[pltpu.VMEM((V7X_SUBLANES, E), F32)],
        compiler_params=_cparams("arbitrary"),
    )(h2, router_w, router_b)


def _dest_kernel(route_ref, start_ref, dest_ref):
    route = route_ref[...]
    tm = route.shape[0]
    lane = lax.broadcasted_iota(jnp.int32, (tm, N_EXPERTS), 1)
    out_lane = lax.broadcasted_iota(jnp.int32, (tm, ROUTE_LANES), 1)
    dest = jnp.zeros((tm, ROUTE_LANES), jnp.int32)
    start = start_ref[...]
    for kk in range(TOP_K):
        idx = route[:, kk:kk + 1]
        base = jnp.sum(jnp.where(lane == idx, start, 0), axis=-1, keepdims=True)
        dest = jnp.where(out_lane == kk, base + route[:, TOP_K + kk:TOP_K + kk + 1], dest)
    dest_ref[...] = dest


def _dest_slots(route, pad_start):
    T = route.shape[0]
    tm = TOKEN_TILE * 2
    return pl.pallas_call(
        _dest_kernel,
        grid=(T // tm,),
        in_specs=[pl.BlockSpec((tm, ROUTE_LANES), lambda i: (i, 0)), _full((1, N_EXPERTS))],
        out_specs=pl.BlockSpec((tm, ROUTE_LANES), lambda i: (i, 0)),
        out_shape=jax.ShapeDtypeStruct((T, ROUTE_LANES), jnp.int32),
        compiler_params=_cparams("parallel"),
    )(route, pad_start.reshape(1, N_EXPERTS))


def _row_copy_wait(src_rows, dst_hbm, sem):
    n = src_rows.shape[0]
    pltpu.make_async_copy(src_rows, dst_hbm.at[pl.ds(0, n)], sem).wait()


def _dispatch_kernel(dest_hbm, h_ref, xb_in, xb_hbm, dest_s, sem_idx, sem):
    del xb_in
    i = pl.program_id(0)
    tm = h_ref.shape[0]
    idx_copy = pltpu.make_async_copy(dest_hbm.at[i], dest_s, sem_idx)
    idx_copy.start()
    idx_copy.wait()

    def body(t, carry):
        for kk in range(TOP_K):
            slot = dest_s[t * TOP_K + kk]
            pltpu.make_async_copy(h_ref.at[pl.ds(t, 1)], xb_hbm.at[pl.ds(slot, 1)], sem).start()
        return carry

    lax.fori_loop(0, tm, body, 0)
    for _ in range(TOP_K):
        _row_copy_wait(h_ref, xb_hbm, sem)


def _dispatch(dest_tiles, h2, xb_zero):
    T, D = h2.shape
    tm = TOKEN_TILE
    return pl.pallas_call(
        _dispatch_kernel,
        grid=(T // tm,),
        in_specs=[pl.BlockSpec(memory_space=pl.ANY),
                  pl.BlockSpec((tm, D), lambda i: (i, 0)),
                  pl.BlockSpec(memory_space=pl.ANY)],
        out_specs=pl.BlockSpec(memory_space=pl.ANY),
        out_shape=jax.ShapeDtypeStruct(xb_zero.shape, xb_zero.dtype),
        scratch_shapes=[pltpu.SMEM((tm * TOP_K,), jnp.int32),
                        pltpu.SemaphoreType.DMA, pltpu.SemaphoreType.DMA],
        input_output_aliases={2: 0},
        compiler_params=_cparams("arbitrary"),
    )(dest_tiles, h2, xb_zero)


GU_BLOCK = 2 * V7X_LANES


def _unzip_kernel(w_ref, p_ref, o_ref):
    for c in range(w_ref.shape[1] // GU_BLOCK):
        cols = slice(c * GU_BLOCK, (c + 1) * GU_BLOCK)
        o_ref[:, cols] = _dot(w_ref[:, cols].astype(BF16), p_ref[...]).astype(o_ref.dtype)


def _unzip_gate_up(w1):
    E, D, F2 = w1.shape
    tr = TOKEN_TILE
    j = jnp.arange(GU_BLOCK)
    perm = (j[:, None] == jnp.where(j < V7X_LANES, 2 * j, 2 * (j - V7X_LANES) + 1)[None, :])
    blk = pl.BlockSpec((None, tr, F2), lambda e, r: (e, r, 0))
    return pl.pallas_call(
        _unzip_kernel,
        grid=(E, D // tr),
        in_specs=[blk, _full((GU_BLOCK, GU_BLOCK))],
        out_specs=blk,
        out_shape=jax.ShapeDtypeStruct((E, D, F2), BF16),
        compiler_params=_cparams("parallel", "parallel"),
    )(w1, perm.astype(BF16))


def _unzip_bias(b1):
    E, F2 = b1.shape
    return b1.reshape(E, F2 // GU_BLOCK, V7X_LANES, 2).transpose(0, 1, 3, 2).reshape(E, 1, F2)


def _expert_kernel(blk_e_ref, n_used_ref, x_ref, w1_ref, b1_ref, w2_ref, b2_ref, o_ref):
    del blk_e_ref
    i = pl.program_id(0)

    @pl.when(i < n_used_ref[0])
    def _():
        x = x_ref[...].astype(BF16)
        hid = _dot(x, w1_ref[...]) + b1_ref[...]
        acts = []
        for c in range(hid.shape[1] // GU_BLOCK):
            gate = jnp.minimum(hid[:, c * GU_BLOCK:c * GU_BLOCK + V7X_LANES], SWIGLU_LIMIT)
            up = jnp.clip(hid[:, c * GU_BLOCK + V7X_LANES:(c + 1) * GU_BLOCK],
                          -SWIGLU_LIMIT, SWIGLU_LIMIT)
            acts.append(((up + 1.0) * gate * jax.nn.sigmoid(SWIGLU_ALPHA * gate)).astype(BF16))
        act = jnp.concatenate(acts, axis=-1)
        o_ref[...] = _dot(act, w2_ref[...]) + b2_ref[...]

    @pl.when(i >= n_used_ref[0])
    def _():
        o_ref[...] = jnp.zeros_like(o_ref)


def _experts(blk_e, n_used, xb, w1, b1, w2, b2):
    n_slots, D = xb.shape
    tm = MOE_TILE
    F = D_FF
    tile = pl.BlockSpec((tm, D), lambda i, be, nu: (i, 0))
    per_e = lambda shape: pl.BlockSpec((None,) + shape, lambda i, be, nu: (be[i], 0, 0))
    return pl.pallas_call(
        _expert_kernel,
        grid_spec=pltpu.PrefetchScalarGridSpec(
            num_scalar_prefetch=2,
            grid=(n_slots // tm,),
            in_specs=[tile, per_e((D, 2 * F)), per_e((1, 2 * F)), per_e((F, D)), per_e((1, D))],
            out_specs=tile),
        out_shape=jax.ShapeDtypeStruct((n_slots, D), F32),
        compiler_params=_cparams("arbitrary"),
    )(blk_e, n_used, xb, w1, b1, w2, b2)


def _combine_kernel(dest_hbm, yb_hbm, h_ref, rw_ref, g_ref, b_ref, o_ref,
                    dest_s, ybuf, sem_idx, sem):
    i = pl.program_id(0)
    tm = h_ref.shape[0]
    idx_copy = pltpu.make_async_copy(dest_hbm.at[i], dest_s, sem_idx)
    idx_copy.start()
    idx_copy.wait()

    def body(t, carry):
        for kk in range(TOP_K):
            slot = dest_s[t * TOP_K + kk]
            pltpu.make_async_copy(yb_hbm.at[pl.ds(slot, 1)], ybuf.at[kk, pl.ds(t, 1)], sem).start()
        return carry

    lax.fori_loop(0, tm, body, 0)
    for kk in range(TOP_K):
        pltpu.make_async_copy(yb_hbm.at[pl.ds(0, tm)], ybuf.at[kk], sem).wait()
    rw = rw_ref[...]
    moe = rw[:, 0:1] * ybuf[0]
    for kk in range(1, TOP_K):
        moe = moe + rw[:, kk:kk + 1] * ybuf[kk]
    o_ref[...] = _layer_norm(DN_ALPHA * h_ref[...] + moe, g_ref[...], b_ref[...])


def _combine(dest_tiles, yb, h2, rw, g, b):
    T, D = h2.shape
    tm = TOKEN_TILE
    return pl.pallas_call(
        _combine_kernel,
        grid=(T // tm,),
        in_specs=[pl.BlockSpec(memory_space=pl.ANY), pl.BlockSpec(memory_space=pl.ANY),
                  pl.BlockSpec((tm, D), lambda i: (i, 0)),
                  pl.BlockSpec((tm, ROUTE_LANES), lambda i: (i, 0)),
                  _full((1, D)), _full((1, D))],
        out_specs=pl.BlockSpec((tm, D), lambda i: (i, 0)),
        out_shape=jax.ShapeDtypeStruct((T, D), F32),
        scratch_shapes=[pltpu.SMEM((tm * TOP_K,), jnp.int32),
                        pltpu.VMEM((TOP_K, tm, D), F32),
                        pltpu.SemaphoreType.DMA, pltpu.SemaphoreType.DMA],
        compiler_params=_cparams("arbitrary"),
    )(dest_tiles, yb, h2, rw, g, b)


def _s5_discretise(a_re, a_im, b_re, b_im, c_re, c_im, log_dt):
    G, P, C = S5_GROUPS, S5_STATE, S5_CH
    dt = jnp.exp(log_dt)[:, None]
    mag = jnp.exp(a_re * dt)
    abar_re = mag * jnp.cos(a_im * dt)
    abar_im = mag * jnp.sin(a_im * dt)
    den = a_re * a_re + a_im * a_im
    num_re = abar_re - 1.0
    coef_re = (num_re * a_re + abar_im * a_im) / den
    coef_im = (abar_im * a_re - num_re * a_im) / den
    bbar_re = coef_re[..., None] * b_re - coef_im[..., None] * b_im
    bbar_im = coef_re[..., None] * b_im + coef_im[..., None] * b_re
    eye = jnp.eye(G, dtype=F32)
    to_in = lambda m: jnp.einsum('gpc,gh->gchp', m, eye).reshape(G * C, G * P)
    to_out = lambda m: jnp.einsum('gcp,gh->gphc', m, eye).reshape(G * P, G * C)
    return (to_in(bbar_re).astype(BF16), to_in(bbar_im).astype(BF16),
            abar_re.reshape(1, G * P), abar_im.reshape(1, G * P),
            to_out(c_re).astype(BF16), to_out(c_im).astype(BF16))


def _moe_layer(h2, router_w, router_b, w1, b1, w2, b2, g, b):
    T, D = h2.shape
    E = N_EXPERTS
    tm = TOKEN_TILE
    route, rw, counts = _router(h2, router_w, router_b.reshape(1, E))
    counts = counts[0].astype(jnp.int32)
    padded = (counts + MOE_TILE - 1) // MOE_TILE * MOE_TILE
    pad_end = jnp.cumsum(padded)
    pad_start = pad_end - padded
    n_tiles = T * TOP_K // MOE_TILE + E
    n_slots = n_tiles * MOE_TILE
    tile_start = jnp.arange(n_tiles, dtype=jnp.int32) * MOE_TILE
    blk_e = jnp.minimum(jnp.sum(tile_start[:, None] >= pad_end[None, :], axis=1), E - 1).astype(jnp.int32)
    n_used = (pad_end[E - 1:] // MOE_TILE).astype(jnp.int32)
    dest = _dest_slots(route, pad_start.astype(jnp.int32))
    dest_tiles = dest[:, :TOP_K].reshape(T // tm, tm * TOP_K)
    xb = _dispatch(dest_tiles, h2, jnp.zeros((n_slots, D), F32))
    yb = _experts(blk_e, n_used, xb, _unzip_gate_up(w1), _unzip_bias(b1),
                  w2.astype(BF16), b2[:, None, :])
    return _combine(dest_tiles, yb, h2, rw, g.reshape(1, D), b.reshape(1, D))


def kernel(x, mem, ln_in_g, ln_in_b, w_in, conv_w, sg_norm_g, sg_norm_b, sg_w, sg_b, rw_mu, rw_w0, rw_w_up, rw_a0, rw_a_up, rw_g_up, rw_k_k, rw_k_a, rw_r_k, rw_ln_g, rw_ln_b, s5_a_re, s5_a_im, s5_b_re, s5_b_im, s5_c_re, s5_c_im, s5_d, s5_log_dt, s5_glu_w, s5_glu_b, br_proj, gate_b, w_out, ln1_g, ln1_b, xa_wq, xa_wk, xa_wv, xa_wo, ln2_g, ln2_b, router_w, router_b, ex_w1, ex_b1, ex_w2, ex_b2, ln3_g, ln3_b):
    B, S, D = x.shape
    M = mem.shape[1]
    T = B * S
    W = BRANCH_W
    row = lambda v: v.reshape(1, -1)
    pos = jnp.arange(SG_BLOCK)
    sg_mask = (pos[None, :] // CHUNK) <= (pos[:, None] // CHUNK)
    head_ones = jnp.kron(jnp.eye(RW_HEADS, dtype=F32), jnp.ones((RW_HEAD_DIM, RW_HEAD_DIM), F32))
    mem2 = mem.reshape(B * M, D)

    h = _input_ln(x.reshape(T, D), ln_in_g, ln_in_b)
    for l in range(DEPTH):
        w_l = w_in[l].astype(BF16)
        h3 = h.reshape(B, S, D)
        sg_wm = jnp.where(sg_mask[None], sg_w[l], 0.0).astype(BF16)
        sg_bias = jnp.repeat(sg_b[l].T, W // SG_GROUPS, axis=1)
        o_a, o_b = _mixers_ab(h3, w_l[:, :OFF_C], conv_w[l], row(sg_norm_g[l]), row(sg_norm_b[l]),
                              sg_wm, sg_bias)
        zero = jnp.zeros((RW_LORA, W), F32)
        lora_w = jnp.concatenate([jnp.concatenate([rw_w_up[l], zero], axis=1),
                                  jnp.concatenate([zero, rw_a_up[l]], axis=1)], axis=0)
        o_c = _mixer_rwkv(h3, w_l[:, OFF_C:OFF_D], row(rw_mu[l]), row(rw_w0[l]), row(rw_a0[l]),
                          lora_w, rw_g_up[l], row(rw_k_k[l]), row(rw_k_a[l]), row(rw_r_k[l]),
                          row(rw_ln_g[l]), row(rw_ln_b[l]), head_ones)
        s5p = _s5_discretise(s5_a_re[l], s5_a_im[l], s5_b_re[l], s5_b_im[l], s5_c_re[l],
                             s5_c_im[l], s5_log_dt[l])
        o_d = _mixer_s5(h3, w_l[:, OFF_D:OFF_G], *s5p, row(s5_d[l]),
                        s5_glu_w[l].astype(BF16), row(s5_glu_b[l]))
        flat = lambda o: o.reshape(T, W)
        h = _merge(h, flat(o_a), flat(o_b), flat(o_c), flat(o_d), w_l[:, OFF_G:], gate_b[l],
                   br_proj[l].astype(BF16), w_out[l].astype(BF16), row(ln1_g[l]), row(ln1_b[l]))
        kmem, vmem = _kv_proj(mem2, xa_wk[l].astype(BF16), xa_wv[l].astype(BF16))
        h = _cross_attention(h.reshape(B, S, D), kmem.reshape(B, M, D), vmem.reshape(B, M, D),
                             xa_wq[l].astype(BF16), xa_wo[l].astype(BF16),
                             row(ln2_g[l]), row(ln2_b[l])).reshape(T, D)
        h = _moe_layer(h, router_w[l], router_b[l], ex_w1[l], ex_b1[l], ex_w2[l], ex_b2[l],
                       ln3_g[l], ln3_b[l])
    return h.reshape(B, S, D)
```

```python
import functools
import math

import jax
import jax.numpy as jnp
from jax import lax
from jax.experimental import pallas as pl
from jax.experimental.pallas import tpu as pltpu

F32 = jnp.float32
BF16 = jnp.bfloat16
HIGHEST = lax.Precision.HIGHEST

D_MODEL = 1024
DEPTH = 2
CHUNK = 64
BRANCH_W = 256
N_BRANCH = 4
CONV_W = 3
SG_BLOCK = 128
SG_GROUPS = 4
RW_HEADS = 4
RW_HEAD_DIM = 64
RW_LORA = 64
RW_GATE_LORA = 128
RW_IN = 3 * BRANCH_W + 2 * RW_LORA + RW_GATE_LORA
RW_LN_EPS = 64e-5
S5_CH = 16
S5_GROUPS = BRANCH_W // S5_CH
S5_STATE = 64
S5_N = S5_GROUPS * S5_STATE
OFF_B = 3 * BRANCH_W
OFF_C = OFF_B + 2 * BRANCH_W
OFF_D = OFF_C + RW_IN
OFF_G = OFF_D + BRANCH_W
XA_HEADS = 4
XA_HEAD_DIM = D_MODEL // XA_HEADS
N_EXPERTS = 32
TOP_K = 4
D_FF = D_MODEL
SWIGLU_LIMIT = 7.0
SWIGLU_ALPHA = 1.702
LN_EPS = 1e-5
DN_ALPHA = (2 * DEPTH) ** 0.25

V7X_LANES = 128
V7X_SUBLANES = 8
V7X_VMEM_LIMIT_BYTES = 56 * 1024 * 1024
V7X_DMA_PRIORITIES = 2
DMA_PRIORITIES = V7X_DMA_PRIORITIES
TOKEN_TILE = 512
SEQ_TILE = 256
RW_CHUNK = 64
MOE_TILE = 512
ROUTE_LANES = 128


def _cparams(*sem):
    return pltpu.CompilerParams(dimension_semantics=sem,
                                vmem_limit_bytes=V7X_VMEM_LIMIT_BYTES)


def _full(shape):
    nd = len(shape)
    return pl.BlockSpec(shape, lambda *_: (0,) * nd)


def _layer_norm(x, g, b, eps=LN_EPS):
    mu = jnp.mean(x, axis=-1, keepdims=True)
    xc = x - mu
    var = jnp.mean(xc * xc, axis=-1, keepdims=True)
    return xc * lax.rsqrt(var + eps) * g + b


def _contract(a, b, dims, **kw):
    return lax.dot_general(a, b, (dims, ((), ())), preferred_element_type=F32, **kw)


def _dot(a, b, **kw):
    return _contract(a, b, ((1,), (0,)), **kw)


def _dot_nt(a, b, **kw):
    return _contract(a, b, ((1,), (1,)), **kw)


def _dot_tn(a, b, **kw):
    return _contract(a, b, ((0,), (0,)), **kw)


def _shift_rows(x, n, tail):
    rolled = pltpu.roll(x, n, 0)
    row = lax.broadcasted_iota(jnp.int32, x.shape, 0)
    out = rolled
    for i in range(n):
        src = tail[V7X_SUBLANES - n + i:V7X_SUBLANES - n + i + 1, :]
        out = jnp.where(row == i, src, out)
    return out


def _ln_kernel(x_ref, g_ref, b_ref, o_ref):
    o_ref[...] = _layer_norm(x_ref[...], g_ref[...], b_ref[...])


def _input_ln(x2, g, b):
    T, D = x2.shape
    tm = TOKEN_TILE * 2
    return pl.pallas_call(
        _ln_kernel,
        grid=(T // tm,),
        in_specs=[pl.BlockSpec((tm, D), lambda i: (i, 0)), _full((1, D)), _full((1, D))],
        out_specs=pl.BlockSpec((tm, D), lambda i: (i, 0)),
        out_shape=jax.ShapeDtypeStruct((T, D), F32),
        compiler_params=_cparams("parallel"),
    )(x2, g.reshape(1, D), b.reshape(1, D))


def _ab_kernel(h_ref, w_ref, cw_ref, ng_ref, nb_ref, sw_ref, sb_ref,
               oa_ref, ob_ref, tail_ref):
    @pl.when(pl.program_id(1) == 0)
    def _():
        tail_ref[...] = jnp.zeros_like(tail_ref)

    tm = h_ref.shape[0]
    W = BRANCH_W
    z = _dot(h_ref[...].astype(BF16), w_ref[...])
    ch = z[:, W:2 * W] * z[:, 2 * W:3 * W]
    tail = tail_ref[...]
    s1 = _shift_rows(ch, 1, tail)
    s2 = _shift_rows(ch, 2, tail)
    tail_ref[...] = ch[tm - V7X_SUBLANES:, :]
    cw = cw_ref[...]
    y = cw[2:3, :] * ch + cw[1:2, :] * s1 + cw[0:1, :] * s2
    oa_ref[...] = (z[:, 0:W] * y).astype(oa_ref.dtype)
    u = z[:, 3 * W:4 * W]
    v = _layer_norm(z[:, 4 * W:5 * W], ng_ref[...], nb_ref[...]).astype(BF16)
    gw = W // SG_GROUPS
    grp = lax.broadcasted_iota(jnp.int32, (SG_BLOCK, W), 1) // gw
    for blk in range(tm // SG_BLOCK):
        rows = slice(blk * SG_BLOCK, (blk + 1) * SG_BLOCK)
        vb = v[rows, :]
        sv = sb_ref[...]
        for g in range(SG_GROUPS):
            sv = sv + jnp.where(grp == g, _dot(sw_ref[g], vb), 0.0)
        ob_ref[rows, :] = (u[rows, :] * sv).astype(ob_ref.dtype)


def _mixers_ab(h, w_ab, conv_w, ng, nb, sg_wm, sg_bias):
    B, S, D = h.shape
    tm = SEQ_TILE
    W = BRANCH_W
    blk = lambda b, j: (b, j, 0)
    return pl.pallas_call(
        _ab_kernel,
        grid=(B, S // tm),
        in_specs=[pl.BlockSpec((None, tm, D), blk),
                  _full((D, 5 * W)), _full((CONV_W, W)), _full((1, W)), _full((1, W)),
                  _full((SG_GROUPS, SG_BLOCK, SG_BLOCK)), _full((SG_BLOCK, W))],
        out_specs=[pl.BlockSpec((None, tm, W), blk), pl.BlockSpec((None, tm, W), blk)],
        out_shape=[jax.ShapeDtypeStruct((B, S, W), BF16)] * 2,
        scratch_shapes=[pltpu.VMEM((V7X_SUBLANES, W), F32)],
        compiler_params=_cparams("parallel", "arbitrary"),
    )(h, w_ab, conv_w, ng, nb, sg_wm, sg_bias)


def _softplus(x):
    return jnp.maximum(x, 0.0) + jnp.log(1.0 + jnp.exp(-jnp.abs(x)))


def _rwkv_kernel(h_ref, w_ref, mu_ref, w0_ref, a0_ref, lora_ref, gup_ref, kk_ref, ka_ref,
                 rk_ref, lng_ref, lnb_ref, ones_ref, o_ref,
                 tail_ref, state_ref, v_s, y_s, gend_s, ar_s, bk_s, inv_s, rb_s, mk_s):
    @pl.when(pl.program_id(1) == 0)
    def _():
        tail_ref[...] = jnp.zeros_like(tail_ref)
        state_ref[...] = jnp.zeros_like(state_ref)

    tm = h_ref.shape[0]
    W = BRANCH_W
    N = RW_HEAD_DIM
    L = RW_CHUNK
    z = _dot(h_ref[...].astype(BF16), w_ref[...])
    zprev = _shift_rows(z, 1, tail_ref[...])
    tail_ref[...] = z[tm - V7X_SUBLANES:, :]
    z = z + (zprev - z) * mu_ref[...]
    r = z[:, 0:W]
    k = z[:, W:2 * W]
    v = z[:, 2 * W:3 * W]
    xwa = z[:, 3 * W:3 * W + 2 * RW_LORA]
    xg = z[:, 3 * W + 2 * RW_LORA:]
    lane = lax.broadcasted_iota(jnp.int32, xwa.shape, 1)
    lora_in = jnp.where(lane < RW_LORA, jnp.tanh(xwa), xwa)
    lora = _dot(lora_in, lora_ref[...], precision=HIGHEST)
    w = -_softplus(-(w0_ref[...] + lora[:, 0:W])) - 0.5
    log_decay = -jnp.exp(w)
    a = jax.nn.sigmoid(a0_ref[...] + lora[:, W:2 * W])
    g = _dot(jax.nn.sigmoid(xg), gup_ref[...], precision=HIGHEST)
    ones = ones_ref[...]
    kk = k * kk_ref[...]
    kk = kk / jnp.maximum(jnp.sqrt(_dot(kk * kk, ones, precision=HIGHEST)), 1e-12)
    k = k * (1.0 + (a - 1.0) * ka_ref[...])
    bonus = _dot(r * k * rk_ref[...], ones, precision=HIGHEST) * v

    ti = lax.broadcasted_iota(jnp.int32, (L, L), 0)
    tj = lax.broadcasted_iota(jnp.int32, (L, L), 1)
    tri = (ti >= tj).astype(F32)
    eye = (ti == tj).astype(F32)
    t2 = lax.broadcasted_iota(jnp.int32, (2 * L, L), 0)
    s2 = lax.broadcasted_iota(jnp.int32, (2 * L, L), 1)
    mask2 = ((t2 < L) & (t2 > s2)) | (t2 - L >= s2)
    n_chunks = tm // L

    alpha = -kk
    beta = kk * a
    v_s[...] = v.astype(BF16)
    for c in range(n_chunks):
        rows = slice(c * L, (c + 1) * L)
        ld = log_decay[rows, :]
        cum = _dot(tri, ld, precision=HIGHEST)
        gam = jnp.exp(cum)
        gam_inv = jnp.exp(-cum)
        gend_s[c * V7X_SUBLANES:(c + 1) * V7X_SUBLANES, :] = jnp.broadcast_to(
            gam[L - 1:L, :], (V7X_SUBLANES, W))
        ar_all = jnp.concatenate([alpha[rows, :] * jnp.exp(cum - ld), r[rows, :] * gam],
                                 axis=0).astype(BF16)
        bk_all = jnp.concatenate([beta[rows, :] * gam_inv, k[rows, :] * gam_inv],
                                 axis=0).astype(BF16)
        ar_s[c] = ar_all
        bk_s[c] = bk_all

    heads = [(c, hd) for c in range(n_chunks) for hd in range(RW_HEADS)]
    hcols = lambda hd: slice(hd * N, (hd + 1) * N)
    invs, ps = [], []
    for c, hd in heads:
        pair = _dot_nt(ar_s[c, :, hcols(hd)], bk_s[c, :, hcols(hd)])
        m_b = jnp.where(mask2, pair[:, :L], 0.0)
        mk_s[c * RW_HEADS + hd] = jnp.where(mask2, pair[:, L:], 0.0).astype(BF16)
        rb_s[c * RW_HEADS + hd] = m_b[L:].astype(BF16)
        invs.append(eye + m_b[:L])
        ps.append(m_b[:L].astype(BF16))
    ps = [_dot(pb, pb) for pb in ps]
    for _ in range(int(math.log2(L)) - 2):
        both = [_dot(jnp.concatenate([p.astype(BF16), inv.astype(BF16)], axis=0), p.astype(BF16))
                for p, inv in zip(ps, invs)]
        ps = [b2[:L] for b2 in both]
        invs = [inv + b2[L:] for inv, b2 in zip(invs, both)]
    for i, (inv, p) in enumerate(zip(invs, ps)):
        inv_s[i] = (inv + _dot(inv.astype(BF16), p.astype(BF16))).astype(BF16)

    def chunk(c, carry):
        rows = pl.ds(pl.multiple_of(c * L, L), L)
        ar_all = ar_s[c]
        bk_all = bk_s[c]
        v_all = v_s[rows, :]
        gend = gend_s[pl.ds(pl.multiple_of(c * V7X_SUBLANES, V7X_SUBLANES), 1), :]
        hs = range(RW_HEADS)
        sts = [state_ref[hd] for hd in hs]
        bases = [_dot_nt(ar_all[:, hcols(hd)], sts[hd].astype(BF16))
                 + _dot(mk_s[c * RW_HEADS + hd], v_all[:, hcols(hd)]) for hd in hs]
        us = [_dot(inv_s[c * RW_HEADS + hd], bases[hd][:L].astype(BF16)).astype(BF16) for hd in hs]
        for hd in hs:
            y_s[rows, hcols(hd)] = bases[hd][L:] + _dot(rb_s[c * RW_HEADS + hd], us[hd])
        for hd in hs:
            st = sts[hd] + _dot_tn(jnp.concatenate([us[hd], v_all[:, hcols(hd)]], axis=0),
                                   bk_all[:, hcols(hd)])
            state_ref[hd] = st * gend[:, hcols(hd)]
        return carry

    lax.fori_loop(0, n_chunks, chunk, 0)

    y = y_s[...]
    inv_n = 1.0 / N
    m = _dot(y, ones, precision=HIGHEST) * inv_n
    yc = y - m
    var = _dot(yc * yc, ones, precision=HIGHEST) * inv_n
    yn = yc * lax.rsqrt(var + RW_LN_EPS) * lng_ref[...] + lnb_ref[...]
    o_ref[...] = ((yn + bonus) * g).astype(o_ref.dtype)


def _mixer_rwkv(h, w_c, mu, w0, a0, lora_w, g_up, k_k, k_a, r_k, ln_g, ln_b, head_ones):
    B, S, D = h.shape
    tm = SEQ_TILE
    W = BRANCH_W
    L = RW_CHUNK
    nc = tm // L
    blk = lambda b, j: (b, j, 0)
    row = lambda n: _full((1, n))
    return pl.pallas_call(
        _rwkv_kernel,
        grid=(B, S // tm),
        in_specs=[pl.BlockSpec((None, tm, D), blk), _full((D, RW_IN)), row(RW_IN), row(W), row(W),
                  _full((2 * RW_LORA, 2 * W)), _full((RW_GATE_LORA, W)), row(W), row(W), row(W),
                  row(W), row(W), _full((W, W))],
        out_specs=pl.BlockSpec((None, tm, W), blk),
        out_shape=jax.ShapeDtypeStruct((B, S, W), BF16),
        scratch_shapes=[pltpu.VMEM((V7X_SUBLANES, RW_IN), F32),
                        pltpu.VMEM((RW_HEADS, RW_HEAD_DIM, RW_HEAD_DIM), F32),
                        pltpu.VMEM((tm, W), BF16),
                        pltpu.VMEM((tm, W), F32),
                        pltpu.VMEM((nc * V7X_SUBLANES, W), F32),
                        pltpu.VMEM((nc, 2 * L, W), BF16),
                        pltpu.VMEM((nc, 2 * L, W), BF16),
                        pltpu.VMEM((nc * RW_HEADS, L, L), BF16),
                        pltpu.VMEM((nc * RW_HEADS, L, L), BF16),
                        pltpu.VMEM((nc * RW_HEADS, 2 * L, L), BF16)],
        compiler_params=_cparams("parallel", "arbitrary"),
    )(h, w_c, mu, w0, a0, lora_w, g_up, k_k, k_a, r_k, ln_g, ln_b, head_ones)


def _s5_kernel(h_ref, w_ref, bre_ref, bim_ref, are_ref, aim_ref, cre_ref, cim_ref,
               d_ref, gw_ref, gb_ref, o_ref, xr_s, xi_s, st_s):
    @pl.when(pl.program_id(1) == 0)
    def _():
        st_s[...] = jnp.zeros_like(st_s)

    tm = h_ref.shape[0]
    u = _dot(h_ref[...].astype(BF16), w_ref[...])
    ub = u.astype(BF16)
    xr_s[...] = _dot(ub, bre_ref[...])
    xi_s[...] = _dot(ub, bim_ref[...])
    ar = are_ref[...]
    ai = aim_ref[...]

    def step(t, carry):
        xr, xi = carry
        row = pl.ds(t, 1)
        nr = ar * xr - ai * xi + xr_s[row, :]
        ni = ar * xi + ai * xr + xi_s[row, :]
        xr_s[row, :] = nr
        xi_s[row, :] = ni
        return nr, ni

    xr, xi = lax.fori_loop(0, tm, step, (st_s[0:1, :], st_s[1:2, :]), unroll=8)
    st_s[0:1, :] = xr
    st_s[1:2, :] = xi
    y = (_dot(xr_s[...].astype(BF16), cre_ref[...]) - _dot(xi_s[...].astype(BF16), cim_ref[...])
         + d_ref[...] * u)
    y = jax.nn.gelu(y)
    gate = jax.nn.sigmoid(_dot(y.astype(BF16), gw_ref[...]) + gb_ref[...])
    o_ref[...] = (y * gate).astype(o_ref.dtype)


def _mixer_s5(h, w_d, bre, bim, are, aim, cre, cim, d, glu_w, glu_b):
    B, S, D = h.shape
    tm = SEQ_TILE
    W = BRANCH_W
    blk = lambda b, j: (b, j, 0)
    return pl.pallas_call(
        _s5_kernel,
        grid=(B, S // tm),
        in_specs=[pl.BlockSpec((None, tm, D), blk), _full((D, W)),
                  _full((W, S5_N)), _full((W, S5_N)), _full((1, S5_N)), _full((1, S5_N)),
                  _full((S5_N, W)), _full((S5_N, W)), _full((1, W)), _full((W, W)), _full((1, W))],
        out_specs=pl.BlockSpec((None, tm, W), blk),
        out_shape=jax.ShapeDtypeStruct((B, S, W), BF16),
        scratch_shapes=[pltpu.VMEM((tm, S5_N), F32), pltpu.VMEM((tm, S5_N), F32),
                        pltpu.VMEM((V7X_SUBLANES, S5_N), F32)],
        compiler_params=_cparams("parallel", "arbitrary"),
    )(h, w_d, bre, bim, are, aim, cre, cim, d, glu_w, glu_b)


def _merge_kernel(h_ref, oa_ref, ob_ref, oc_ref, od_ref, wg_ref, gb_ref, br_ref, wo_ref,
                  g_ref, b_ref, o_ref):
    D = D_MODEL
    h = h_ref[...]
    hb = h.astype(BF16)
    merged = None
    for i, br_in in enumerate((oa_ref, ob_ref, oc_ref, od_ref)):
        gate = jax.nn.sigmoid(_dot(hb, wg_ref[:, i * D:(i + 1) * D]) + gb_ref[i:i + 1, :])
        term = gate * _dot(br_in[...], br_ref[i])
        merged = term if merged is None else merged + term
    y = _dot(merged.astype(BF16), wo_ref[...])
    o_ref[...] = _layer_norm(DN_ALPHA * h + y, g_ref[...], b_ref[...])


def _merge(h2, oa, ob, oc, od, w_g, gate_b, br_proj, w_out, g, b):
    T, D = h2.shape
    tm = TOKEN_TILE
    W = BRANCH_W
    tok = lambda n: pl.BlockSpec((tm, n), lambda i: (i, 0))
    return pl.pallas_call(
        _merge_kernel,
        grid=(T // tm,),
        in_specs=[tok(D), tok(W), tok(W), tok(W), tok(W),
                  _full((D, N_BRANCH * D)), _full((N_BRANCH, D)), _full((N_BRANCH, W, D)),
                  _full((D, D)), _full((1, D)), _full((1, D))],
        out_specs=tok(D),
        out_shape=jax.ShapeDtypeStruct((T, D), F32),
        compiler_params=_cparams("parallel"),
    )(h2, oa, ob, oc, od, w_g, gate_b, br_proj, w_out, g, b)


def _kv_kernel(m_ref, wk_ref, wv_ref, k_ref, v_ref):
    mb = m_ref[...].astype(BF16)
    k_ref[...] = _dot(mb, wk_ref[...]).astype(k_ref.dtype)
    v_ref[...] = _dot(mb, wv_ref[...]).astype(v_ref.dtype)


def _kv_proj(mem2, wk, wv):
    R, D = mem2.shape
    tm = TOKEN_TILE
    tok = pl.BlockSpec((tm, D), lambda i: (i, 0))
    return pl.pallas_call(
        _kv_kernel,
        grid=(R // tm,),
        in_specs=[tok, _full((D, D)), _full((D, D))],
        out_specs=[tok, tok],
        out_shape=[jax.ShapeDtypeStruct((R, D), BF16)] * 2,
        compiler_params=_cparams("parallel"),
    )(mem2, wk, wv)


def _attn_kernel(h_ref, wq_ref, k_ref, v_ref, wo_ref, g_ref, b_ref, o_ref):
    h = h_ref[...]
    q = _dot(h.astype(BF16), wq_ref[...]).astype(BF16)
    scale = XA_HEAD_DIM ** -0.5
    outs = []
    for hd in range(XA_HEADS):
        cols = slice(hd * XA_HEAD_DIM, (hd + 1) * XA_HEAD_DIM)
        s = _dot_nt(q[:, cols], k_ref[:, cols]) * scale
        s = s - jnp.max(s, axis=-1, keepdims=True)
        p = jnp.exp(s)
        p = p / jnp.sum(p, axis=-1, keepdims=True)
        outs.append(_dot(p.astype(BF16), v_ref[:, cols]))
    o = jnp.concatenate(outs, axis=-1).astype(BF16)
    y = _dot(o, wo_ref[...])
    o_ref[...] = _layer_norm(DN_ALPHA * h + y, g_ref[...], b_ref[...])


def _cross_attention(h, kmem, vmem, wq, wo, g, b):
    B, S, D = h.shape
    M = kmem.shape[1]
    tm = TOKEN_TILE
    blk = lambda bi, j: (bi, j, 0)
    mem = lambda bi, j: (bi, 0, 0)
    return pl.pallas_call(
        _attn_kernel,
        grid=(B, S // tm),
        in_specs=[pl.BlockSpec((None, tm, D), blk), _full((D, D)),
                  pl.BlockSpec((None, M, D), mem), pl.BlockSpec((None, M, D), mem),
                  _full((D, D)), _full((1, D)), _full((1, D))],
        out_specs=pl.BlockSpec((None, tm, D), blk),
        out_shape=jax.ShapeDtypeStruct((B, S, D), F32),
        compiler_params=_cparams("parallel", "parallel"),
    )(h, wq, kmem, vmem, wo, g, b)


def _router_kernel(h_ref, w_ref, b_ref, route_ref, rw_ref, cnt_ref, carry_ref):
    @pl.when(pl.program_id(0) == 0)
    def _():
        carry_ref[...] = jnp.zeros_like(carry_ref)

    tm = h_ref.shape[0]
    E = N_EXPERTS
    logits = _dot(h_ref[...], w_ref[...], precision=HIGHEST) + b_ref[...]
    lane = lax.broadcasted_iota(jnp.int32, (tm, E), 1)
    work = logits
    vals, hots, idxs = [], [], []
    for _ in range(TOP_K):
        m = jnp.max(work, axis=-1, keepdims=True)
        idx = jnp.min(jnp.where(work == m, lane, E), axis=-1, keepdims=True)
        hot = lane == idx
        work = jnp.where(hot, -jnp.inf, work)
        vals.append(m)
        hots.append(hot)
        idxs.append(idx)
    exps = [jnp.exp(v - vals[0]) for v in vals]
    denom = exps[0] + exps[1] + exps[2] + exps[3]
    multi = (hots[0] | hots[1] | hots[2] | hots[3]).astype(F32)
    ti = lax.broadcasted_iota(jnp.int32, (tm, tm), 0)
    tj = lax.broadcasted_iota(jnp.int32, (tm, tm), 1)
    before = (ti > tj).astype(BF16)
    prefix = _dot(before, multi.astype(BF16)) + carry_ref[0:1, :]
    out_lane = lax.broadcasted_iota(jnp.int32, (tm, ROUTE_LANES), 1)
    route = jnp.zeros((tm, ROUTE_LANES), jnp.int32)
    rw = jnp.zeros((tm, ROUTE_LANES), F32)
    for kk in range(TOP_K):
        rank = jnp.sum(jnp.where(hots[kk], prefix, 0.0), axis=-1, keepdims=True).astype(jnp.int32)
        route = jnp.where(out_lane == kk, idxs[kk], route)
        route = jnp.where(out_lane == TOP_K + kk, rank, route)
        rw = jnp.where(out_lane == kk, exps[kk] / denom, rw)
    route_ref[...] = route
    rw_ref[...] = rw
    carry_ref[0:1, :] = carry_ref[0:1, :] + jnp.sum(multi, axis=0, keepdims=True)
    cnt_ref[...] = carry_ref[...]


def _router(h2, router_w, router_b):
    T, D = h2.shape
    tm = TOKEN_TILE
    E = N_EXPERTS
    return pl.pallas_call(
        _router_kernel,
        grid=(T // tm,),
        in_specs=[pl.BlockSpec((tm, D), lambda i: (i, 0)), _full((D, E)), _full((1, E))],
        out_specs=[pl.BlockSpec((tm, ROUTE_LANES), lambda i: (i, 0)),
                   pl.BlockSpec((tm, ROUTE_LANES), lambda i: (i, 0)),
                   _full((V7X_SUBLANES, E))],
        out_shape=[jax.ShapeDtypeStruct((T, ROUTE_LANES), jnp.int32),
                   jax.ShapeDtypeStruct((T, ROUTE_LANES), F32),
                   jax.ShapeDtypeStruct((V7X_SUBLANES, E), F32)],
        scratch_shapes=[pltpu.VMEM((V7X_SUBLANES, E), F32)],
        compiler_params=_cparams("arbitrary"),
    )(h2, router_w, router_b)


def _dest_kernel(route_ref, start_ref, dest_ref):
    route = route_ref[...]
    tm = route.shape[0]
    lane = lax.broadcasted_iota(jnp.int32, (tm, N_EXPERTS), 1)
    out_lane = lax.broadcasted_iota(jnp.int32, (tm, ROUTE_LANES), 1)
    dest = jnp.zeros((tm, ROUTE_LANES), jnp.int32)
    start = start_ref[...]
    for kk in range(TOP_K):
        idx = route[:, kk:kk + 1]
        base = jnp.sum(jnp.where(lane == idx, start, 0), axis=-1, keepdims=True)
        dest = jnp.where(out_lane == kk, base + route[:, TOP_K + kk:TOP_K + kk + 1], dest)
    dest_ref[...] = dest


def _dest_slots(route, pad_start):
    T = route.shape[0]
    tm = TOKEN_TILE * 2
    return pl.pallas_call(
        _dest_kernel,
        grid=(T // tm,),
        in_specs=[pl.BlockSpec((tm, ROUTE_LANES), lambda i: (i, 0)), _full((1, N_EXPERTS))],
        out_specs=pl.BlockSpec((tm, ROUTE_LANES), lambda i: (i, 0)),
        out_shape=jax.ShapeDtypeStruct((T, ROUTE_LANES), jnp.int32),
        compiler_params=_cparams("parallel"),
    )(route, pad_start.reshape(1, N_EXPERTS))


def _row_copy_wait(src_rows, dst_hbm, sem):
    n = src_rows.shape[0]
    pltpu.make_async_copy(src_rows, dst_hbm.at[pl.ds(0, n)], sem).wait()


def _dispatch_kernel(dest_hbm, h_ref, xb_in, xb_hbm, dest_s, sem_idx, sem):
    del xb_in
    i = pl.program_id(0)
    tm = h_ref.shape[0]
    idx_copy = pltpu.make_async_copy(dest_hbm.at[i], dest_s, sem_idx)
    idx_copy.start()
    idx_copy.wait()

    def body(t, carry):
        for kk in range(TOP_K):
            slot = dest_s[t * TOP_K + kk]
            pltpu.make_async_copy(h_ref.at[pl.ds(t, 1)], xb_hbm.at[pl.ds(slot, 1)], sem).start(priority=kk % DMA_PRIORITIES)
        return carry

    lax.fori_loop(0, tm, body, 0)
    for _ in range(TOP_K):
        _row_copy_wait(h_ref, xb_hbm, sem)


def _dispatch(dest_tiles, h2, xb_zero):
    T, D = h2.shape
    tm = TOKEN_TILE
    return pl.pallas_call(
        _dispatch_kernel,
        grid=(T // tm,),
        in_specs=[pl.BlockSpec(memory_space=pl.ANY),
                  pl.BlockSpec((tm, D), lambda i: (i, 0)),
                  pl.BlockSpec(memory_space=pl.ANY)],
        out_specs=pl.BlockSpec(memory_space=pl.ANY),
        out_shape=jax.ShapeDtypeStruct(xb_zero.shape, xb_zero.dtype),
        scratch_shapes=[pltpu.SMEM((tm * TOP_K,), jnp.int32),
                        pltpu.SemaphoreType.DMA, pltpu.SemaphoreType.DMA],
        input_output_aliases={2: 0},
        compiler_params=_cparams("arbitrary"),
    )(dest_tiles, h2, xb_zero)


GU_BLOCK = 2 * V7X_LANES


def _unzip_kernel(w_ref, p_ref, o_ref):
    for c in range(w_ref.shape[1] // GU_BLOCK):
        cols = slice(c * GU_BLOCK, (c + 1) * GU_BLOCK)
        o_ref[:, cols] = _dot(w_ref[:, cols].astype(BF16), p_ref[...]).astype(o_ref.dtype)


def _unzip_gate_up(w1):
    E, D, F2 = w1.shape
    tr = TOKEN_TILE
    j = jnp.arange(GU_BLOCK)
    perm = (j[:, None] == jnp.where(j < V7X_LANES, 2 * j, 2 * (j - V7X_LANES) + 1)[None, :])
    blk = pl.BlockSpec((None, tr, F2), lambda e, r: (e, r, 0))
    return pl.pallas_call(
        _unzip_kernel,
        grid=(E, D // tr),
        in_specs=[blk, _full((GU_BLOCK, GU_BLOCK))],
        out_specs=blk,
        out_shape=jax.ShapeDtypeStruct((E, D, F2), BF16),
        compiler_params=_cparams("parallel", "parallel"),
    )(w1, perm.astype(BF16))


def _unzip_bias(b1):
    E, F2 = b1.shape
    return b1.reshape(E, F2 // GU_BLOCK, V7X_LANES, 2).transpose(0, 1, 3, 2).reshape(E, 1, F2)


def _expert_kernel(blk_e_ref, n_used_ref, x_ref, w1_ref, b1_ref, w2_ref, b2_ref, o_ref):
    del blk_e_ref
    i = pl.program_id(0)

    @pl.when(i < n_used_ref[0])
    def _():
        x = x_ref[...].astype(BF16)
        hid = _dot(x, w1_ref[...]) + b1_ref[...]
        acts = []
        for c in range(hid.shape[1] // GU_BLOCK):
            gate = jnp.minimum(hid[:, c * GU_BLOCK:c * GU_BLOCK + V7X_LANES], SWIGLU_LIMIT)
            up = jnp.clip(hid[:, c * GU_BLOCK + V7X_LANES:(c + 1) * GU_BLOCK],
                          -SWIGLU_LIMIT, SWIGLU_LIMIT)
            acts.append(((up + 1.0) * gate * jax.nn.sigmoid(SWIGLU_ALPHA * gate)).astype(BF16))
        act = jnp.concatenate(acts, axis=-1)
        o_ref[...] = _dot(act, w2_ref[...]) + b2_ref[...]

    @pl.when(i >= n_used_ref[0])
    def _():
        o_ref[...] = jnp.zeros_like(o_ref)


def _experts(blk_e, n_used, xb, w1, b1, w2, b2):
    n_slots, D = xb.shape
    tm = MOE_TILE
    F = D_FF
    tile = pl.BlockSpec((tm, D), lambda i, be, nu: (i, 0))
    per_e = lambda shape: pl.BlockSpec((None,) + shape, lambda i, be, nu: (be[i], 0, 0))
    return pl.pallas_call(
        _expert_kernel,
        grid_spec=pltpu.PrefetchScalarGridSpec(
            num_scalar_prefetch=2,
            grid=(n_slots // tm,),
            in_specs=[tile, per_e((D, 2 * F)), per_e((1, 2 * F)), per_e((F, D)), per_e((1, D))],
            out_specs=tile),
        out_shape=jax.ShapeDtypeStruct((n_slots, D), F32),
        compiler_params=_cparams("arbitrary"),
    )(blk_e, n_used, xb, w1, b1, w2, b2)


def _combine_kernel(dest_hbm, yb_hbm, h_ref, rw_ref, g_ref, b_ref, o_ref,
                    dest_s, ybuf, sem_idx, sem):
    i = pl.program_id(0)
    tm = h_ref.shape[0]
    idx_copy = pltpu.make_async_copy(dest_hbm.at[i], dest_s, sem_idx)
    idx_copy.start()
    idx_copy.wait()

    def body(t, carry):
        for kk in range(TOP_K):
            slot = dest_s[t * TOP_K + kk]
            pltpu.make_async_copy(yb_hbm.at[pl.ds(slot, 1)], ybuf.at[kk, pl.ds(t, 1)], sem).start(priority=kk % DMA_PRIORITIES)
        return carry

    lax.fori_loop(0, tm, body, 0)
    for kk in range(TOP_K):
        pltpu.make_async_copy(yb_hbm.at[pl.ds(0, tm)], ybuf.at[kk], sem).wait()
    rw = rw_ref[...]
    moe = rw[:, 0:1] * ybuf[0]
    for kk in range(1, TOP_K):
        moe = moe + rw[:, kk:kk + 1] * ybuf[kk]
    o_ref[...] = _layer_norm(DN_ALPHA * h_ref[...] + moe, g_ref[...], b_ref[...])


def _combine(dest_tiles, yb, h2, rw, g, b):
    T, D = h2.shape
    tm = TOKEN_TILE
    return pl.pallas_call(
        _combine_kernel,
        grid=(T // tm,),
        in_specs=[pl.BlockSpec(memory_space=pl.ANY), pl.BlockSpec(memory_space=pl.ANY),
                  pl.BlockSpec((tm, D), lambda i: (i, 0)),
                  pl.BlockSpec((tm, ROUTE_LANES), lambda i: (i, 0)),
                  _full((1, D)), _full((1, D))],
        out_specs=pl.BlockSpec((tm, D), lambda i: (i, 0)),
        out_shape=jax.ShapeDtypeStruct((T, D), F32),
        scratch_shapes=[pltpu.SMEM((tm * TOP_K,), jnp.int32),
                        pltpu.VMEM((TOP_K, tm, D), F32),
                        pltpu.SemaphoreType.DMA, pltpu.SemaphoreType.DMA],
        compiler_params=_cparams("arbitrary"),
    )(dest_tiles, yb, h2, rw, g, b)


def _s5_discretise(a_re, a_im, b_re, b_im, c_re, c_im, log_dt):
    G, P, C = S5_GROUPS, S5_STATE, S5_CH
    dt = jnp.exp(log_dt)[:, None]
    mag = jnp.exp(a_re * dt)
    abar_re = mag * jnp.cos(a_im * dt)
    abar_im = mag * jnp.sin(a_im * dt)
    den = a_re * a_re + a_im * a_im
    num_re = abar_re - 1.0
    coef_re = (num_re * a_re + abar_im * a_im) / den
    coef_im = (abar_im * a_re - num_re * a_im) / den
    bbar_re = coef_re[..., None] * b_re - coef_im[..., None] * b_im
    bbar_im = coef_re[..., None] * b_im + coef_im[..., None] * b_re
    eye = jnp.eye(G, dtype=F32)
    to_in = lambda m: jnp.einsum('gpc,gh->gchp', m, eye).reshape(G * C, G * P)
    to_out = lambda m: jnp.einsum('gcp,gh->gphc', m, eye).reshape(G * P, G * C)
    return (to_in(bbar_re).astype(BF16), to_in(bbar_im).astype(BF16),
            abar_re.reshape(1, G * P), abar_im.reshape(1, G * P),
            to_out(c_re).astype(BF16), to_out(c_im).astype(BF16))


def _moe_layer(h2, router_w, router_b, w1, b1, w2, b2, g, b):
    T, D = h2.shape
    E = N_EXPERTS
    tm = TOKEN_TILE
    route, rw, counts = _router(h2, router_w, router_b.reshape(1, E))
    counts = counts[0].astype(jnp.int32)
    padded = (counts + MOE_TILE - 1) // MOE_TILE * MOE_TILE
    pad_end = jnp.cumsum(padded)
    pad_start = pad_end - padded
    n_tiles = T * TOP_K // MOE_TILE + E
    n_slots = n_tiles * MOE_TILE
    tile_start = jnp.arange(n_tiles, dtype=jnp.int32) * MOE_TILE
    blk_e = jnp.minimum(jnp.sum(tile_start[:, None] >= pad_end[None, :], axis=1), E - 1).astype(jnp.int32)
    n_used = (pad_end[E - 1:] // MOE_TILE).astype(jnp.int32)
    dest = _dest_slots(route, pad_start.astype(jnp.int32))
    dest_tiles = dest[:, :TOP_K].reshape(T // tm, tm * TOP_K)
    xb = _dispatch(dest_tiles, h2, jnp.zeros((n_slots, D), F32))
    yb = _experts(blk_e, n_used, xb, _unzip_gate_up(w1), _unzip_bias(b1),
                  w2.astype(BF16), b2[:, None, :])
    return _combine(dest_tiles, yb, h2, rw, g.reshape(1, D), b.reshape(1, D))


def kernel(x, mem, ln_in_g, ln_in_b, w_in, conv_w, sg_norm_g, sg_norm_b, sg_w, sg_b, rw_mu, rw_w0, rw_w_up, rw_a0, rw_a_up, rw_g_up, rw_k_k, rw_k_a, rw_r_k, rw_ln_g, rw_ln_b, s5_a_re, s5_a_im, s5_b_re, s5_b_im, s5_c_re, s5_c_im, s5_d, s5_log_dt, s5_glu_w, s5_glu_b, br_proj, gate_b, w_out, ln1_g, ln1_b, xa_wq, xa_wk, xa_wv, xa_wo, ln2_g, ln2_b, router_w, router_b, ex_w1, ex_b1, ex_w2, ex_b2, ln3_g, ln3_b):
    B, S, D = x.shape
    M = mem.shape[1]
    T = B * S
    W = BRANCH_W
    row = lambda v: v.reshape(1, -1)
    pos = jnp.arange(SG_BLOCK)
    sg_mask = (pos[None, :] // CHUNK) <= (pos[:, None] // CHUNK)
    head_ones = jnp.kron(jnp.eye(RW_HEADS, dtype=F32), jnp.ones((RW_HEAD_DIM, RW_HEAD_DIM), F32))
    mem2 = mem.reshape(B * M, D)

    h = _input_ln(x.reshape(T, D), ln_in_g, ln_in_b)
    for l in range(DEPTH):
        w_l = w_in[l].astype(BF16)
        h3 = h.reshape(B, S, D)
        sg_wm = jnp.where(sg_mask[None], sg_w[l], 0.0).astype(BF16)
        sg_bias = jnp.repeat(sg_b[l].T, W // SG_GROUPS, axis=1)
        o_a, o_b = _mixers_ab(h3, w_l[:, :OFF_C], conv_w[l], row(sg_norm_g[l]), row(sg_norm_b[l]),
                              sg_wm, sg_bias)
        zero = jnp.zeros((RW_LORA, W), F32)
        lora_w = jnp.concatenate([jnp.concatenate([rw_w_up[l], zero], axis=1),
                                  jnp.concatenate([zero, rw_a_up[l]], axis=1)], axis=0)
        o_c = _mixer_rwkv(h3, w_l[:, OFF_C:OFF_D], row(rw_mu[l]), row(rw_w0[l]), row(rw_a0[l]),
                          lora_w, rw_g_up[l], row(rw_k_k[l]), row(rw_k_a[l]), row(rw_r_k[l]),
                          row(rw_ln_g[l]), row(rw_ln_b[l]), head_ones)
        s5p = _s5_discretise(s5_a_re[l], s5_a_im[l], s5_b_re[l], s5_b_im[l], s5_c_re[l],
                             s5_c_im[l], s5_log_dt[l])
        o_d = _mixer_s5(h3, w_l[:, OFF_D:OFF_G], *s5p, row(s5_d[l]),
                        s5_glu_w[l].astype(BF16), row(s5_glu_b[l]))
        flat = lambda o: o.reshape(T, W)
        h = _merge(h, flat(o_a), flat(o_b), flat(o_c), flat(o_d), w_l[:, OFF_G:], gate_b[l],
                   br_proj[l].astype(BF16), w_out[l].astype(BF16), row(ln1_g[l]), row(ln1_b[l]))
        kmem, vmem = _kv_proj(mem2, xa_wk[l].astype(BF16), xa_wv[l].astype(BF16))
        h = _cross_attention(h.reshape(B, S, D), kmem.reshape(B, M, D), vmem.reshape(B, M, D),
                             xa_wq[l].astype(BF16), xa_wo[l].astype(BF16),
                             row(ln2_g[l]), row(ln2_b[l])).reshape(T, D)
        h = _moe_layer(h, router_w[l], router_b[l], ex_w1[l], ex_b1[l], ex_w2[l], ex_b2[l],
                       ln3_g[l], ln3_b[l])
    return h.reshape(B, S, D)
```

```python
import functools
import math

import jax
import jax.numpy as jnp
from jax import lax
from jax.experimental import pallas as pl
from jax.experimental.pallas import tpu as pltpu

F32 = jnp.float32
BF16 = jnp.bfloat16
HIGHEST = lax.Precision.HIGHEST

D_MODEL = 1024
DEPTH = 2
CHUNK = 64
BRANCH_W = 256
N_BRANCH = 4
CONV_W = 3
SG_BLOCK = 128
SG_GROUPS = 4
RW_HEADS = 4
RW_HEAD_DIM = 64
RW_LORA = 64
RW_GATE_LORA = 128
RW_IN = 3 * BRANCH_W + 2 * RW_LORA + RW_GATE_LORA
RW_LN_EPS = 64e-5
S5_CH = 16
S5_GROUPS = BRANCH_W // S5_CH
S5_STATE = 64
S5_N = S5_GROUPS * S5_STATE
OFF_B = 3 * BRANCH_W
OFF_C = OFF_B + 2 * BRANCH_W
OFF_D = OFF_C + RW_IN
OFF_G = OFF_D + BRANCH_W
XA_HEADS = 4
XA_HEAD_DIM = D_MODEL // XA_HEADS
N_EXPERTS = 32
TOP_K = 4
TOP_K_BITS = 2
D_FF = D_MODEL
SWIGLU_LIMIT = 7.0
SWIGLU_ALPHA = 1.702
LN_EPS = 1e-5
DN_ALPHA = (2 * DEPTH) ** 0.25

V7X_LANES = 128
V7X_SUBLANES = 8
V7X_VMEM_LIMIT_BYTES = 56 * 1024 * 1024
V7X_DMA_PRIORITIES = 2
DMA_PRIORITIES = V7X_DMA_PRIORITIES
TOKEN_TILE = 512
SEQ_TILE = 256
RW_CHUNK = 64
MOE_TILE = 512
ROUTE_LANES = 128


def _cparams(*sem):
    return pltpu.CompilerParams(dimension_semantics=sem,
                                vmem_limit_bytes=V7X_VMEM_LIMIT_BYTES)


def _full(shape):
    nd = len(shape)
    return pl.BlockSpec(shape, lambda *_: (0,) * nd)


def _layer_norm(x, g, b, eps=LN_EPS):
    mu = jnp.mean(x, axis=-1, keepdims=True)
    xc = x - mu
    var = jnp.mean(xc * xc, axis=-1, keepdims=True)
    return xc * lax.rsqrt(var + eps) * g + b


def _contract(a, b, dims, **kw):
    return lax.dot_general(a, b, (dims, ((), ())), preferred_element_type=F32, **kw)


def _dot(a, b, **kw):
    return _contract(a, b, ((1,), (0,)), **kw)


def _dot_nt(a, b, **kw):
    return _contract(a, b, ((1,), (1,)), **kw)


def _dot_tn(a, b, **kw):
    return _contract(a, b, ((0,), (0,)), **kw)


def _shift_rows(x, n, tail):
    rolled = pltpu.roll(x, n, 0)
    row = lax.broadcasted_iota(jnp.int32, x.shape, 0)
    out = rolled
    for i in range(n):
        src = tail[V7X_SUBLANES - n + i:V7X_SUBLANES - n + i + 1, :]
        out = jnp.where(row == i, src, out)
    return out


def _ln_kernel(x_ref, g_ref, b_ref, o_ref):
    o_ref[...] = _layer_norm(x_ref[...], g_ref[...], b_ref[...])


def _input_ln(x2, g, b):
    T, D = x2.shape
    tm = TOKEN_TILE * 2
    return pl.pallas_call(
        _ln_kernel,
        grid=(T // tm,),
        in_specs=[pl.BlockSpec((tm, D), lambda i: (i, 0)), _full((1, D)), _full((1, D))],
        out_specs=pl.BlockSpec((tm, D), lambda i: (i, 0)),
        out_shape=jax.ShapeDtypeStruct((T, D), F32),
        compiler_params=_cparams("parallel"),
    )(x2, g.reshape(1, D), b.reshape(1, D))


def _ab_kernel(h_ref, w_ref, cw_ref, ng_ref, nb_ref, sw_ref, sb_ref,
               oa_ref, ob_ref, tail_ref):
    @pl.when(pl.program_id(1) == 0)
    def _():
        tail_ref[...] = jnp.zeros_like(tail_ref)

    tm = h_ref.shape[0]
    W = BRANCH_W
    z = _dot(h_ref[...].astype(BF16), w_ref[...])
    ch = z[:, W:2 * W] * z[:, 2 * W:3 * W]
    tail = tail_ref[...]
    s1 = _shift_rows(ch, 1, tail)
    s2 = _shift_rows(ch, 2, tail)
    tail_ref[...] = ch[tm - V7X_SUBLANES:, :]
    cw = cw_ref[...]
    y = cw[2:3, :] * ch + cw[1:2, :] * s1 + cw[0:1, :] * s2
    oa_ref[...] = (z[:, 0:W] * y).astype(oa_ref.dtype)
    u = z[:, 3 * W:4 * W]
    v = _layer_norm(z[:, 4 * W:5 * W], ng_ref[...], nb_ref[...]).astype(BF16)
    gw = W // SG_GROUPS
    grp = lax.broadcasted_iota(jnp.int32, (SG_BLOCK, W), 1) // gw
    for blk in range(tm // SG_BLOCK):
        rows = slice(blk * SG_BLOCK, (blk + 1) * SG_BLOCK)
        vb = v[rows, :]
        sv = sb_ref[...]
        for g in range(SG_GROUPS):
            sv = sv + jnp.where(grp == g, _dot(sw_ref[g], vb), 0.0)
        ob_ref[rows, :] = (u[rows, :] * sv).astype(ob_ref.dtype)


def _mixers_ab(h, w_ab, conv_w, ng, nb, sg_wm, sg_bias):
    B, S, D = h.shape
    tm = SEQ_TILE
    W = BRANCH_W
    blk = lambda b, j: (b, j, 0)
    return pl.pallas_call(
        _ab_kernel,
        grid=(B, S // tm),
        in_specs=[pl.BlockSpec((None, tm, D), blk),
                  _full((D, 5 * W)), _full((CONV_W, W)), _full((1, W)), _full((1, W)),
                  _full((SG_GROUPS, SG_BLOCK, SG_BLOCK)), _full((SG_BLOCK, W))],
        out_specs=[pl.BlockSpec((None, tm, W), blk), pl.BlockSpec((None, tm, W), blk)],
        out_shape=[jax.ShapeDtypeStruct((B, S, W), BF16)] * 2,
        scratch_shapes=[pltpu.VMEM((V7X_SUBLANES, W), F32)],
        compiler_params=_cparams("parallel", "arbitrary"),
    )(h, w_ab, conv_w, ng, nb, sg_wm, sg_bias)


def _softplus(x):
    return jnp.maximum(x, 0.0) + jnp.log(1.0 + jnp.exp(-jnp.abs(x)))


def _rwkv_kernel(h_ref, w_ref, mu_ref, w0_ref, a0_ref, lora_ref, gup_ref, kk_ref, ka_ref,
                 rk_ref, lng_ref, lnb_ref, ones_ref, o_ref,
                 tail_ref, state_ref, v_s, y_s, gend_s, ar_s, bk_s, inv_s, rb_s, mk_s):
    @pl.when(pl.program_id(1) == 0)
    def _():
        tail_ref[...] = jnp.zeros_like(tail_ref)
        state_ref[...] = jnp.zeros_like(state_ref)

    tm = h_ref.shape[0]
    W = BRANCH_W
    N = RW_HEAD_DIM
    L = RW_CHUNK
    z = _dot(h_ref[...].astype(BF16), w_ref[...])
    zprev = _shift_rows(z, 1, tail_ref[...])
    tail_ref[...] = z[tm - V7X_SUBLANES:, :]
    z = z + (zprev - z) * mu_ref[...]
    r = z[:, 0:W]
    k = z[:, W:2 * W]
    v = z[:, 2 * W:3 * W]
    xwa = z[:, 3 * W:3 * W + 2 * RW_LORA]
    xg = z[:, 3 * W + 2 * RW_LORA:]
    lane = lax.broadcasted_iota(jnp.int32, xwa.shape, 1)
    lora_in = jnp.where(lane < RW_LORA, jnp.tanh(xwa), xwa)
    lora = _dot(lora_in, lora_ref[...], precision=HIGHEST)
    w = -_softplus(-(w0_ref[...] + lora[:, 0:W])) - 0.5
    log_decay = -jnp.exp(w)
    a = jax.nn.sigmoid(a0_ref[...] + lora[:, W:2 * W])
    g = _dot(jax.nn.sigmoid(xg), gup_ref[...], precision=HIGHEST)
    ones = ones_ref[...]
    kk = k * kk_ref[...]
    kk = kk / jnp.maximum(jnp.sqrt(_dot(kk * kk, ones, precision=HIGHEST)), 1e-12)
    k = k * (1.0 + (a - 1.0) * ka_ref[...])
    bonus = _dot(r * k * rk_ref[...], ones, precision=HIGHEST) * v

    ti = lax.broadcasted_iota(jnp.int32, (L, L), 0)
    tj = lax.broadcasted_iota(jnp.int32, (L, L), 1)
    tri = (ti >= tj).astype(F32)
    eye = (ti == tj).astype(F32)
    t2 = lax.broadcasted_iota(jnp.int32, (2 * L, L), 0)
    s2 = lax.broadcasted_iota(jnp.int32, (2 * L, L), 1)
    mask2 = ((t2 < L) & (t2 > s2)) | (t2 - L >= s2)
    n_chunks = tm // L

    alpha = -kk
    beta = kk * a
    v_s[...] = v.astype(BF16)
    for c in range(n_chunks):
        rows = slice(c * L, (c + 1) * L)
        ld = log_decay[rows, :]
        cum = _dot(tri, ld, precision=HIGHEST)
        gam = jnp.exp(cum)
        gam_inv = jnp.exp(-cum)
        gend_s[c * V7X_SUBLANES:(c + 1) * V7X_SUBLANES, :] = jnp.broadcast_to(
            gam[L - 1:L, :], (V7X_SUBLANES, W))
        ar_all = jnp.concatenate([alpha[rows, :] * jnp.exp(cum - ld), r[rows, :] * gam],
                                 axis=0).astype(BF16)
        bk_all = jnp.concatenate([beta[rows, :] * gam_inv, k[rows, :] * gam_inv],
                                 axis=0).astype(BF16)
        ar_s[c] = ar_all
        bk_s[c] = bk_all

    heads = [(c, hd) for c in range(n_chunks) for hd in range(RW_HEADS)]
    hcols = lambda hd: slice(hd * N, (hd + 1) * N)
    invs, ps = [], []
    for c, hd in heads:
        pair = _dot_nt(ar_s[c, :, hcols(hd)], bk_s[c, :, hcols(hd)])
        m_b = jnp.where(mask2, pair[:, :L], 0.0)
        mk_s[c * RW_HEADS + hd] = jnp.where(mask2, pair[:, L:], 0.0).astype(BF16)
        rb_s[c * RW_HEADS + hd] = m_b[L:].astype(BF16)
        invs.append(eye + m_b[:L])
        ps.append(m_b[:L].astype(BF16))
    ps = [_dot(pb, pb) for pb in ps]
    for _ in range(int(math.log2(L)) - 2):
        both = [_dot(jnp.concatenate([p.astype(BF16), inv.astype(BF16)], axis=0), p.astype(BF16))
                for p, inv in zip(ps, invs)]
        ps = [b2[:L] for b2 in both]
        invs = [inv + b2[L:] for inv, b2 in zip(invs, both)]
    for i, (inv, p) in enumerate(zip(invs, ps)):
        inv_s[i] = (inv + _dot(inv.astype(BF16), p.astype(BF16))).astype(BF16)

    def chunk(c, carry):
        rows = pl.ds(pl.multiple_of(c * L, L), L)
        ar_all = ar_s[c]
        bk_all = bk_s[c]
        v_all = v_s[rows, :]
        gend = gend_s[pl.ds(pl.multiple_of(c * V7X_SUBLANES, V7X_SUBLANES), 1), :]
        hs = range(RW_HEADS)
        sts = [state_ref[hd] for hd in hs]
        bases = [_dot_nt(ar_all[:, hcols(hd)], sts[hd].astype(BF16))
                 + _dot(mk_s[c * RW_HEADS + hd], v_all[:, hcols(hd)]) for hd in hs]
        us = [_dot(inv_s[c * RW_HEADS + hd], bases[hd][:L].astype(BF16)).astype(BF16) for hd in hs]
        for hd in hs:
            y_s[rows, hcols(hd)] = bases[hd][L:] + _dot(rb_s[c * RW_HEADS + hd], us[hd])
        for hd in hs:
            st = sts[hd] + _dot_tn(jnp.concatenate([us[hd], v_all[:, hcols(hd)]], axis=0),
                                   bk_all[:, hcols(hd)])
            state_ref[hd] = st * gend[:, hcols(hd)]
        return carry

    lax.fori_loop(0, n_chunks, chunk, 0)

    y = y_s[...]
    inv_n = 1.0 / N
    m = _dot(y, ones, precision=HIGHEST) * inv_n
    yc = y - m
    var = _dot(yc * yc, ones, precision=HIGHEST) * inv_n
    yn = yc * lax.rsqrt(var + RW_LN_EPS) * lng_ref[...] + lnb_ref[...]
    o_ref[...] = ((yn + bonus) * g).astype(o_ref.dtype)


def _mixer_rwkv(h, w_c, mu, w0, a0, lora_w, g_up, k_k, k_a, r_k, ln_g, ln_b, head_ones):
    B, S, D = h.shape
    tm = SEQ_TILE
    W = BRANCH_W
    L = RW_CHUNK
    nc = tm // L
    blk = lambda b, j: (b, j, 0)
    row = lambda n: _full((1, n))
    return pl.pallas_call(
        _rwkv_kernel,
        grid=(B, S // tm),
        in_specs=[pl.BlockSpec((None, tm, D), blk), _full((D, RW_IN)), row(RW_IN), row(W), row(W),
                  _full((2 * RW_LORA, 2 * W)), _full((RW_GATE_LORA, W)), row(W), row(W), row(W),
                  row(W), row(W), _full((W, W))],
        out_specs=pl.BlockSpec((None, tm, W), blk),
        out_shape=jax.ShapeDtypeStruct((B, S, W), BF16),
        scratch_shapes=[pltpu.VMEM((V7X_SUBLANES, RW_IN), F32),
                        pltpu.VMEM((RW_HEADS, RW_HEAD_DIM, RW_HEAD_DIM), F32),
                        pltpu.VMEM((tm, W), BF16),
                        pltpu.VMEM((tm, W), F32),
                        pltpu.VMEM((nc * V7X_SUBLANES, W), F32),
                        pltpu.VMEM((nc, 2 * L, W), BF16),
                        pltpu.VMEM((nc, 2 * L, W), BF16),
                        pltpu.VMEM((nc * RW_HEADS, L, L), BF16),
                        pltpu.VMEM((nc * RW_HEADS, L, L), BF16),
                        pltpu.VMEM((nc * RW_HEADS, 2 * L, L), BF16)],
        compiler_params=_cparams("parallel", "arbitrary"),
    )(h, w_c, mu, w0, a0, lora_w, g_up, k_k, k_a, r_k, ln_g, ln_b, head_ones)


def _s5_kernel(h_ref, w_ref, bre_ref, bim_ref, are_ref, aim_ref, cre_ref, cim_ref,
               d_ref, gw_ref, gb_ref, o_ref, xr_s, xi_s, st_s):
    @pl.when(pl.program_id(1) == 0)
    def _():
        st_s[...] = jnp.zeros_like(st_s)

    tm = h_ref.shape[0]
    u = _dot(h_ref[...].astype(BF16), w_ref[...])
    ub = u.astype(BF16)
    xr_s[...] = _dot(ub, bre_ref[...])
    xi_s[...] = _dot(ub, bim_ref[...])
    ar = are_ref[...]
    ai = aim_ref[...]

    def step(t, carry):
        xr, xi = carry
        row = pl.ds(t, 1)
        nr = ar * xr - ai * xi + xr_s[row, :]
        ni = ar * xi + ai * xr + xi_s[row, :]
        xr_s[row, :] = nr
        xi_s[row, :] = ni
        return nr, ni

    xr, xi = lax.fori_loop(0, tm, step, (st_s[0:1, :], st_s[1:2, :]), unroll=8)
    st_s[0:1, :] = xr
    st_s[1:2, :] = xi
    y = (_dot(xr_s[...].astype(BF16), cre_ref[...]) - _dot(xi_s[...].astype(BF16), cim_ref[...])
         + d_ref[...] * u)
    y = jax.nn.gelu(y)
    gate = jax.nn.sigmoid(_dot(y.astype(BF16), gw_ref[...]) + gb_ref[...])
    o_ref[...] = (y * gate).astype(o_ref.dtype)


def _mixer_s5(h, w_d, bre, bim, are, aim, cre, cim, d, glu_w, glu_b):
    B, S, D = h.shape
    tm = SEQ_TILE
    W = BRANCH_W
    blk = lambda b, j: (b, j, 0)
    return pl.pallas_call(
        _s5_kernel,
        grid=(B, S // tm),
        in_specs=[pl.BlockSpec((None, tm, D), blk), _full((D, W)),
                  _full((W, S5_N)), _full((W, S5_N)), _full((1, S5_N)), _full((1, S5_N)),
                  _full((S5_N, W)), _full((S5_N, W)), _full((1, W)), _full((W, W)), _full((1, W))],
        out_specs=pl.BlockSpec((None, tm, W), blk),
        out_shape=jax.ShapeDtypeStruct((B, S, W), BF16),
        scratch_shapes=[pltpu.VMEM((tm, S5_N), F32), pltpu.VMEM((tm, S5_N), F32),
                        pltpu.VMEM((V7X_SUBLANES, S5_N), F32)],
        compiler_params=_cparams("parallel", "arbitrary"),
    )(h, w_d, bre, bim, are, aim, cre, cim, d, glu_w, glu_b)


def _merge_kernel(h_ref, oa_ref, ob_ref, oc_ref, od_ref, wg_ref, gb_ref, br_ref, wo_ref,
                  g_ref, b_ref, o_ref):
    D = D_MODEL
    h = h_ref[...]
    hb = h.astype(BF16)
    merged = None
    for i, br_in in enumerate((oa_ref, ob_ref, oc_ref, od_ref)):
        gate = jax.nn.sigmoid(_dot(hb, wg_ref[:, i * D:(i + 1) * D]) + gb_ref[i:i + 1, :])
        term = gate * _dot(br_in[...], br_ref[i])
        merged = term if merged is None else merged + term
    y = _dot(merged.astype(BF16), wo_ref[...])
    o_ref[...] = _layer_norm(DN_ALPHA * h + y, g_ref[...], b_ref[...])


def _merge(h2, oa, ob, oc, od, w_g, gate_b, br_proj, w_out, g, b):
    T, D = h2.shape
    tm = TOKEN_TILE
    W = BRANCH_W
    tok = lambda n: pl.BlockSpec((tm, n), lambda i: (i, 0))
    return pl.pallas_call(
        _merge_kernel,
        grid=(T // tm,),
        in_specs=[tok(D), tok(W), tok(W), tok(W), tok(W),
                  _full((D, N_BRANCH * D)), _full((N_BRANCH, D)), _full((N_BRANCH, W, D)),
                  _full((D, D)), _full((1, D)), _full((1, D))],
        out_specs=tok(D),
        out_shape=jax.ShapeDtypeStruct((T, D), F32),
        compiler_params=_cparams("parallel"),
    )(h2, oa, ob, oc, od, w_g, gate_b, br_proj, w_out, g, b)


def _kv_kernel(m_ref, wk_ref, wv_ref, k_ref, v_ref):
    mb = m_ref[...].astype(BF16)
    k_ref[...] = _dot(mb, wk_ref[...]).astype(k_ref.dtype)
    v_ref[...] = _dot(mb, wv_ref[...]).astype(v_ref.dtype)


def _kv_proj(mem2, wk, wv):
    R, D = mem2.shape
    tm = TOKEN_TILE
    tok = pl.BlockSpec((tm, D), lambda i: (i, 0))
    return pl.pallas_call(
        _kv_kernel,
        grid=(R // tm,),
        in_specs=[tok, _full((D, D)), _full((D, D))],
        out_specs=[tok, tok],
        out_shape=[jax.ShapeDtypeStruct((R, D), BF16)] * 2,
        compiler_params=_cparams("parallel"),
    )(mem2, wk, wv)


def _attn_kernel(h_ref, wq_ref, k_ref, v_ref, wo_ref, g_ref, b_ref, o_ref):
    h = h_ref[...]
    q = _dot(h.astype(BF16), wq_ref[...]).astype(BF16)
    scale = XA_HEAD_DIM ** -0.5
    outs = []
    for hd in range(XA_HEADS):
        cols = slice(hd * XA_HEAD_DIM, (hd + 1) * XA_HEAD_DIM)
        s = _dot_nt(q[:, cols], k_ref[:, cols]) * scale
        s = s - jnp.max(s, axis=-1, keepdims=True)
        p = jnp.exp(s)
        p = p / jnp.sum(p, axis=-1, keepdims=True)
        outs.append(_dot(p.astype(BF16), v_ref[:, cols]))
    o = jnp.concatenate(outs, axis=-1).astype(BF16)
    y = _dot(o, wo_ref[...])
    o_ref[...] = _layer_norm(DN_ALPHA * h + y, g_ref[...], b_ref[...])


def _cross_attention(h, kmem, vmem, wq, wo, g, b):
    B, S, D = h.shape
    M = kmem.shape[1]
    tm = TOKEN_TILE
    blk = lambda bi, j: (bi, j, 0)
    mem = lambda bi, j: (bi, 0, 0)
    return pl.pallas_call(
        _attn_kernel,
        grid=(B, S // tm),
        in_specs=[pl.BlockSpec((None, tm, D), blk), _full((D, D)),
                  pl.BlockSpec((None, M, D), mem), pl.BlockSpec((None, M, D), mem),
                  _full((D, D)), _full((1, D)), _full((1, D))],
        out_specs=pl.BlockSpec((None, tm, D), blk),
        out_shape=jax.ShapeDtypeStruct((B, S, D), F32),
        compiler_params=_cparams("parallel", "parallel"),
    )(h, wq, kmem, vmem, wo, g, b)


def _router_kernel(h_ref, w_ref, b_ref, route_ref, rw_ref, cnt_ref, carry_ref):
    @pl.when(pl.program_id(0) == 0)
    def _():
        carry_ref[...] = jnp.zeros_like(carry_ref)

    tm = h_ref.shape[0]
    E = N_EXPERTS
    logits = _dot(h_ref[...], w_ref[...], precision=HIGHEST) + b_ref[...]
    lane = lax.broadcasted_iota(jnp.int32, (tm, E), 1)
    work = logits
    vals, hots, idxs = [], [], []
    for _ in range(TOP_K):
        m = jnp.max(work, axis=-1, keepdims=True)
        idx = jnp.min(jnp.where(work == m, lane, E), axis=-1, keepdims=True)
        hot = lane == idx
        work = jnp.where(hot, -jnp.inf, work)
        vals.append(m)
        hots.append(hot)
        idxs.append(idx)
    exps = [jnp.exp(v - vals[0]) for v in vals]
    denom = exps[0] + exps[1] + exps[2] + exps[3]
    multi = (hots[0] | hots[1] | hots[2] | hots[3]).astype(F32)
    ti = lax.broadcasted_iota(jnp.int32, (tm, tm), 0)
    tj = lax.broadcasted_iota(jnp.int32, (tm, tm), 1)
    before = (ti > tj).astype(BF16)
    prefix = _dot(before, multi.astype(BF16)) + carry_ref[0:1, :]
    out_lane = lax.broadcasted_iota(jnp.int32, (tm, ROUTE_LANES), 1)
    route = jnp.zeros((tm, ROUTE_LANES), jnp.int32)
    rw = jnp.zeros((tm, ROUTE_LANES), F32)
    for kk in range(TOP_K):
        rank = jnp.sum(jnp.where(hots[kk], prefix, 0.0), axis=-1, keepdims=True).astype(jnp.int32)
        route = jnp.where(out_lane == kk, idxs[kk], route)
        route = jnp.where(out_lane == TOP_K + kk, rank, route)
        rw = jnp.where(out_lane == kk, exps[kk] / denom, rw)
    route_ref[...] = route
    rw_ref[...] = rw
    carry_ref[0:1, :] = carry_ref[0:1, :] + jnp.sum(multi, axis=0, keepdims=True)
    cnt_ref[...] = carry_ref[...]


def _router(h2, router_w, router_b):
    T, D = h2.shape
    tm = TOKEN_TILE
    E = N_EXPERTS
    return pl.pallas_call(
        _router_kernel,
        grid=(T // tm,),
        in_specs=[pl.BlockSpec((tm, D), lambda i: (i, 0)), _full((D, E)), _full((1, E))],
        out_specs=[pl.BlockSpec((tm, ROUTE_LANES), lambda i: (i, 0)),
                   pl.BlockSpec((tm, ROUTE_LANES), lambda i: (i, 0)),
                   _full((V7X_SUBLANES, E))],
        out_shape=[jax.ShapeDtypeStruct((T, ROUTE_LANES), jnp.int32),
                   jax.ShapeDtypeStruct((T, ROUTE_LANES), F32),
                   jax.ShapeDtypeStruct((V7X_SUBLANES, E), F32)],
        scratch_shapes=[pltpu.VMEM((V7X_SUBLANES, E), F32)],
        compiler_params=_cparams("arbitrary"),
    )(h2, router_w, router_b)


def _dest_kernel(route_ref, start_ref, dest_ref):
    route = route_ref[...]
    tm = route.shape[0]
    lane = lax.broadcasted_iota(jnp.int32, (tm, N_EXPERTS), 1)
    out_lane = lax.broadcasted_iota(jnp.int32, (tm, ROUTE_LANES), 1)
    dest = jnp.zeros((tm, ROUTE_LANES), jnp.int32)
    start = start_ref[...]
    for kk in range(TOP_K):
        idx = route[:, kk:kk + 1]
        base = jnp.sum(jnp.where(lane == idx, start, 0), axis=-1, keepdims=True)
        dest = jnp.where(out_lane == kk, base + route[:, TOP_K + kk:TOP_K + kk + 1], dest)
    dest_ref[...] = dest


def _dest_slots(route, pad_start):
    T = route.shape[0]
    tm = TOKEN_TILE * 2
    return pl.pallas_call(
        _dest_kernel,
        grid=(T // tm,),
        in_specs=[pl.BlockSpec((tm, ROUTE_LANES), lambda i: (i, 0)), _full((1, N_EXPERTS))],
        out_specs=pl.BlockSpec((tm, ROUTE_LANES), lambda i: (i, 0)),
        out_shape=jax.ShapeDtypeStruct((T, ROUTE_LANES), jnp.int32),
        compiler_params=_cparams("parallel"),
    )(route, pad_start.reshape(1, N_EXPERTS))


def _slot_source_kernel(fill_lo_ref, fill_hi_ref, dest_hbm, row_ref, dest_s, sem, *, n_tok):
    i = pl.program_id(0)
    n = dest_s.shape[0]

    @pl.when(i == 0)
    def _():
        def fill(s, carry):
            parity = (s // MOE_TILE + 1) % 2
            row_ref[s] = n_tok * TOP_K + parity * MOE_TILE + s % MOE_TILE
            return carry

        lax.fori_loop(0, MOE_TILE, fill, 0)
        for e in range(fill_lo_ref.shape[0]):
            lax.fori_loop(MOE_TILE + fill_lo_ref[e], MOE_TILE + fill_hi_ref[e], fill, 0)

    idx_copy = pltpu.make_async_copy(dest_hbm.at[i], dest_s, sem)
    idx_copy.start()
    idx_copy.wait()

    def body(j, carry):
        a = i * n + j
        row_ref[MOE_TILE + dest_s[j]] = (a & (TOP_K - 1)) * n_tok + (a >> TOP_K_BITS)
        return carry

    lax.fori_loop(0, n, body, 0, unroll=8)


def _slot_sources(fill_lo, fill_hi, dest_tiles, n_slots):
    n_tok_tiles, n = dest_tiles.shape
    return pl.pallas_call(
        functools.partial(_slot_source_kernel, n_tok=n_tok_tiles * n // TOP_K),
        grid_spec=pltpu.PrefetchScalarGridSpec(
            num_scalar_prefetch=2,
            grid=(n_tok_tiles,),
            in_specs=[pl.BlockSpec(memory_space=pl.ANY)],
            out_specs=pl.BlockSpec(memory_space=pltpu.SMEM),
            scratch_shapes=[pltpu.SMEM((n,), jnp.int32), pltpu.SemaphoreType.DMA]),
        out_shape=jax.ShapeDtypeStruct((MOE_TILE + n_slots,), jnp.int32),
        compiler_params=_cparams("arbitrary"),
    )(fill_lo, fill_hi, dest_tiles)


GU_BLOCK = 2 * V7X_LANES


def _unzip_kernel(w_ref, p_ref, o_ref):
    for c in range(w_ref.shape[1] // GU_BLOCK):
        cols = slice(c * GU_BLOCK, (c + 1) * GU_BLOCK)
        o_ref[:, cols] = _dot(w_ref[:, cols].astype(BF16), p_ref[...]).astype(o_ref.dtype)


def _unzip_gate_up(w1):
    E, D, F2 = w1.shape
    tr = TOKEN_TILE
    j = jnp.arange(GU_BLOCK)
    perm = (j[:, None] == jnp.where(j < V7X_LANES, 2 * j, 2 * (j - V7X_LANES) + 1)[None, :])
    blk = pl.BlockSpec((None, tr, F2), lambda e, r: (e, r, 0))
    return pl.pallas_call(
        _unzip_kernel,
        grid=(E, D // tr),
        in_specs=[blk, _full((GU_BLOCK, GU_BLOCK))],
        out_specs=blk,
        out_shape=jax.ShapeDtypeStruct((E, D, F2), BF16),
        compiler_params=_cparams("parallel", "parallel"),
    )(w1, perm.astype(BF16))


def _unzip_bias(b1):
    E, F2 = b1.shape
    return b1.reshape(E, F2 // GU_BLOCK, V7X_LANES, 2).transpose(0, 1, 3, 2).reshape(E, 1, F2)


def _expert_ffn(x, w1_ref, b1_ref, w2_ref, b2_ref):
    hid = _dot(x, w1_ref[...]) + b1_ref[...]
    acts = []
    for c in range(hid.shape[1] // GU_BLOCK):
        gate = jnp.minimum(hid[:, c * GU_BLOCK:c * GU_BLOCK + V7X_LANES], SWIGLU_LIMIT)
        up = jnp.clip(hid[:, c * GU_BLOCK + V7X_LANES:(c + 1) * GU_BLOCK],
                      -SWIGLU_LIMIT, SWIGLU_LIMIT)
        acts.append(((up + 1.0) * gate * jax.nn.sigmoid(SWIGLU_ALPHA * gate)).astype(BF16))
    return _dot(jnp.concatenate(acts, axis=-1), w2_ref[...]) + b2_ref[...]


def _expert_kernel(blk_e_ref, n_used_ref, src_ref, h_hbm, w1_ref, b1_ref, w2_ref, b2_ref, yk_hbm,
                   xbuf0, xbuf1, ybuf0, ybuf1, gsem, ssem):
    del blk_e_ref
    i = pl.program_id(0)
    n_used = n_used_ref[0]
    xbufs = (xbuf0, xbuf1)
    ybufs = (ybuf0, ybuf1)
    tm = xbuf0.shape[0]
    T = h_hbm.shape[0]
    n_assign = T * TOP_K

    def gather_row(tile, b, j, priority):
        row = src_ref[(tile + 1) * tm + j]
        tok = row & (T - 1) if T & (T - 1) == 0 else lax.rem(row, T)
        pltpu.make_async_copy(h_hbm.at[pl.ds(tok, 1)], xbufs[b].at[pl.ds(j, 1)],
                              gsem.at[b]).start(priority=priority)

    def scatter_row(tile, b, j, priority):
        row = src_ref[(tile + 1) * tm + j]
        pltpu.make_async_copy(ybufs[b].at[pl.ds(j, 1)], yk_hbm.at[pl.ds(row, 1)],
                              ssem.at[b]).start(priority=priority)

    def wait_gather(b):
        pltpu.make_async_copy(h_hbm.at[pl.ds(0, tm)], xbufs[b], gsem.at[b]).wait()

    def wait_scatter(b):
        pltpu.make_async_copy(ybufs[b], yk_hbm.at[pl.ds(0, tm)], ssem.at[b]).wait()

    def rows_loop(fn):
        def body(j, carry):
            fn(j)
            return carry
        lax.fori_loop(0, tm, body, 0)

    @pl.when(i == 0)
    def _():
        rows_loop(lambda j: gather_row(0, 0, j, 0))
        ybuf0[...] = jnp.zeros_like(ybuf0)
        ybuf1[...] = jnp.zeros_like(ybuf1)
        pltpu.make_async_copy(ybuf0, yk_hbm.at[pl.ds(n_assign, tm)], ssem.at[0]).start()

    for b in range(2):
        @pl.when((i < n_used) & (i % 2 == b))
        def _(b=b):
            wait_gather(b)
            wait_scatter(b)
            for j in range(tm):
                gather_row(i + 1, 1 - b, j, 0)
                scatter_row(i - 1, 1 - b, j, 1)
            ybufs[b][...] = _expert_ffn(xbufs[b][...].astype(BF16), w1_ref, b1_ref, w2_ref, b2_ref)

        @pl.when((i == n_used) & (i % 2 == b))
        def _(b=b):
            wait_gather(b)
            wait_scatter(b)
            rows_loop(lambda j: scatter_row(i - 1, 1 - b, j, 0))
            wait_scatter(1 - b)


def _experts(blk_e, n_used, src, h2, w1, b1, w2, b2):
    T, D = h2.shape
    tm = MOE_TILE
    F = D_FF
    n_tiles = src.shape[0] // tm - 1
    per_e = lambda shape: pl.BlockSpec((None,) + shape, lambda i, be, nu, sr: (be[i], 0, 0))
    return pl.pallas_call(
        _expert_kernel,
        grid_spec=pltpu.PrefetchScalarGridSpec(
            num_scalar_prefetch=3,
            grid=(n_tiles,),
            in_specs=[pl.BlockSpec(memory_space=pl.ANY),
                      per_e((D, 2 * F)), per_e((1, 2 * F)), per_e((F, D)), per_e((1, D))],
            out_specs=pl.BlockSpec(memory_space=pl.ANY),
            scratch_shapes=[pltpu.VMEM((tm, D), F32)] * 4
                           + [pltpu.SemaphoreType.DMA((2,)), pltpu.SemaphoreType.DMA((2,))]),
        out_shape=jax.ShapeDtypeStruct((T * TOP_K + 2 * tm, D), F32),
        compiler_params=_cparams("arbitrary"),
    )(blk_e, n_used, src, h2, w1, b1, w2, b2)


def _combine_kernel(y0_ref, y1_ref, y2_ref, y3_ref, h_ref, rw_ref, g_ref, b_ref, o_ref):
    rw = rw_ref[...]
    moe = rw[:, 0:1] * y0_ref[...]
    for kk, y_ref in enumerate((y1_ref, y2_ref, y3_ref), start=1):
        moe = moe + rw[:, kk:kk + 1] * y_ref[...]
    o_ref[...] = _layer_norm(DN_ALPHA * h_ref[...] + moe, g_ref[...], b_ref[...])


def _combine(yk, h2, rw, g, b):
    T, D = h2.shape
    tm = TOKEN_TILE
    n = T // tm
    choice = lambda kk: pl.BlockSpec((tm, D), lambda i: (kk * n + i, 0))
    return pl.pallas_call(
        _combine_kernel,
        grid=(n,),
        in_specs=[choice(kk) for kk in range(TOP_K)]
                 + [pl.BlockSpec((tm, D), lambda i: (i, 0)),
                    pl.BlockSpec((tm, ROUTE_LANES), lambda i: (i, 0)),
                    _full((1, D)), _full((1, D))],
        out_specs=pl.BlockSpec((tm, D), lambda i: (i, 0)),
        out_shape=jax.ShapeDtypeStruct((T, D), F32),
        compiler_params=_cparams("parallel"),
    )(yk, yk, yk, yk, h2, rw, g, b)


def _s5_discretise(a_re, a_im, b_re, b_im, c_re, c_im, log_dt):
    G, P, C = S5_GROUPS, S5_STATE, S5_CH
    dt = jnp.exp(log_dt)[:, None]
    mag = jnp.exp(a_re * dt)
    abar_re = mag * jnp.cos(a_im * dt)
    abar_im = mag * jnp.sin(a_im * dt)
    den = a_re * a_re + a_im * a_im
    num_re = abar_re - 1.0
    coef_re = (num_re * a_re + abar_im * a_im) / den
    coef_im = (abar_im * a_re - num_re * a_im) / den
    bbar_re = coef_re[..., None] * b_re - coef_im[..., None] * b_im
    bbar_im = coef_re[..., None] * b_im + coef_im[..., None] * b_re
    eye = jnp.eye(G, dtype=F32)
    to_in = lambda m: jnp.einsum('gpc,gh->gchp', m, eye).reshape(G * C, G * P)
    to_out = lambda m: jnp.einsum('gcp,gh->gphc', m, eye).reshape(G * P, G * C)
    return (to_in(bbar_re).astype(BF16), to_in(bbar_im).astype(BF16),
            abar_re.reshape(1, G * P), abar_im.reshape(1, G * P),
            to_out(c_re).astype(BF16), to_out(c_im).astype(BF16))


def _moe_layer(h2, router_w, router_b, w1, b1, w2, b2, g, b):
    T, D = h2.shape
    E = N_EXPERTS
    tm = TOKEN_TILE
    route, rw, counts = _router(h2, router_w, router_b.reshape(1, E))
    counts = counts[0].astype(jnp.int32)
    padded = (counts + MOE_TILE - 1) // MOE_TILE * MOE_TILE
    pad_end = jnp.cumsum(padded)
    pad_start = pad_end - padded
    n_tiles = T * TOP_K // MOE_TILE + E
    n_slots = n_tiles * MOE_TILE
    tile_start = jnp.arange(n_tiles, dtype=jnp.int32) * MOE_TILE
    blk_e = jnp.minimum(jnp.sum(tile_start[:, None] >= pad_end[None, :], axis=1), E - 1).astype(jnp.int32)
    n_used = (pad_end[E - 1:] // MOE_TILE).astype(jnp.int32)
    dest = _dest_slots(route, pad_start.astype(jnp.int32))
    dest_tiles = dest[:, :TOP_K].reshape(T // tm, tm * TOP_K)
    fill_lo = jnp.concatenate([pad_start + counts, pad_end[E - 1:]]).astype(jnp.int32)
    fill_hi = jnp.concatenate([pad_end, jnp.full((1,), n_slots)]).astype(jnp.int32)
    src = _slot_sources(fill_lo, fill_hi, dest_tiles, n_slots)
    yk = _experts(blk_e, n_used, src, h2, _unzip_gate_up(w1), _unzip_bias(b1),
                  w2.astype(BF16), b2[:, None, :])
    return _combine(yk, h2, rw, g.reshape(1, D), b.reshape(1, D))


def kernel(x, mem, ln_in_g, ln_in_b, w_in, conv_w, sg_norm_g, sg_norm_b, sg_w, sg_b, rw_mu, rw_w0, rw_w_up, rw_a0, rw_a_up, rw_g_up, rw_k_k, rw_k_a, rw_r_k, rw_ln_g, rw_ln_b, s5_a_re, s5_a_im, s5_b_re, s5_b_im, s5_c_re, s5_c_im, s5_d, s5_log_dt, s5_glu_w, s5_glu_b, br_proj, gate_b, w_out, ln1_g, ln1_b, xa_wq, xa_wk, xa_wv, xa_wo, ln2_g, ln2_b, router_w, router_b, ex_w1, ex_b1, ex_w2, ex_b2, ln3_g, ln3_b):
    B, S, D = x.shape
    M = mem.shape[1]
    T = B * S
    W = BRANCH_W
    row = lambda v: v.reshape(1, -1)
    pos = jnp.arange(SG_BLOCK)
    sg_mask = (pos[None, :] // CHUNK) <= (pos[:, None] // CHUNK)
    head_ones = jnp.kron(jnp.eye(RW_HEADS, dtype=F32), jnp.ones((RW_HEAD_DIM, RW_HEAD_DIM), F32))
    mem2 = mem.reshape(B * M, D)

    h = _input_ln(x.reshape(T, D), ln_in_g, ln_in_b)
    for l in range(DEPTH):
        w_l = w_in[l].astype(BF16)
        h3 = h.reshape(B, S, D)
        sg_wm = jnp.where(sg_mask[None], sg_w[l], 0.0).astype(BF16)
        sg_bias = jnp.repeat(sg_b[l].T, W // SG_GROUPS, axis=1)
        o_a, o_b = _mixers_ab(h3, w_l[:, :OFF_C], conv_w[l], row(sg_norm_g[l]), row(sg_norm_b[l]),
                              sg_wm, sg_bias)
        zero = jnp.zeros((RW_LORA, W), F32)
        lora_w = jnp.concatenate([jnp.concatenate([rw_w_up[l], zero], axis=1),
                                  jnp.concatenate([zero, rw_a_up[l]], axis=1)], axis=0)
        o_c = _mixer_rwkv(h3, w_l[:, OFF_C:OFF_D], row(rw_mu[l]), row(rw_w0[l]), row(rw_a0[l]),
                          lora_w, rw_g_up[l], row(rw_k_k[l]), row(rw_k_a[l]), row(rw_r_k[l]),
                          row(rw_ln_g[l]), row(rw_ln_b[l]), head_ones)
        s5p = _s5_discretise(s5_a_re[l], s5_a_im[l], s5_b_re[l], s5_b_im[l], s5_c_re[l],
                             s5_c_im[l], s5_log_dt[l])
        o_d = _mixer_s5(h3, w_l[:, OFF_D:OFF_G], *s5p, row(s5_d[l]),
                        s5_glu_w[l].astype(BF16), row(s5_glu_b[l]))
        flat = lambda o: o.reshape(T, W)
        h = _merge(h, flat(o_a), flat(o_b), flat(o_c), flat(o_d), w_l[:, OFF_G:], gate_b[l],
                   br_proj[l].astype(BF16), w_out[l].astype(BF16), row(ln1_g[l]), row(ln1_b[l]))
        kmem, vmem = _kv_proj(mem2, xa_wk[l].astype(BF16), xa_wv[l].astype(BF16))
        h = _cross_attention(h.reshape(B, S, D), kmem.reshape(B, M, D), vmem.reshape(B, M, D),
                             xa_wq[l].astype(BF16), xa_wo[l].astype(BF16),
                             row(ln2_g[l]), row(ln2_b[l])).reshape(T, D)
        h = _moe_layer(h, router_w[l], router_b[l], ex_w1[l], ex_b1[l], ex_w2[l], ex_b2[l],
                       ln3_g[l], ln3_b[l])
    return h.reshape(B, S, D)
```

```python
import functools
import math

import jax
import jax.numpy as jnp
from jax import lax
from jax.experimental import pallas as pl
from jax.experimental.pallas import tpu as pltpu

F32 = jnp.float32
BF16 = jnp.bfloat16
HIGHEST = lax.Precision.HIGHEST

D_MODEL = 1024
DEPTH = 2
CHUNK = 64
BRANCH_W = 256
N_BRANCH = 4
CONV_W = 3
SG_BLOCK = 128
SG_GROUPS = 4
RW_HEADS = 4
RW_HEAD_DIM = 64
RW_LORA = 64
RW_GATE_LORA = 128
RW_IN = 3 * BRANCH_W + 2 * RW_LORA + RW_GATE_LORA
RW_LN_EPS = 64e-5
S5_CH = 16
S5_GROUPS = BRANCH_W // S5_CH
S5_STATE = 64
S5_N = S5_GROUPS * S5_STATE
OFF_B = 3 * BRANCH_W
OFF_C = OFF_B + 2 * BRANCH_W
OFF_D = OFF_C + RW_IN
OFF_G = OFF_D + BRANCH_W
XA_HEADS = 4
XA_HEAD_DIM = D_MODEL // XA_HEADS
N_EXPERTS = 32
TOP_K = 4
TOP_K_BITS = 2
D_FF = D_MODEL
SWIGLU_LIMIT = 7.0
SWIGLU_ALPHA = 1.702
LN_EPS = 1e-5
DN_ALPHA = (2 * DEPTH) ** 0.25

V7X_LANES = 128
V7X_SUBLANES = 8
V7X_VMEM_LIMIT_BYTES = 56 * 1024 * 1024
V7X_DMA_PRIORITIES = 2
DMA_PRIORITIES = V7X_DMA_PRIORITIES
TOKEN_TILE = 512
SEQ_TILE = 256
RW_CHUNK = 64
MOE_TILE = 512
ROUTE_LANES = 128


def _cparams(*sem):
    return pltpu.CompilerParams(dimension_semantics=sem,
                                vmem_limit_bytes=V7X_VMEM_LIMIT_BYTES)


def _full(shape):
    nd = len(shape)
    return pl.BlockSpec(shape, lambda *_: (0,) * nd)


def _layer_norm(x, g, b, eps=LN_EPS):
    mu = jnp.mean(x, axis=-1, keepdims=True)
    xc = x - mu
    var = jnp.mean(xc * xc, axis=-1, keepdims=True)
    return xc * lax.rsqrt(var + eps) * g + b


def _contract(a, b, dims, **kw):
    return lax.dot_general(a, b, (dims, ((), ())), preferred_element_type=F32, **kw)


def _dot(a, b, **kw):
    return _contract(a, b, ((1,), (0,)), **kw)


def _dot_nt(a, b, **kw):
    return _contract(a, b, ((1,), (1,)), **kw)


def _dot_tn(a, b, **kw):
    return _contract(a, b, ((0,), (0,)), **kw)


ROW_TILE = D_MODEL // V7X_LANES


def _store_row_tiles(ref, x):
    n = x.shape[0]
    for c in range(ROW_TILE):
        ref[pl.ds(c, n, stride=ROW_TILE), :] = x[:, c * V7X_LANES:(c + 1) * V7X_LANES]


def _load_row_tiles(ref):
    n = ref.shape[0] // ROW_TILE
    return jnp.concatenate([ref[pl.ds(c, n, stride=ROW_TILE), :] for c in range(ROW_TILE)], axis=-1)


def _shift_rows(x, n, tail):
    rolled = pltpu.roll(x, n, 0)
    row = lax.broadcasted_iota(jnp.int32, x.shape, 0)
    out = rolled
    for i in range(n):
        src = tail[V7X_SUBLANES - n + i:V7X_SUBLANES - n + i + 1, :]
        out = jnp.where(row == i, src, out)
    return out


def _ln_kernel(x_ref, g_ref, b_ref, o_ref):
    o_ref[...] = _layer_norm(x_ref[...], g_ref[...], b_ref[...])


def _input_ln(x2, g, b):
    T, D = x2.shape
    tm = TOKEN_TILE * 2
    return pl.pallas_call(
        _ln_kernel,
        grid=(T // tm,),
        in_specs=[pl.BlockSpec((tm, D), lambda i: (i, 0)), _full((1, D)), _full((1, D))],
        out_specs=pl.BlockSpec((tm, D), lambda i: (i, 0)),
        out_shape=jax.ShapeDtypeStruct((T, D), F32),
        compiler_params=_cparams("parallel"),
    )(x2, g.reshape(1, D), b.reshape(1, D))


def _ab_kernel(h_ref, w_ref, cw_ref, ng_ref, nb_ref, sw_ref, sb_ref,
               oa_ref, ob_ref, tail_ref):
    @pl.when(pl.program_id(1) == 0)
    def _():
        tail_ref[...] = jnp.zeros_like(tail_ref)

    tm = h_ref.shape[0]
    W = BRANCH_W
    z = _dot(h_ref[...].astype(BF16), w_ref[...])
    ch = z[:, W:2 * W] * z[:, 2 * W:3 * W]
    tail = tail_ref[...]
    s1 = _shift_rows(ch, 1, tail)
    s2 = _shift_rows(ch, 2, tail)
    tail_ref[...] = ch[tm - V7X_SUBLANES:, :]
    cw = cw_ref[...]
    y = cw[2:3, :] * ch + cw[1:2, :] * s1 + cw[0:1, :] * s2
    oa_ref[...] = (z[:, 0:W] * y).astype(oa_ref.dtype)
    u = z[:, 3 * W:4 * W]
    v = _layer_norm(z[:, 4 * W:5 * W], ng_ref[...], nb_ref[...]).astype(BF16)
    gw = W // SG_GROUPS
    grp = lax.broadcasted_iota(jnp.int32, (SG_BLOCK, W), 1) // gw
    for blk in range(tm // SG_BLOCK):
        rows = slice(blk * SG_BLOCK, (blk + 1) * SG_BLOCK)
        vb = v[rows, :]
        sv = sb_ref[...]
        for g in range(SG_GROUPS):
            sv = sv + jnp.where(grp == g, _dot(sw_ref[g], vb), 0.0)
        ob_ref[rows, :] = (u[rows, :] * sv).astype(ob_ref.dtype)


def _mixers_ab(h, w_ab, conv_w, ng, nb, sg_wm, sg_bias):
    B, S, D = h.shape
    tm = SEQ_TILE
    W = BRANCH_W
    blk = lambda b, j: (b, j, 0)
    return pl.pallas_call(
        _ab_kernel,
        grid=(B, S // tm),
        in_specs=[pl.BlockSpec((None, tm, D), blk),
                  _full((D, 5 * W)), _full((CONV_W, W)), _full((1, W)), _full((1, W)),
                  _full((SG_GROUPS, SG_BLOCK, SG_BLOCK)), _full((SG_BLOCK, W))],
        out_specs=[pl.BlockSpec((None, tm, W), blk), pl.BlockSpec((None, tm, W), blk)],
        out_shape=[jax.ShapeDtypeStruct((B, S, W), BF16)] * 2,
        scratch_shapes=[pltpu.VMEM((V7X_SUBLANES, W), F32)],
        compiler_params=_cparams("parallel", "arbitrary"),
    )(h, w_ab, conv_w, ng, nb, sg_wm, sg_bias)


def _softplus(x):
    return jnp.maximum(x, 0.0) + jnp.log(1.0 + jnp.exp(-jnp.abs(x)))


def _rwkv_kernel(h_ref, w_ref, mu_ref, w0_ref, a0_ref, lora_ref, gup_ref, kk_ref, ka_ref,
                 rk_ref, lng_ref, lnb_ref, ones_ref, o_ref,
                 tail_ref, state_ref, v_s, y_s, gend_s, ar_s, bk_s, inv_s, rb_s, mk_s):
    @pl.when(pl.program_id(1) == 0)
    def _():
        tail_ref[...] = jnp.zeros_like(tail_ref)
        state_ref[...] = jnp.zeros_like(state_ref)

    tm = h_ref.shape[0]
    W = BRANCH_W
    N = RW_HEAD_DIM
    L = RW_CHUNK
    z = _dot(h_ref[...].astype(BF16), w_ref[...])
    zprev = _shift_rows(z, 1, tail_ref[...])
    tail_ref[...] = z[tm - V7X_SUBLANES:, :]
    z = z + (zprev - z) * mu_ref[...]
    r = z[:, 0:W]
    k = z[:, W:2 * W]
    v = z[:, 2 * W:3 * W]
    xwa = z[:, 3 * W:3 * W + 2 * RW_LORA]
    xg = z[:, 3 * W + 2 * RW_LORA:]
    lane = lax.broadcasted_iota(jnp.int32, xwa.shape, 1)
    lora_in = jnp.where(lane < RW_LORA, jnp.tanh(xwa), xwa)
    lora = _dot(lora_in, lora_ref[...], precision=HIGHEST)
    w = -_softplus(-(w0_ref[...] + lora[:, 0:W])) - 0.5
    log_decay = -jnp.exp(w)
    a = jax.nn.sigmoid(a0_ref[...] + lora[:, W:2 * W])
    g = _dot(jax.nn.sigmoid(xg), gup_ref[...], precision=HIGHEST)
    ones = ones_ref[...]
    kk = k * kk_ref[...]
    kk = kk / jnp.maximum(jnp.sqrt(_dot(kk * kk, ones, precision=HIGHEST)), 1e-12)
    k = k * (1.0 + (a - 1.0) * ka_ref[...])
    bonus = _dot(r * k * rk_ref[...], ones, precision=HIGHEST) * v

    ti = lax.broadcasted_iota(jnp.int32, (L, L), 0)
    tj = lax.broadcasted_iota(jnp.int32, (L, L), 1)
    tri = (ti >= tj).astype(F32)
    eye = (ti == tj).astype(F32)
    t2 = lax.broadcasted_iota(jnp.int32, (2 * L, L), 0)
    s2 = lax.broadcasted_iota(jnp.int32, (2 * L, L), 1)
    mask2 = ((t2 < L) & (t2 > s2)) | (t2 - L >= s2)
    n_chunks = tm // L

    alpha = -kk
    beta = kk * a
    v_s[...] = v.astype(BF16)
    for c in range(n_chunks):
        rows = slice(c * L, (c + 1) * L)
        ld = log_decay[rows, :]
        cum = _dot(tri, ld, precision=HIGHEST)
        gam = jnp.exp(cum)
        gam_inv = jnp.exp(-cum)
        gend_s[c * V7X_SUBLANES:(c + 1) * V7X_SUBLANES, :] = jnp.broadcast_to(
            gam[L - 1:L, :], (V7X_SUBLANES, W))
        ar_all = jnp.concatenate([alpha[rows, :] * jnp.exp(cum - ld), r[rows, :] * gam],
                                 axis=0).astype(BF16)
        bk_all = jnp.concatenate([beta[rows, :] * gam_inv, k[rows, :] * gam_inv],
                                 axis=0).astype(BF16)
        ar_s[c] = ar_all
        bk_s[c] = bk_all

    heads = [(c, hd) for c in range(n_chunks) for hd in range(RW_HEADS)]
    hcols = lambda hd: slice(hd * N, (hd + 1) * N)
    invs, ps = [], []
    for c, hd in heads:
        pair = _dot_nt(ar_s[c, :, hcols(hd)], bk_s[c, :, hcols(hd)])
        m_b = jnp.where(mask2, pair[:, :L], 0.0)
        mk_s[c * RW_HEADS + hd] = jnp.where(mask2, pair[:, L:], 0.0).astype(BF16)
        rb_s[c * RW_HEADS + hd] = m_b[L:].astype(BF16)
        invs.append(eye + m_b[:L])
        ps.append(m_b[:L].astype(BF16))
    ps = [_dot(pb, pb) for pb in ps]
    for _ in range(int(math.log2(L)) - 2):
        both = [_dot(jnp.concatenate([p.astype(BF16), inv.astype(BF16)], axis=0), p.astype(BF16))
                for p, inv in zip(ps, invs)]
        ps = [b2[:L] for b2 in both]
        invs = [inv + b2[L:] for inv, b2 in zip(invs, both)]
    for i, (inv, p) in enumerate(zip(invs, ps)):
        inv_s[i] = (inv + _dot(inv.astype(BF16), p.astype(BF16))).astype(BF16)

    def chunk(c, carry):
        rows = pl.ds(pl.multiple_of(c * L, L), L)
        ar_all = ar_s[c]
        bk_all = bk_s[c]
        v_all = v_s[rows, :]
        gend = gend_s[pl.ds(pl.multiple_of(c * V7X_SUBLANES, V7X_SUBLANES), 1), :]
        hs = range(RW_HEADS)
        sts = [state_ref[hd] for hd in hs]
        bases = [_dot_nt(ar_all[:, hcols(hd)], sts[hd].astype(BF16))
                 + _dot(mk_s[c * RW_HEADS + hd], v_all[:, hcols(hd)]) for hd in hs]
        us = [_dot(inv_s[c * RW_HEADS + hd], bases[hd][:L].astype(BF16)).astype(BF16) for hd in hs]
        for hd in hs:
            y_s[rows, hcols(hd)] = bases[hd][L:] + _dot(rb_s[c * RW_HEADS + hd], us[hd])
        for hd in hs:
            st = sts[hd] + _dot_tn(jnp.concatenate([us[hd], v_all[:, hcols(hd)]], axis=0),
                                   bk_all[:, hcols(hd)])
            state_ref[hd] = st * gend[:, hcols(hd)]
        return carry

    lax.fori_loop(0, n_chunks, chunk, 0)

    y = y_s[...]
    inv_n = 1.0 / N
    m = _dot(y, ones, precision=HIGHEST) * inv_n
    yc = y - m
    var = _dot(yc * yc, ones, precision=HIGHEST) * inv_n
    yn = yc * lax.rsqrt(var + RW_LN_EPS) * lng_ref[...] + lnb_ref[...]
    o_ref[...] = ((yn + bonus) * g).astype(o_ref.dtype)


def _mixer_rwkv(h, w_c, mu, w0, a0, lora_w, g_up, k_k, k_a, r_k, ln_g, ln_b, head_ones):
    B, S, D = h.shape
    tm = SEQ_TILE
    W = BRANCH_W
    L = RW_CHUNK
    nc = tm // L
    blk = lambda b, j: (b, j, 0)
    row = lambda n: _full((1, n))
    return pl.pallas_call(
        _rwkv_kernel,
        grid=(B, S // tm),
        in_specs=[pl.BlockSpec((None, tm, D), blk), _full((D, RW_IN)), row(RW_IN), row(W), row(W),
                  _full((2 * RW_LORA, 2 * W)), _full((RW_GATE_LORA, W)), row(W), row(W), row(W),
                  row(W), row(W), _full((W, W))],
        out_specs=pl.BlockSpec((None, tm, W), blk),
        out_shape=jax.ShapeDtypeStruct((B, S, W), BF16),
        scratch_shapes=[pltpu.VMEM((V7X_SUBLANES, RW_IN), F32),
                        pltpu.VMEM((RW_HEADS, RW_HEAD_DIM, RW_HEAD_DIM), F32),
                        pltpu.VMEM((tm, W), BF16),
                        pltpu.VMEM((tm, W), F32),
                        pltpu.VMEM((nc * V7X_SUBLANES, W), F32),
                        pltpu.VMEM((nc, 2 * L, W), BF16),
                        pltpu.VMEM((nc, 2 * L, W), BF16),
                        pltpu.VMEM((nc * RW_HEADS, L, L), BF16),
                        pltpu.VMEM((nc * RW_HEADS, L, L), BF16),
                        pltpu.VMEM((nc * RW_HEADS, 2 * L, L), BF16)],
        compiler_params=_cparams("parallel", "arbitrary"),
    )(h, w_c, mu, w0, a0, lora_w, g_up, k_k, k_a, r_k, ln_g, ln_b, head_ones)


def _s5_kernel(h_ref, w_ref, bre_ref, bim_ref, are_ref, aim_ref, cre_ref, cim_ref,
               d_ref, gw_ref, gb_ref, o_ref, xr_s, xi_s, st_s):
    @pl.when(pl.program_id(1) == 0)
    def _():
        st_s[...] = jnp.zeros_like(st_s)

    tm = h_ref.shape[0]
    u = _dot(h_ref[...].astype(BF16), w_ref[...])
    ub = u.astype(BF16)
    xr_s[...] = _dot(ub, bre_ref[...])
    xi_s[...] = _dot(ub, bim_ref[...])
    ar = are_ref[...]
    ai = aim_ref[...]

    def step(t, carry):
        xr, xi = carry
        row = pl.ds(t, 1)
        nr = ar * xr - ai * xi + xr_s[row, :]
        ni = ar * xi + ai * xr + xi_s[row, :]
        xr_s[row, :] = nr
        xi_s[row, :] = ni
        return nr, ni

    xr, xi = lax.fori_loop(0, tm, step, (st_s[0:1, :], st_s[1:2, :]), unroll=8)
    st_s[0:1, :] = xr
    st_s[1:2, :] = xi
    y = (_dot(xr_s[...].astype(BF16), cre_ref[...]) - _dot(xi_s[...].astype(BF16), cim_ref[...])
         + d_ref[...] * u)
    y = jax.nn.gelu(y)
    gate = jax.nn.sigmoid(_dot(y.astype(BF16), gw_ref[...]) + gb_ref[...])
    o_ref[...] = (y * gate).astype(o_ref.dtype)


def _mixer_s5(h, w_d, bre, bim, are, aim, cre, cim, d, glu_w, glu_b):
    B, S, D = h.shape
    tm = SEQ_TILE
    W = BRANCH_W
    blk = lambda b, j: (b, j, 0)
    return pl.pallas_call(
        _s5_kernel,
        grid=(B, S // tm),
        in_specs=[pl.BlockSpec((None, tm, D), blk), _full((D, W)),
                  _full((W, S5_N)), _full((W, S5_N)), _full((1, S5_N)), _full((1, S5_N)),
                  _full((S5_N, W)), _full((S5_N, W)), _full((1, W)), _full((W, W)), _full((1, W))],
        out_specs=pl.BlockSpec((None, tm, W), blk),
        out_shape=jax.ShapeDtypeStruct((B, S, W), BF16),
        scratch_shapes=[pltpu.VMEM((tm, S5_N), F32), pltpu.VMEM((tm, S5_N), F32),
                        pltpu.VMEM((V7X_SUBLANES, S5_N), F32)],
        compiler_params=_cparams("parallel", "arbitrary"),
    )(h, w_d, bre, bim, are, aim, cre, cim, d, glu_w, glu_b)


def _merge_kernel(h_ref, oa_ref, ob_ref, oc_ref, od_ref, wg_ref, gb_ref, br_ref, wo_ref,
                  g_ref, b_ref, o_ref):
    D = D_MODEL
    h = h_ref[...]
    hb = h.astype(BF16)
    merged = None
    for i, br_in in enumerate((oa_ref, ob_ref, oc_ref, od_ref)):
        gate = jax.nn.sigmoid(_dot(hb, wg_ref[:, i * D:(i + 1) * D]) + gb_ref[i:i + 1, :])
        term = gate * _dot(br_in[...], br_ref[i])
        merged = term if merged is None else merged + term
    y = _dot(merged.astype(BF16), wo_ref[...])
    o_ref[...] = _layer_norm(DN_ALPHA * h + y, g_ref[...], b_ref[...])


def _merge(h2, oa, ob, oc, od, w_g, gate_b, br_proj, w_out, g, b):
    T, D = h2.shape
    tm = TOKEN_TILE
    W = BRANCH_W
    tok = lambda n: pl.BlockSpec((tm, n), lambda i: (i, 0))
    return pl.pallas_call(
        _merge_kernel,
        grid=(T // tm,),
        in_specs=[tok(D), tok(W), tok(W), tok(W), tok(W),
                  _full((D, N_BRANCH * D)), _full((N_BRANCH, D)), _full((N_BRANCH, W, D)),
                  _full((D, D)), _full((1, D)), _full((1, D))],
        out_specs=tok(D),
        out_shape=jax.ShapeDtypeStruct((T, D), F32),
        compiler_params=_cparams("parallel"),
    )(h2, oa, ob, oc, od, w_g, gate_b, br_proj, w_out, g, b)


def _kv_kernel(m_ref, wk_ref, wv_ref, k_ref, v_ref):
    mb = m_ref[...].astype(BF16)
    k_ref[...] = _dot(mb, wk_ref[...]).astype(k_ref.dtype)
    v_ref[...] = _dot(mb, wv_ref[...]).astype(v_ref.dtype)


def _kv_proj(mem2, wk, wv):
    R, D = mem2.shape
    tm = TOKEN_TILE
    tok = pl.BlockSpec((tm, D), lambda i: (i, 0))
    return pl.pallas_call(
        _kv_kernel,
        grid=(R // tm,),
        in_specs=[tok, _full((D, D)), _full((D, D))],
        out_specs=[tok, tok],
        out_shape=[jax.ShapeDtypeStruct((R, D), BF16)] * 2,
        compiler_params=_cparams("parallel"),
    )(mem2, wk, wv)


def _attn_kernel(h_ref, wq_ref, k_ref, v_ref, wo_ref, g_ref, b_ref, o_ref, ot_ref):
    h = h_ref[...]
    q = _dot(h.astype(BF16), wq_ref[...]).astype(BF16)
    scale = XA_HEAD_DIM ** -0.5
    outs = []
    for hd in range(XA_HEADS):
        cols = slice(hd * XA_HEAD_DIM, (hd + 1) * XA_HEAD_DIM)
        s = _dot_nt(q[:, cols], k_ref[:, cols]) * scale
        s = s - jnp.max(s, axis=-1, keepdims=True)
        p = jnp.exp(s)
        p = p / jnp.sum(p, axis=-1, keepdims=True)
        outs.append(_dot(p.astype(BF16), v_ref[:, cols]))
    o = jnp.concatenate(outs, axis=-1).astype(BF16)
    y = _dot(o, wo_ref[...])
    out = _layer_norm(DN_ALPHA * h + y, g_ref[...], b_ref[...])
    o_ref[...] = out
    _store_row_tiles(ot_ref, out)


def _cross_attention(h, kmem, vmem, wq, wo, g, b):
    B, S, D = h.shape
    M = kmem.shape[1]
    tm = TOKEN_TILE
    n = S // tm
    blk = lambda bi, j: (bi, j, 0)
    mem = lambda bi, j: (bi, 0, 0)
    return pl.pallas_call(
        _attn_kernel,
        grid=(B, n),
        in_specs=[pl.BlockSpec((None, tm, D), blk), _full((D, D)),
                  pl.BlockSpec((None, M, D), mem), pl.BlockSpec((None, M, D), mem),
                  _full((D, D)), _full((1, D)), _full((1, D))],
        out_specs=[pl.BlockSpec((None, tm, D), blk),
                   pl.BlockSpec((tm * ROW_TILE, V7X_LANES), lambda bi, j: (bi * n + j, 0))],
        out_shape=[jax.ShapeDtypeStruct((B, S, D), F32),
                   jax.ShapeDtypeStruct((B * S * ROW_TILE, V7X_LANES), F32)],
        compiler_params=_cparams("parallel", "parallel"),
    )(h, wq, kmem, vmem, wo, g, b)


def _router_kernel(h_ref, w_ref, b_ref, route_ref, rw_ref, cnt_ref, carry_ref):
    @pl.when(pl.program_id(0) == 0)
    def _():
        carry_ref[...] = jnp.zeros_like(carry_ref)

    tm = h_ref.shape[0]
    E = N_EXPERTS
    logits = _dot(h_ref[...], w_ref[...], precision=HIGHEST) + b_ref[...]
    lane = lax.broadcasted_iota(jnp.int32, (tm, E), 1)
    work = logits
    vals, hots, idxs = [], [], []
    for _ in range(TOP_K):
        m = jnp.max(work, axis=-1, keepdims=True)
        idx = jnp.min(jnp.where(work == m, lane, E), axis=-1, keepdims=True)
        hot = lane == idx
        work = jnp.where(hot, -jnp.inf, work)
        vals.append(m)
        hots.append(hot)
        idxs.append(idx)
    exps = [jnp.exp(v - vals[0]) for v in vals]
    denom = exps[0] + exps[1] + exps[2] + exps[3]
    multi = (hots[0] | hots[1] | hots[2] | hots[3]).astype(F32)
    ti = lax.broadcasted_iota(jnp.int32, (tm, tm), 0)
    tj = lax.broadcasted_iota(jnp.int32, (tm, tm), 1)
    before = (ti > tj).astype(BF16)
    prefix = _dot(before, multi.astype(BF16)) + carry_ref[0:1, :]
    out_lane = lax.broadcasted_iota(jnp.int32, (tm, ROUTE_LANES), 1)
    route = jnp.zeros((tm, ROUTE_LANES), jnp.int32)
    rw = jnp.zeros((tm, ROUTE_LANES), F32)
    for kk in range(TOP_K):
        rank = jnp.sum(jnp.where(hots[kk], prefix, 0.0), axis=-1, keepdims=True).astype(jnp.int32)
        route = jnp.where(out_lane == kk, idxs[kk], route)
        route = jnp.where(out_lane == TOP_K + kk, rank, route)
        rw = jnp.where(out_lane == kk, exps[kk] / denom, rw)
    route_ref[...] = route
    rw_ref[...] = rw
    carry_ref[0:1, :] = carry_ref[0:1, :] + jnp.sum(multi, axis=0, keepdims=True)
    cnt_ref[...] = carry_ref[...]


def _router(h2, router_w, router_b):
    T, D = h2.shape
    tm = TOKEN_TILE
    E = N_EXPERTS
    return pl.pallas_call(
        _router_kernel,
        grid=(T // tm,),
        in_specs=[pl.BlockSpec((tm, D), lambda i: (i, 0)), _full((D, E)), _full((1, E))],
        out_specs=[pl.BlockSpec((tm, ROUTE_LANES), lambda i: (i, 0)),
                   pl.BlockSpec((tm, ROUTE_LANES), lambda i: (i, 0)),
                   _full((V7X_SUBLANES, E))],
        out_shape=[jax.ShapeDtypeStruct((T, ROUTE_LANES), jnp.int32),
                   jax.ShapeDtypeStruct((T, ROUTE_LANES), F32),
                   jax.ShapeDtypeStruct((V7X_SUBLANES, E), F32)],
        scratch_shapes=[pltpu.VMEM((V7X_SUBLANES, E), F32)],
        compiler_params=_cparams("arbitrary"),
    )(h2, router_w, router_b)


def _dest_kernel(route_ref, start_ref, dest_ref):
    route = route_ref[...]
    tm = route.shape[0]
    lane = lax.broadcasted_iota(jnp.int32, (tm, N_EXPERTS), 1)
    out_lane = lax.broadcasted_iota(jnp.int32, (tm, ROUTE_LANES), 1)
    dest = jnp.zeros((tm, ROUTE_LANES), jnp.int32)
    start = start_ref[...]
    for kk in range(TOP_K):
        idx = route[:, kk:kk + 1]
        base = jnp.sum(jnp.where(lane == idx, start, 0), axis=-1, keepdims=True)
        dest = jnp.where(out_lane == kk, base + route[:, TOP_K + kk:TOP_K + kk + 1], dest)
    dest_ref[...] = dest


def _dest_slots(route, pad_start):
    T = route.shape[0]
    tm = TOKEN_TILE * 2
    return pl.pallas_call(
        _dest_kernel,
        grid=(T // tm,),
        in_specs=[pl.BlockSpec((tm, ROUTE_LANES), lambda i: (i, 0)), _full((1, N_EXPERTS))],
        out_specs=pl.BlockSpec((tm, ROUTE_LANES), lambda i: (i, 0)),
        out_shape=jax.ShapeDtypeStruct((T, ROUTE_LANES), jnp.int32),
        compiler_params=_cparams("parallel"),
    )(route, pad_start.reshape(1, N_EXPERTS))


def _slot_source_kernel(fill_lo_ref, fill_hi_ref, dest_hbm, row_ref, dest_s, sem, *, n_tok):
    i = pl.program_id(0)
    n = dest_s.shape[0]

    @pl.when(i == 0)
    def _():
        def fill(s, carry):
            parity = (s // MOE_TILE + 1) % 2
            row_ref[s] = n_tok * TOP_K + parity * MOE_TILE + s % MOE_TILE
            return carry

        lax.fori_loop(0, MOE_TILE, fill, 0)
        for e in range(fill_lo_ref.shape[0]):
            lax.fori_loop(MOE_TILE + fill_lo_ref[e], MOE_TILE + fill_hi_ref[e], fill, 0)

    idx_copy = pltpu.make_async_copy(dest_hbm.at[i], dest_s, sem)
    idx_copy.start()
    idx_copy.wait()
    slots = row_ref.at[pl.ds(MOE_TILE, row_ref.shape[0] - MOE_TILE)]
    first_tok = i * (n // TOP_K)

    def body(t, carry):
        for kk in range(TOP_K):
            slots[dest_s[t * TOP_K + kk]] = (kk * n_tok + first_tok) + t
        return carry

    lax.fori_loop(0, n // TOP_K, body, 0, unroll=8)


def _slot_sources(fill_lo, fill_hi, dest_tiles, n_slots):
    n_tok_tiles, n = dest_tiles.shape
    return pl.pallas_call(
        functools.partial(_slot_source_kernel, n_tok=n_tok_tiles * n // TOP_K),
        grid_spec=pltpu.PrefetchScalarGridSpec(
            num_scalar_prefetch=2,
            grid=(n_tok_tiles,),
            in_specs=[pl.BlockSpec(memory_space=pl.ANY)],
            out_specs=pl.BlockSpec(memory_space=pltpu.SMEM),
            scratch_shapes=[pltpu.SMEM((n,), jnp.int32), pltpu.SemaphoreType.DMA]),
        out_shape=jax.ShapeDtypeStruct((MOE_TILE + n_slots,), jnp.int32),
        compiler_params=_cparams("arbitrary"),
    )(fill_lo, fill_hi, dest_tiles)


GU_BLOCK = 2 * V7X_LANES


def _unzip_kernel(w_ref, p_ref, o_ref):
    for c in range(w_ref.shape[1] // GU_BLOCK):
        cols = slice(c * GU_BLOCK, (c + 1) * GU_BLOCK)
        o_ref[:, cols] = _dot(w_ref[:, cols].astype(BF16), p_ref[...]).astype(o_ref.dtype)


def _unzip_gate_up(w1_layers, layer):
    _, E, D, F2 = w1_layers.shape
    tr = TOKEN_TILE
    j = jnp.arange(GU_BLOCK)
    perm = (j[:, None] == jnp.where(j < V7X_LANES, 2 * j, 2 * (j - V7X_LANES) + 1)[None, :])
    return pl.pallas_call(
        _unzip_kernel,
        grid=(E, D // tr),
        in_specs=[pl.BlockSpec((None, None, tr, F2), lambda e, r: (layer, e, r, 0)),
                  _full((GU_BLOCK, GU_BLOCK))],
        out_specs=pl.BlockSpec((None, tr, F2), lambda e, r: (e, r, 0)),
        out_shape=jax.ShapeDtypeStruct((E, D, F2), BF16),
        compiler_params=_cparams("parallel", "parallel"),
    )(w1_layers, perm.astype(BF16))


def _unzip_bias(b1):
    E, F2 = b1.shape
    return b1.reshape(E, F2 // GU_BLOCK, V7X_LANES, 2).transpose(0, 1, 3, 2).reshape(E, 1, F2)


def _expert_ffn(x, w1_ref, b1_ref, w2_ref, b2_ref):
    hid = _dot(x, w1_ref[...]) + b1_ref[...]
    acts = []
    for c in range(hid.shape[1] // GU_BLOCK):
        gate = jnp.minimum(hid[:, c * GU_BLOCK:c * GU_BLOCK + V7X_LANES], SWIGLU_LIMIT)
        up = jnp.clip(hid[:, c * GU_BLOCK + V7X_LANES:(c + 1) * GU_BLOCK],
                      -SWIGLU_LIMIT, SWIGLU_LIMIT)
        acts.append(((up + 1.0) * gate * jax.nn.sigmoid(SWIGLU_ALPHA * gate)).astype(BF16))
    return _dot(jnp.concatenate(acts, axis=-1), w2_ref[...].astype(BF16)) + b2_ref[...]


def _expert_kernel(blk_e_ref, n_used_ref, src_ref, h_hbm, w1_ref, b1_ref, w2_ref, b2_ref, yk_hbm,
                   xbuf0, xbuf1, ybuf0, ybuf1, gsem, ssem):
    del blk_e_ref
    i = pl.program_id(0)
    n_used = n_used_ref[0]
    xbufs = (xbuf0, xbuf1)
    ybufs = (ybuf0, ybuf1)
    R = ROW_TILE
    tm = xbuf0.shape[0] // R
    T = h_hbm.shape[0] // R
    n_assign = T * TOP_K
    def tile_rows(r):
        return pl.ds(r * R if isinstance(r, int) else pl.multiple_of(r * R, R), R)

    def gather_row(tile, b, j, priority):
        row = src_ref[(tile + 1) * tm + j]
        tok = row & (T - 1) if T & (T - 1) == 0 else lax.rem(row, T)
        pltpu.make_async_copy(h_hbm.at[tile_rows(tok)], xbufs[b].at[tile_rows(j)],
                              gsem.at[b]).start(priority=priority)

    def scatter_row(tile, b, j, priority):
        row = src_ref[(tile + 1) * tm + j]
        pltpu.make_async_copy(ybufs[b].at[tile_rows(j)], yk_hbm.at[tile_rows(row)],
                              ssem.at[b]).start(priority=priority)

    def wait_gather(b):
        pltpu.make_async_copy(h_hbm.at[pl.ds(0, tm * R)], xbufs[b], gsem.at[b]).wait()

    def wait_scatter(b):
        pltpu.make_async_copy(ybufs[b], yk_hbm.at[pl.ds(0, tm * R)], ssem.at[b]).wait()

    def rows_loop(fn):
        def body(j, carry):
            fn(j)
            return carry
        lax.fori_loop(0, tm, body, 0)

    @pl.when(i == 0)
    def _():
        rows_loop(lambda j: gather_row(0, 0, j, 0))
        ybuf0[...] = jnp.zeros_like(ybuf0)
        ybuf1[...] = jnp.zeros_like(ybuf1)
        pltpu.make_async_copy(ybuf0, yk_hbm.at[pl.ds(n_assign * R, tm * R)], ssem.at[0]).start()

    for b in range(2):
        @pl.when((i < n_used) & (i % 2 == b))
        def _(b=b):
            wait_gather(b)
            wait_scatter(b)
            for j in range(tm):
                gather_row(i + 1, 1 - b, j, 0)
                scatter_row(i - 1, 1 - b, j, 1)
            x = _load_row_tiles(xbufs[b]).astype(BF16)
            _store_row_tiles(ybufs[b], _expert_ffn(x, w1_ref, b1_ref, w2_ref, b2_ref))

        @pl.when((i == n_used) & (i % 2 == b))
        def _(b=b):
            wait_gather(b)
            wait_scatter(b)
            rows_loop(lambda j: scatter_row(i - 1, 1 - b, j, 0))
            wait_scatter(1 - b)


def _experts(blk_e, n_used, src, h_tiles, w1, b1, w2_layers, layer, b2):
    R = ROW_TILE
    T = h_tiles.shape[0] // R
    D = D_MODEL
    tm = MOE_TILE
    F = D_FF
    n_tiles = src.shape[0] // tm - 1
    per_e = lambda shape: pl.BlockSpec((None,) + shape, lambda i, be, nu, sr: (be[i], 0, 0))
    w2_spec = pl.BlockSpec((None, None, F, D), lambda i, be, nu, sr: (layer, be[i], 0, 0))
    return pl.pallas_call(
        _expert_kernel,
        grid_spec=pltpu.PrefetchScalarGridSpec(
            num_scalar_prefetch=3,
            grid=(n_tiles,),
            in_specs=[pl.BlockSpec(memory_space=pl.ANY),
                      per_e((D, 2 * F)), per_e((1, 2 * F)), w2_spec, per_e((1, D))],
            out_specs=pl.BlockSpec(memory_space=pl.ANY),
            scratch_shapes=[pltpu.VMEM((tm * R, V7X_LANES), F32)] * 4
                           + [pltpu.SemaphoreType.DMA((2,)), pltpu.SemaphoreType.DMA((2,))]),
        out_shape=jax.ShapeDtypeStruct(((T * TOP_K + 2 * tm) * R, V7X_LANES), F32),
        compiler_params=_cparams("arbitrary"),
    )(blk_e, n_used, src, h_tiles, w1, b1, w2_layers, b2)


def _combine_kernel(y0_ref, y1_ref, y2_ref, y3_ref, h_ref, rw_ref, g_ref, b_ref, o_ref):
    rw = rw_ref[...]
    moe = rw[:, 0:1] * _load_row_tiles(y0_ref)
    for kk, y_ref in enumerate((y1_ref, y2_ref, y3_ref), start=1):
        moe = moe + rw[:, kk:kk + 1] * _load_row_tiles(y_ref)
    o_ref[...] = _layer_norm(DN_ALPHA * h_ref[...] + moe, g_ref[...], b_ref[...])


def _combine(yk, h2, rw, g, b):
    T, D = h2.shape
    tm = TOKEN_TILE
    n = T // tm
    choice = lambda kk: pl.BlockSpec((tm * ROW_TILE, V7X_LANES), lambda i: (kk * n + i, 0))
    return pl.pallas_call(
        _combine_kernel,
        grid=(n,),
        in_specs=[choice(kk) for kk in range(TOP_K)]
                 + [pl.BlockSpec((tm, D), lambda i: (i, 0)),
                    pl.BlockSpec((tm, ROUTE_LANES), lambda i: (i, 0)),
                    _full((1, D)), _full((1, D))],
        out_specs=pl.BlockSpec((tm, D), lambda i: (i, 0)),
        out_shape=jax.ShapeDtypeStruct((T, D), F32),
        compiler_params=_cparams("parallel"),
    )(yk, yk, yk, yk, h2, rw, g, b)


def _s5_discretise(a_re, a_im, b_re, b_im, c_re, c_im, log_dt):
    G, P, C = S5_GROUPS, S5_STATE, S5_CH
    dt = jnp.exp(log_dt)[:, None]
    mag = jnp.exp(a_re * dt)
    abar_re = mag * jnp.cos(a_im * dt)
    abar_im = mag * jnp.sin(a_im * dt)
    den = a_re * a_re + a_im * a_im
    num_re = abar_re - 1.0
    coef_re = (num_re * a_re + abar_im * a_im) / den
    coef_im = (abar_im * a_re - num_re * a_im) / den
    bbar_re = coef_re[..., None] * b_re - coef_im[..., None] * b_im
    bbar_im = coef_re[..., None] * b_im + coef_im[..., None] * b_re
    eye = jnp.eye(G, dtype=F32)
    to_in = lambda m: jnp.einsum('gpc,gh->gchp', m, eye).reshape(G * C, G * P)
    to_out = lambda m: jnp.einsum('gcp,gh->gphc', m, eye).reshape(G * P, G * C)
    return (to_in(bbar_re).astype(BF16), to_in(bbar_im).astype(BF16),
            abar_re.reshape(1, G * P), abar_im.reshape(1, G * P),
            to_out(c_re).astype(BF16), to_out(c_im).astype(BF16))


def _moe_layer(h2, h_tiles, layer, router_w, router_b, w1_layers, b1, w2_layers, b2, g, b):
    T, D = h2.shape
    E = N_EXPERTS
    tm = TOKEN_TILE
    route, rw, counts = _router(h2, router_w, router_b.reshape(1, E))
    counts = counts[0].astype(jnp.int32)
    padded = (counts + MOE_TILE - 1) // MOE_TILE * MOE_TILE
    pad_end = jnp.cumsum(padded)
    pad_start = pad_end - padded
    n_tiles = T * TOP_K // MOE_TILE + E
    n_slots = n_tiles * MOE_TILE
    tile_start = jnp.arange(n_tiles, dtype=jnp.int32) * MOE_TILE
    blk_e = jnp.minimum(jnp.sum(tile_start[:, None] >= pad_end[None, :], axis=1), E - 1).astype(jnp.int32)
    n_used = (pad_end[E - 1:] // MOE_TILE).astype(jnp.int32)
    dest = _dest_slots(route, pad_start.astype(jnp.int32))
    dest_tiles = dest[:, :TOP_K].reshape(T // tm, tm * TOP_K)
    fill_lo = jnp.concatenate([pad_start + counts, pad_end[E - 1:]]).astype(jnp.int32)
    fill_hi = jnp.concatenate([pad_end, jnp.full((1,), n_slots)]).astype(jnp.int32)
    src = _slot_sources(fill_lo, fill_hi, dest_tiles, n_slots)
    yk = _experts(blk_e, n_used, src, h_tiles, _unzip_gate_up(w1_layers, layer), _unzip_bias(b1),
                  w2_layers, layer, b2[:, None, :])
    return _combine(yk, h2, rw, g.reshape(1, D), b.reshape(1, D))


def kernel(x, mem, ln_in_g, ln_in_b, w_in, conv_w, sg_norm_g, sg_norm_b, sg_w, sg_b, rw_mu, rw_w0, rw_w_up, rw_a0, rw_a_up, rw_g_up, rw_k_k, rw_k_a, rw_r_k, rw_ln_g, rw_ln_b, s5_a_re, s5_a_im, s5_b_re, s5_b_im, s5_c_re, s5_c_im, s5_d, s5_log_dt, s5_glu_w, s5_glu_b, br_proj, gate_b, w_out, ln1_g, ln1_b, xa_wq, xa_wk, xa_wv, xa_wo, ln2_g, ln2_b, router_w, router_b, ex_w1, ex_b1, ex_w2, ex_b2, ln3_g, ln3_b):
    B, S, D = x.shape
    M = mem.shape[1]
    T = B * S
    W = BRANCH_W
    row = lambda v: v.reshape(1, -1)
    pos = jnp.arange(SG_BLOCK)
    sg_mask = (pos[None, :] // CHUNK) <= (pos[:, None] // CHUNK)
    head_ones = jnp.kron(jnp.eye(RW_HEADS, dtype=F32), jnp.ones((RW_HEAD_DIM, RW_HEAD_DIM), F32))
    mem2 = mem.reshape(B * M, D)

    h = _input_ln(x.reshape(T, D), ln_in_g, ln_in_b)
    for l in range(DEPTH):
        w_l = w_in[l].astype(BF16)
        h3 = h.reshape(B, S, D)
        sg_wm = jnp.where(sg_mask[None], sg_w[l], 0.0).astype(BF16)
        sg_bias = jnp.repeat(sg_b[l].T, W // SG_GROUPS, axis=1)
        o_a, o_b = _mixers_ab(h3, w_l[:, :OFF_C], conv_w[l], row(sg_norm_g[l]), row(sg_norm_b[l]),
                              sg_wm, sg_bias)
        zero = jnp.zeros((RW_LORA, W), F32)
        lora_w = jnp.concatenate([jnp.concatenate([rw_w_up[l], zero], axis=1),
                                  jnp.concatenate([zero, rw_a_up[l]], axis=1)], axis=0)
        o_c = _mixer_rwkv(h3, w_l[:, OFF_C:OFF_D], row(rw_mu[l]), row(rw_w0[l]), row(rw_a0[l]),
                          lora_w, rw_g_up[l], row(rw_k_k[l]), row(rw_k_a[l]), row(rw_r_k[l]),
                          row(rw_ln_g[l]), row(rw_ln_b[l]), head_ones)
        s5p = _s5_discretise(s5_a_re[l], s5_a_im[l], s5_b_re[l], s5_b_im[l], s5_c_re[l],
                             s5_c_im[l], s5_log_dt[l])
        o_d = _mixer_s5(h3, w_l[:, OFF_D:OFF_G], *s5p, row(s5_d[l]),
                        s5_glu_w[l].astype(BF16), row(s5_glu_b[l]))
        flat = lambda o: o.reshape(T, W)
        h = _merge(h, flat(o_a), flat(o_b), flat(o_c), flat(o_d), w_l[:, OFF_G:], gate_b[l],
                   br_proj[l].astype(BF16), w_out[l].astype(BF16), row(ln1_g[l]), row(ln1_b[l]))
        kmem, vmem = _kv_proj(mem2, xa_wk[l].astype(BF16), xa_wv[l].astype(BF16))
        h, h_tiles = _cross_attention(h.reshape(B, S, D), kmem.reshape(B, M, D),
                                      vmem.reshape(B, M, D), xa_wq[l].astype(BF16),
                                      xa_wo[l].astype(BF16), row(ln2_g[l]), row(ln2_b[l]))
        h = _moe_layer(h.reshape(T, D), h_tiles, l, router_w[l], router_b[l], ex_w1, ex_b1[l],
                       ex_w2, ex_b2[l], ln3_g[l], ln3_b[l])
    return h.reshape(B, S, D)
```

```python
import functools
import math

import jax
import jax.numpy as jnp
from jax import lax
from jax.experimental import pallas as pl
from jax.experimental.pallas import tpu as pltpu

F32 = jnp.float32
BF16 = jnp.bfloat16

D_MODEL = 1024
DEPTH = 2
CHUNK = 64
BRANCH_W = 256
N_BRANCH = 4
CONV_W = 3
SG_BLOCK = 128
SG_GROUPS = 4
RW_HEADS = 4
RW_HEAD_DIM = 64
RW_LORA = 64
RW_GATE_LORA = 128
RW_IN = 3 * BRANCH_W + 2 * RW_LORA + RW_GATE_LORA
RW_LN_EPS = 64e-5
S5_CH = 16
S5_GROUPS = BRANCH_W // S5_CH
S5_STATE = 64
S5_N = S5_GROUPS * S5_STATE
OFF_B = 3 * BRANCH_W
OFF_C = OFF_B + 2 * BRANCH_W
OFF_D = OFF_C + RW_IN
OFF_G = OFF_D + BRANCH_W
XA_HEADS = 4
XA_HEAD_DIM = D_MODEL // XA_HEADS
N_EXPERTS = 32
TOP_K = 4
TOP_K_BITS = 2
D_FF = D_MODEL
SWIGLU_LIMIT = 7.0
SWIGLU_ALPHA = 1.702
LN_EPS = 1e-5
DN_ALPHA = (2 * DEPTH) ** 0.25

V7X_LANES = 128
V7X_SUBLANES = 8
V7X_VMEM_LIMIT_BYTES = 56 * 1024 * 1024
V7X_DMA_PRIORITIES = 2
DMA_PRIORITIES = V7X_DMA_PRIORITIES
TOKEN_TILE = 512
SEQ_TILE = 256
RW_CHUNK = 64
MOE_TILE = 512
ROUTE_LANES = 128


def _cparams(*sem):
    return pltpu.CompilerParams(dimension_semantics=sem,
                                vmem_limit_bytes=V7X_VMEM_LIMIT_BYTES)


def _full(shape):
    nd = len(shape)
    return pl.BlockSpec(shape, lambda *_: (0,) * nd)


def _layer_norm(x, g, b, eps=LN_EPS):
    mu = jnp.mean(x, axis=-1, keepdims=True)
    xc = x - mu
    var = jnp.mean(xc * xc, axis=-1, keepdims=True)
    return xc * lax.rsqrt(var + eps) * g + b


def _contract(a, b, dims, **kw):
    return lax.dot_general(a, b, (dims, ((), ())), preferred_element_type=F32, **kw)


def _dot(a, b, **kw):
    return _contract(a, b, ((1,), (0,)), **kw)


def _bf16_terms(x, n):
    terms = []
    for _ in range(n):
        t = x.astype(BF16)
        terms.append(t)
        x = x - t.astype(F32)
    return terms


def _dot_split(a, b):
    ah, al = _bf16_terms(a, 2)
    bh, bl = _bf16_terms(b, 2)
    return _dot(jnp.concatenate([ah, ah, al], axis=1), jnp.concatenate([bh, bl, bh], axis=0))


def _dot_select(x, sel_stack):
    n = sel_stack.shape[0] // x.shape[1]
    return _dot(jnp.concatenate(_bf16_terms(x, n), axis=1), sel_stack)


def _dot_nt(a, b, **kw):
    return _contract(a, b, ((1,), (1,)), **kw)


def _dot_tn(a, b, **kw):
    return _contract(a, b, ((0,), (0,)), **kw)


ROW_TILE = D_MODEL // V7X_LANES


def _store_row_tiles(ref, x):
    n = x.shape[0]
    for c in range(ROW_TILE):
        ref[pl.ds(c, n, stride=ROW_TILE), :] = x[:, c * V7X_LANES:(c + 1) * V7X_LANES]


def _load_row_tiles(ref):
    n = ref.shape[0] // ROW_TILE
    return jnp.concatenate([ref[pl.ds(c, n, stride=ROW_TILE), :] for c in range(ROW_TILE)], axis=-1)


def _shift_rows(x, n, tail):
    rolled = pltpu.roll(x, n, 0)
    row = lax.broadcasted_iota(jnp.int32, x.shape, 0)
    out = rolled
    for i in range(n):
        src = tail[V7X_SUBLANES - n + i:V7X_SUBLANES - n + i + 1, :]
        out = jnp.where(row == i, src, out)
    return out


def _ln_kernel(x_ref, g_ref, b_ref, o_ref):
    o_ref[...] = _layer_norm(x_ref[...], g_ref[...], b_ref[...])


def _input_ln(x2, g, b):
    T, D = x2.shape
    tm = TOKEN_TILE * 2
    return pl.pallas_call(
        _ln_kernel,
        grid=(T // tm,),
        in_specs=[pl.BlockSpec((tm, D), lambda i: (i, 0)), _full((1, D)), _full((1, D))],
        out_specs=pl.BlockSpec((tm, D), lambda i: (i, 0)),
        out_shape=jax.ShapeDtypeStruct((T, D), F32),
        compiler_params=_cparams("parallel"),
    )(x2, g.reshape(1, D), b.reshape(1, D))


def _ab_kernel(h_ref, w_ref, cw_ref, ng_ref, nb_ref, sw_ref, sb_ref,
               oa_ref, ob_ref, tail_ref):
    @pl.when(pl.program_id(1) == 0)
    def _():
        tail_ref[...] = jnp.zeros_like(tail_ref)

    tm = h_ref.shape[0]
    W = BRANCH_W
    z = _dot(h_ref[...].astype(BF16), w_ref[...])
    ch = z[:, W:2 * W] * z[:, 2 * W:3 * W]
    tail = tail_ref[...]
    s1 = _shift_rows(ch, 1, tail)
    s2 = _shift_rows(ch, 2, tail)
    tail_ref[...] = ch[tm - V7X_SUBLANES:, :]
    cw = cw_ref[...]
    y = cw[2:3, :] * ch + cw[1:2, :] * s1 + cw[0:1, :] * s2
    oa_ref[...] = (z[:, 0:W] * y).astype(oa_ref.dtype)
    u = z[:, 3 * W:4 * W]
    v = _layer_norm(z[:, 4 * W:5 * W], ng_ref[...], nb_ref[...]).astype(BF16)
    gw = W // SG_GROUPS
    grp = lax.broadcasted_iota(jnp.int32, (SG_BLOCK, W), 1) // gw
    for blk in range(tm // SG_BLOCK):
        rows = slice(blk * SG_BLOCK, (blk + 1) * SG_BLOCK)
        vb = v[rows, :]
        sv = sb_ref[...]
        for g in range(SG_GROUPS):
            sv = sv + jnp.where(grp == g, _dot(sw_ref[g], vb), 0.0)
        ob_ref[rows, :] = (u[rows, :] * sv).astype(ob_ref.dtype)


def _mixers_ab(h, w_ab, conv_w, ng, nb, sg_wm, sg_bias):
    B, S, D = h.shape
    tm = SEQ_TILE
    W = BRANCH_W
    blk = lambda b, j: (b, j, 0)
    return pl.pallas_call(
        _ab_kernel,
        grid=(B, S // tm),
        in_specs=[pl.BlockSpec((None, tm, D), blk),
                  _full((D, 5 * W)), _full((CONV_W, W)), _full((1, W)), _full((1, W)),
                  _full((SG_GROUPS, SG_BLOCK, SG_BLOCK)), _full((SG_BLOCK, W))],
        out_specs=[pl.BlockSpec((None, tm, W), blk), pl.BlockSpec((None, tm, W), blk)],
        out_shape=[jax.ShapeDtypeStruct((B, S, W), BF16)] * 2,
        scratch_shapes=[pltpu.VMEM((V7X_SUBLANES, W), F32)],
        compiler_params=_cparams("parallel", "arbitrary"),
    )(h, w_ab, conv_w, ng, nb, sg_wm, sg_bias)


def _softplus(x):
    return jnp.maximum(x, 0.0) + jnp.log(1.0 + jnp.exp(-jnp.abs(x)))


def _rwkv_kernel(h_ref, w_ref, mu_ref, w0_ref, a0_ref, lora_ref, gup_ref, kk_ref, ka_ref,
                 rk_ref, lng_ref, lnb_ref, ones_ref, o_ref,
                 tail_ref, state_ref, v_s, y_s, gend_s, ar_s, bk_s, inv_s, rb_s, mk_s):
    @pl.when(pl.program_id(1) == 0)
    def _():
        tail_ref[...] = jnp.zeros_like(tail_ref)
        state_ref[...] = jnp.zeros_like(state_ref)

    tm = h_ref.shape[0]
    W = BRANCH_W
    N = RW_HEAD_DIM
    L = RW_CHUNK
    z = _dot(h_ref[...].astype(BF16), w_ref[...])
    zprev = _shift_rows(z, 1, tail_ref[...])
    tail_ref[...] = z[tm - V7X_SUBLANES:, :]
    z = z + (zprev - z) * mu_ref[...]
    r = z[:, 0:W]
    k = z[:, W:2 * W]
    v = z[:, 2 * W:3 * W]
    xwa = z[:, 3 * W:3 * W + 2 * RW_LORA]
    xg = z[:, 3 * W + 2 * RW_LORA:]
    lane = lax.broadcasted_iota(jnp.int32, xwa.shape, 1)
    lora_in = jnp.where(lane < RW_LORA, jnp.tanh(xwa), xwa)
    lora = _dot_split(lora_in, lora_ref[...])
    w = -_softplus(-(w0_ref[...] + lora[:, 0:W])) - 0.5
    log_decay = -jnp.exp(w)
    a = jax.nn.sigmoid(a0_ref[...] + lora[:, W:2 * W])
    g = _dot(jax.nn.sigmoid(xg).astype(BF16), gup_ref[...])
    ones = ones_ref[...]
    kk = k * kk_ref[...]
    kk = kk / jnp.maximum(jnp.sqrt(_dot_select(kk * kk, ones)), 1e-12)
    k = k * (1.0 + (a - 1.0) * ka_ref[...])
    bonus = _dot_select(r * k * rk_ref[...], ones) * v

    ti = lax.broadcasted_iota(jnp.int32, (L, L), 0)
    tj = lax.broadcasted_iota(jnp.int32, (L, L), 1)
    tri3 = jnp.concatenate([(ti >= tj).astype(BF16)] * 3, axis=1)
    eye = (ti == tj).astype(F32)
    t2 = lax.broadcasted_iota(jnp.int32, (2 * L, L), 0)
    s2 = lax.broadcasted_iota(jnp.int32, (2 * L, L), 1)
    mask2 = ((t2 < L) & (t2 > s2)) | (t2 - L >= s2)
    n_chunks = tm // L

    alpha = -kk
    beta = kk * a
    v_s[...] = v.astype(BF16)
    for c in range(n_chunks):
        rows = slice(c * L, (c + 1) * L)
        ld = log_decay[rows, :]
        cum = _dot(tri3, jnp.concatenate(_bf16_terms(ld, 3), axis=0))
        gam = jnp.exp(cum)
        gam_inv = jnp.exp(-cum)
        gend_s[c * V7X_SUBLANES:(c + 1) * V7X_SUBLANES, :] = jnp.broadcast_to(
            gam[L - 1:L, :], (V7X_SUBLANES, W))
        ar_all = jnp.concatenate([alpha[rows, :] * jnp.exp(cum - ld), r[rows, :] * gam],
                                 axis=0).astype(BF16)
        bk_all = jnp.concatenate([beta[rows, :] * gam_inv, k[rows, :] * gam_inv],
                                 axis=0).astype(BF16)
        ar_s[c] = ar_all
        bk_s[c] = bk_all

    heads = [(c, hd) for c in range(n_chunks) for hd in range(RW_HEADS)]
    hcols = lambda hd: slice(hd * N, (hd + 1) * N)
    invs, ps = [], []
    for c, hd in heads:
        pair = _dot_nt(ar_s[c, :, hcols(hd)], bk_s[c, :, hcols(hd)])
        m_b = jnp.where(mask2, pair[:, :L], 0.0)
        mk_s[c * RW_HEADS + hd] = jnp.where(mask2, pair[:, L:], 0.0).astype(BF16)
        rb_s[c * RW_HEADS + hd] = m_b[L:].astype(BF16)
        invs.append(eye + m_b[:L])
        ps.append(m_b[:L].astype(BF16))
    ps = [_dot(pb, pb) for pb in ps]
    for _ in range(int(math.log2(L)) - 2):
        both = [_dot(jnp.concatenate([p.astype(BF16), inv.astype(BF16)], axis=0), p.astype(BF16))
                for p, inv in zip(ps, invs)]
        ps = [b2[:L] for b2 in both]
        invs = [inv + b2[L:] for inv, b2 in zip(invs, both)]
    for i, (inv, p) in enumerate(zip(invs, ps)):
        inv_s[i] = (inv + _dot(inv.astype(BF16), p.astype(BF16))).astype(BF16)

    def chunk(c, carry):
        rows = pl.ds(pl.multiple_of(c * L, L), L)
        ar_all = ar_s[c]
        bk_all = bk_s[c]
        v_all = v_s[rows, :]
        gend = gend_s[pl.ds(pl.multiple_of(c * V7X_SUBLANES, V7X_SUBLANES), 1), :]
        hs = range(RW_HEADS)
        sts = [state_ref[hd] for hd in hs]
        bases = [_dot_nt(ar_all[:, hcols(hd)], sts[hd].astype(BF16))
                 + _dot(mk_s[c * RW_HEADS + hd], v_all[:, hcols(hd)]) for hd in hs]
        us = [_dot(inv_s[c * RW_HEADS + hd], bases[hd][:L].astype(BF16)).astype(BF16) for hd in hs]
        for hd in hs:
            y_s[rows, hcols(hd)] = bases[hd][L:] + _dot(rb_s[c * RW_HEADS + hd], us[hd])
        for hd in hs:
            st = sts[hd] + _dot_tn(jnp.concatenate([us[hd], v_all[:, hcols(hd)]], axis=0),
                                   bk_all[:, hcols(hd)])
            state_ref[hd] = st * gend[:, hcols(hd)]
        return carry

    lax.fori_loop(0, n_chunks, chunk, 0)

    y = y_s[...]
    inv_n = 1.0 / N
    m = _dot_select(y, ones) * inv_n
    yc = y - m
    var = _dot_select(yc * yc, ones) * inv_n
    yn = yc * lax.rsqrt(var + RW_LN_EPS) * lng_ref[...] + lnb_ref[...]
    o_ref[...] = ((yn + bonus) * g).astype(o_ref.dtype)


def _mixer_rwkv(h, w_c, mu, w0, a0, lora_w, g_up, k_k, k_a, r_k, ln_g, ln_b, head_ones):
    B, S, D = h.shape
    tm = SEQ_TILE
    W = BRANCH_W
    L = RW_CHUNK
    nc = tm // L
    blk = lambda b, j: (b, j, 0)
    row = lambda n: _full((1, n))
    return pl.pallas_call(
        _rwkv_kernel,
        grid=(B, S // tm),
        in_specs=[pl.BlockSpec((None, tm, D), blk), _full((D, RW_IN)), row(RW_IN), row(W), row(W),
                  _full((2 * RW_LORA, 2 * W)), _full((RW_GATE_LORA, W)), row(W), row(W), row(W),
                  row(W), row(W), _full(head_ones.shape)],
        out_specs=pl.BlockSpec((None, tm, W), blk),
        out_shape=jax.ShapeDtypeStruct((B, S, W), BF16),
        scratch_shapes=[pltpu.VMEM((V7X_SUBLANES, RW_IN), F32),
                        pltpu.VMEM((RW_HEADS, RW_HEAD_DIM, RW_HEAD_DIM), F32),
                        pltpu.VMEM((tm, W), BF16),
                        pltpu.VMEM((tm, W), F32),
                        pltpu.VMEM((nc * V7X_SUBLANES, W), F32),
                        pltpu.VMEM((nc, 2 * L, W), BF16),
                        pltpu.VMEM((nc, 2 * L, W), BF16),
                        pltpu.VMEM((nc * RW_HEADS, L, L), BF16),
                        pltpu.VMEM((nc * RW_HEADS, L, L), BF16),
                        pltpu.VMEM((nc * RW_HEADS, 2 * L, L), BF16)],
        compiler_params=_cparams("parallel", "arbitrary"),
    )(h, w_c, mu, w0, a0, lora_w, g_up, k_k, k_a, r_k, ln_g, ln_b, head_ones)


def _s5_kernel(h_ref, w_ref, bre_ref, bim_ref, are_ref, aim_ref, cre_ref, cim_ref,
               d_ref, gw_ref, gb_ref, o_ref, xr_s, xi_s, st_s):
    @pl.when(pl.program_id(1) == 0)
    def _():
        st_s[...] = jnp.zeros_like(st_s)

    tm = h_ref.shape[0]
    u = _dot(h_ref[...].astype(BF16), w_ref[...])
    ub = u.astype(BF16)
    xr_s[...] = _dot(ub, bre_ref[...])
    xi_s[...] = _dot(ub, bim_ref[...])
    ar = are_ref[...]
    ai = aim_ref[...]

    def step(t, carry):
        xr, xi = carry
        row = pl.ds(t, 1)
        nr = ar * xr - ai * xi + xr_s[row, :]
        ni = ar * xi + ai * xr + xi_s[row, :]
        xr_s[row, :] = nr
        xi_s[row, :] = ni
        return nr, ni

    xr, xi = lax.fori_loop(0, tm, step, (st_s[0:1, :], st_s[1:2, :]), unroll=8)
    st_s[0:1, :] = xr
    st_s[1:2, :] = xi
    y = (_dot(xr_s[...].astype(BF16), cre_ref[...]) - _dot(xi_s[...].astype(BF16), cim_ref[...])
         + d_ref[...] * u)
    y = jax.nn.gelu(y)
    gate = jax.nn.sigmoid(_dot(y.astype(BF16), gw_ref[...]) + gb_ref[...])
    o_ref[...] = (y * gate).astype(o_ref.dtype)


def _mixer_s5(h, w_d, bre, bim, are, aim, cre, cim, d, glu_w, glu_b):
    B, S, D = h.shape
    tm = SEQ_TILE
    W = BRANCH_W
    blk = lambda b, j: (b, j, 0)
    return pl.pallas_call(
        _s5_kernel,
        grid=(B, S // tm),
        in_specs=[pl.BlockSpec((None, tm, D), blk), _full((D, W)),
                  _full((W, S5_N)), _full((W, S5_N)), _full(are.shape), _full(aim.shape),
                  _full((S5_N, W)), _full((S5_N, W)), _full((1, W)), _full((W, W)), _full((1, W))],
        out_specs=pl.BlockSpec((None, tm, W), blk),
        out_shape=jax.ShapeDtypeStruct((B, S, W), BF16),
        scratch_shapes=[pltpu.VMEM((tm, S5_N), F32), pltpu.VMEM((tm, S5_N), F32),
                        pltpu.VMEM((V7X_SUBLANES, S5_N), F32)],
        compiler_params=_cparams("parallel", "arbitrary"),
    )(h, w_d, bre, bim, are, aim, cre, cim, d, glu_w, glu_b)


def _merge_kernel(h_ref, oa_ref, ob_ref, oc_ref, od_ref, wg_ref, gb_ref, br_ref, wo_ref,
                  g_ref, b_ref, o_ref):
    D = D_MODEL
    h = h_ref[...]
    hb = h.astype(BF16)
    merged = None
    for i, br_in in enumerate((oa_ref, ob_ref, oc_ref, od_ref)):
        gate = jax.nn.sigmoid(_dot(hb, wg_ref[:, i * D:(i + 1) * D]) + gb_ref[i:i + 1, :])
        term = gate * _dot(br_in[...], br_ref[i])
        merged = term if merged is None else merged + term
    y = _dot(merged.astype(BF16), wo_ref[...])
    o_ref[...] = _layer_norm(DN_ALPHA * h + y, g_ref[...], b_ref[...])


def _merge(h2, oa, ob, oc, od, w_g, gate_b, br_proj, w_out, g, b):
    T, D = h2.shape
    tm = TOKEN_TILE
    W = BRANCH_W
    tok = lambda n: pl.BlockSpec((tm, n), lambda i: (i, 0))
    return pl.pallas_call(
        _merge_kernel,
        grid=(T // tm,),
        in_specs=[tok(D), tok(W), tok(W), tok(W), tok(W),
                  _full((D, N_BRANCH * D)), _full((N_BRANCH, D)), _full((N_BRANCH, W, D)),
                  _full((D, D)), _full((1, D)), _full((1, D))],
        out_specs=tok(D),
        out_shape=jax.ShapeDtypeStruct((T, D), F32),
        compiler_params=_cparams("parallel"),
    )(h2, oa, ob, oc, od, w_g, gate_b, br_proj, w_out, g, b)


def _kv_kernel(m_ref, wk_ref, wv_ref, k_ref, v_ref):
    mb = m_ref[...].astype(BF16)
    k_ref[...] = _dot(mb, wk_ref[...]).astype(k_ref.dtype)
    v_ref[...] = _dot(mb, wv_ref[...]).astype(v_ref.dtype)


def _kv_proj(mem2, wk, wv):
    R, D = mem2.shape
    tm = TOKEN_TILE
    tok = pl.BlockSpec((tm, D), lambda i: (i, 0))
    return pl.pallas_call(
        _kv_kernel,
        grid=(R // tm,),
        in_specs=[tok, _full((D, D)), _full((D, D))],
        out_specs=[tok, tok],
        out_shape=[jax.ShapeDtypeStruct((R, D), BF16)] * 2,
        compiler_params=_cparams("parallel"),
    )(mem2, wk, wv)


def _attn_kernel(h_ref, wq_ref, k_ref, v_ref, wo_ref, g_ref, b_ref, o_ref, ot_ref):
    h = h_ref[...]
    q = _dot(h.astype(BF16), wq_ref[...]).astype(BF16)
    scale = XA_HEAD_DIM ** -0.5
    outs = []
    for hd in range(XA_HEADS):
        cols = slice(hd * XA_HEAD_DIM, (hd + 1) * XA_HEAD_DIM)
        s = _dot_nt(q[:, cols], k_ref[:, cols]) * scale
        s = s - jnp.max(s, axis=-1, keepdims=True)
        p = jnp.exp(s)
        p = p / jnp.sum(p, axis=-1, keepdims=True)
        outs.append(_dot(p.astype(BF16), v_ref[:, cols]))
    o = jnp.concatenate(outs, axis=-1).astype(BF16)
    y = _dot(o, wo_ref[...])
    out = _layer_norm(DN_ALPHA * h + y, g_ref[...], b_ref[...])
    o_ref[...] = out
    _store_row_tiles(ot_ref, out)


def _cross_attention(h, kmem, vmem, wq, wo, g, b):
    B, S, D = h.shape
    M = kmem.shape[1]
    tm = TOKEN_TILE
    n = S // tm
    blk = lambda bi, j: (bi, j, 0)
    mem = lambda bi, j: (bi, 0, 0)
    return pl.pallas_call(
        _attn_kernel,
        grid=(B, n),
        in_specs=[pl.BlockSpec((None, tm, D), blk), _full((D, D)),
                  pl.BlockSpec((None, M, D), mem), pl.BlockSpec((None, M, D), mem),
                  _full((D, D)), _full((1, D)), _full((1, D))],
        out_specs=[pl.BlockSpec((None, tm, D), blk),
                   pl.BlockSpec((tm * ROW_TILE, V7X_LANES), lambda bi, j: (bi * n + j, 0))],
        out_shape=[jax.ShapeDtypeStruct((B, S, D), F32),
                   jax.ShapeDtypeStruct((B * S * ROW_TILE, V7X_LANES), F32)],
        compiler_params=_cparams("parallel", "parallel"),
    )(h, wq, kmem, vmem, wo, g, b)


def _router_kernel(h_ref, w_ref, b_ref, route_ref, rw_ref, cnt_ref, carry_ref):
    @pl.when(pl.program_id(0) == 0)
    def _():
        carry_ref[...] = jnp.zeros_like(carry_ref)

    tm = h_ref.shape[0]
    E = N_EXPERTS
    logits = _dot_split(h_ref[...], w_ref[...]) + b_ref[...]
    lane = lax.broadcasted_iota(jnp.int32, (tm, E), 1)
    work = logits
    vals, hots, idxs = [], [], []
    for _ in range(TOP_K):
        m = jnp.max(work, axis=-1, keepdims=True)
        idx = jnp.min(jnp.where(work == m, lane, E), axis=-1, keepdims=True)
        hot = lane == idx
        work = jnp.where(hot, -jnp.inf, work)
        vals.append(m)
        hots.append(hot)
        idxs.append(idx)
    exps = [jnp.exp(v - vals[0]) for v in vals]
    denom = exps[0] + exps[1] + exps[2] + exps[3]
    multi = (hots[0] | hots[1] | hots[2] | hots[3]).astype(F32)
    ti = lax.broadcasted_iota(jnp.int32, (tm, tm), 0)
    tj = lax.broadcasted_iota(jnp.int32, (tm, tm), 1)
    before = (ti > tj).astype(BF16)
    prefix = _dot(before, multi.astype(BF16)) + carry_ref[0:1, :]
    out_lane = lax.broadcasted_iota(jnp.int32, (tm, ROUTE_LANES), 1)
    route = jnp.zeros((tm, ROUTE_LANES), jnp.int32)
    rw = jnp.zeros((tm, ROUTE_LANES), F32)
    for kk in range(TOP_K):
        rank = jnp.sum(jnp.where(hots[kk], prefix, 0.0), axis=-1, keepdims=True).astype(jnp.int32)
        route = jnp.where(out_lane == kk, idxs[kk], route)
        route = jnp.where(out_lane == TOP_K + kk, rank, route)
        rw = jnp.where(out_lane == kk, exps[kk] / denom, rw)
    route_ref[...] = route
    rw_ref[...] = rw
    carry_ref[0:1, :] = carry_ref[0:1, :] + jnp.sum(multi, axis=0, keepdims=True)
    cnt_ref[...] = carry_ref[...]


def _router(h2, router_w, router_b):
    T, D = h2.shape
    tm = TOKEN_TILE
    E = N_EXPERTS
    return pl.pallas_call(
        _router_kernel,
        grid=(T // tm,),
        in_specs=[pl.BlockSpec((tm, D), lambda i: (i, 0)), _full((D, E)), _full((1, E))],
        out_specs=[pl.BlockSpec((tm, ROUTE_LANES), lambda i: (i, 0)),
                   pl.BlockSpec((tm, ROUTE_LANES), lambda i: (i, 0)),
                   _full((V7X_SUBLANES, E))],
        out_shape=[jax.ShapeDtypeStruct((T, ROUTE_LANES), jnp.int32),
                   jax.ShapeDtypeStruct((T, ROUTE_LANES), F32),
                   jax.ShapeDtypeStruct((V7X_SUBLANES, E), F32)],
        scratch_shapes=[pltpu.VMEM((V7X_SUBLANES, E), F32)],
        compiler_params=_cparams("arbitrary"),
    )(h2, router_w, router_b)


def _dest_kernel(route_ref, start_ref, dest_ref):
    route = route_ref[...]
    tm = route.shape[0]
    lane = lax.broadcasted_iota(jnp.int32, (tm, N_EXPERTS), 1)
    out_lane = lax.broadcasted_iota(jnp.int32, (tm, ROUTE_LANES), 1)
    dest = jnp.zeros((tm, ROUTE_LANES), jnp.int32)
    start = start_ref[...]
    for kk in range(TOP_K):
        idx = route[:, kk:kk + 1]
        base = jnp.sum(jnp.where(lane == idx, start, 0), axis=-1, keepdims=True)
        dest = jnp.where(out_lane == kk, base + route[:, TOP_K + kk:TOP_K + kk + 1], dest)
    dest_ref[...] = dest


def _dest_slots(route, pad_start):
    T = route.shape[0]
    tm = TOKEN_TILE * 2
    return pl.pallas_call(
        _dest_kernel,
        grid=(T // tm,),
        in_specs=[pl.BlockSpec((tm, ROUTE_LANES), lambda i: (i, 0)), _full((1, N_EXPERTS))],
        out_specs=pl.BlockSpec((tm, ROUTE_LANES), lambda i: (i, 0)),
        out_shape=jax.ShapeDtypeStruct((T, ROUTE_LANES), jnp.int32),
        compiler_params=_cparams("parallel"),
    )(route, pad_start.reshape(1, N_EXPERTS))


def _slot_source_kernel(fill_lo_ref, fill_hi_ref, dest_hbm, row_ref, dest_s, sem, *, n_tok):
    i = pl.program_id(0)
    n = dest_s.shape[0]

    @pl.when(i == 0)
    def _():
        def fill(s, carry):
            parity = (s // MOE_TILE + 1) % 2
            row_ref[s] = n_tok * TOP_K + parity * MOE_TILE + s % MOE_TILE
            return carry

        lax.fori_loop(0, MOE_TILE, fill, 0)
        for e in range(fill_lo_ref.shape[0]):
            lax.fori_loop(MOE_TILE + fill_lo_ref[e], MOE_TILE + fill_hi_ref[e], fill, 0)

    idx_copy = pltpu.make_async_copy(dest_hbm.at[i], dest_s, sem)
    idx_copy.start()
    idx_copy.wait()
    slots = row_ref.at[pl.ds(MOE_TILE, row_ref.shape[0] - MOE_TILE)]
    first_tok = i * (n // TOP_K)

    def body(t, carry):
        for kk in range(TOP_K):
            slots[dest_s[t * TOP_K + kk]] = (kk * n_tok + first_tok) + t
        return carry

    lax.fori_loop(0, n // TOP_K, body, 0, unroll=8)


def _slot_sources(fill_lo, fill_hi, dest_tiles, n_slots):
    n_tok_tiles, n = dest_tiles.shape
    return pl.pallas_call(
        functools.partial(_slot_source_kernel, n_tok=n_tok_tiles * n // TOP_K),
        grid_spec=pltpu.PrefetchScalarGridSpec(
            num_scalar_prefetch=2,
            grid=(n_tok_tiles,),
            in_specs=[pl.BlockSpec(memory_space=pl.ANY)],
            out_specs=pl.BlockSpec(memory_space=pltpu.SMEM),
            scratch_shapes=[pltpu.SMEM((n,), jnp.int32), pltpu.SemaphoreType.DMA]),
        out_shape=jax.ShapeDtypeStruct((MOE_TILE + n_slots,), jnp.int32),
        compiler_params=_cparams("arbitrary"),
    )(fill_lo, fill_hi, dest_tiles)


GU_BLOCK = 2 * V7X_LANES


def _unzip_kernel(w_ref, p_ref, o_ref):
    for c in range(w_ref.shape[1] // GU_BLOCK):
        cols = slice(c * GU_BLOCK, (c + 1) * GU_BLOCK)
        o_ref[:, cols] = _dot(w_ref[:, cols].astype(BF16), p_ref[...]).astype(o_ref.dtype)


def _unzip_gate_up(w1_layers, layer):
    _, E, D, F2 = w1_layers.shape
    tr = TOKEN_TILE
    j = jnp.arange(GU_BLOCK)
    perm = (j[:, None] == jnp.where(j < V7X_LANES, 2 * j, 2 * (j - V7X_LANES) + 1)[None, :])
    return pl.pallas_call(
        _unzip_kernel,
        grid=(E, D // tr),
        in_specs=[pl.BlockSpec((None, None, tr, F2), lambda e, r: (layer, e, r, 0)),
                  _full((GU_BLOCK, GU_BLOCK))],
        out_specs=pl.BlockSpec((None, tr, F2), lambda e, r: (e, r, 0)),
        out_shape=jax.ShapeDtypeStruct((E, D, F2), BF16),
        compiler_params=_cparams("parallel", "parallel"),
    )(w1_layers, perm.astype(BF16))


def _unzip_bias(b1):
    E, F2 = b1.shape
    return b1.reshape(E, F2 // GU_BLOCK, V7X_LANES, 2).transpose(0, 1, 3, 2).reshape(E, 1, F2)


def _expert_ffn(x, w1_ref, b1_ref, w2_ref, b2_ref):
    hid = _dot(x, w1_ref[...]) + b1_ref[...]
    acts = []
    for c in range(hid.shape[1] // GU_BLOCK):
        gate = jnp.minimum(hid[:, c * GU_BLOCK:c * GU_BLOCK + V7X_LANES], SWIGLU_LIMIT)
        up = jnp.clip(hid[:, c * GU_BLOCK + V7X_LANES:(c + 1) * GU_BLOCK],
                      -SWIGLU_LIMIT, SWIGLU_LIMIT)
        acts.append(((up + 1.0) * gate * jax.nn.sigmoid(SWIGLU_ALPHA * gate)).astype(BF16))
    return _dot(jnp.concatenate(acts, axis=-1), w2_ref[...].astype(BF16)) + b2_ref[...]


def _expert_kernel(blk_e_ref, n_used_ref, src_ref, h_hbm, w1_ref, b1_ref, w2_ref, b2_ref, yk_hbm,
                   xbuf0, xbuf1, ybuf0, ybuf1, gsem, ssem):
    del blk_e_ref
    i = pl.program_id(0)
    n_used = n_used_ref[0]
    xbufs = (xbuf0, xbuf1)
    ybufs = (ybuf0, ybuf1)
    R = ROW_TILE
    tm = xbuf0.shape[0] // R
    T = h_hbm.shape[0] // R
    n_assign = T * TOP_K
    def tile_rows(r):
        return pl.ds(r * R if isinstance(r, int) else pl.multiple_of(r * R, R), R)

    def gather_row(tile, b, j, priority):
        row = src_ref[(tile + 1) * tm + j]
        tok = row & (T - 1) if T & (T - 1) == 0 else lax.rem(row, T)
        pltpu.make_async_copy(h_hbm.at[tile_rows(tok)], xbufs[b].at[tile_rows(j)],
                              gsem.at[b]).start(priority=priority)

    def scatter_row(tile, b, j, priority):
        row = src_ref[(tile + 1) * tm + j]
        pltpu.make_async_copy(ybufs[b].at[tile_rows(j)], yk_hbm.at[tile_rows(row)],
                              ssem.at[b]).start(priority=priority)

    def wait_gather(b):
        pltpu.make_async_copy(h_hbm.at[pl.ds(0, tm * R)], xbufs[b], gsem.at[b]).wait()

    def wait_scatter(b):
        pltpu.make_async_copy(ybufs[b], yk_hbm.at[pl.ds(0, tm * R)], ssem.at[b]).wait()

    def rows_loop(fn):
        def body(j, carry):
            fn(j)
            return carry
        lax.fori_loop(0, tm, body, 0)

    @pl.when(i == 0)
    def _():
        rows_loop(lambda j: gather_row(0, 0, j, 0))
        ybuf0[...] = jnp.zeros_like(ybuf0)
        ybuf1[...] = jnp.zeros_like(ybuf1)
        pltpu.make_async_copy(ybuf0, yk_hbm.at[pl.ds(n_assign * R, tm * R)], ssem.at[0]).start()

    for b in range(2):
        @pl.when((i < n_used) & (i % 2 == b))
        def _(b=b):
            wait_gather(b)
            wait_scatter(b)
            for j in range(tm):
                gather_row(i + 1, 1 - b, j, 0)
                scatter_row(i - 1, 1 - b, j, 1)
            x = _load_row_tiles(xbufs[b]).astype(BF16)
            _store_row_tiles(ybufs[b], _expert_ffn(x, w1_ref, b1_ref, w2_ref, b2_ref))

        @pl.when((i == n_used) & (i % 2 == b))
        def _(b=b):
            wait_gather(b)
            wait_scatter(b)
            rows_loop(lambda j: scatter_row(i - 1, 1 - b, j, 0))
            wait_scatter(1 - b)


def _experts(blk_e, n_used, src, h_tiles, w1, b1, w2_layers, layer, b2):
    R = ROW_TILE
    T = h_tiles.shape[0] // R
    D = D_MODEL
    tm = MOE_TILE
    F = D_FF
    n_tiles = src.shape[0] // tm - 1
    per_e = lambda shape: pl.BlockSpec((None,) + shape, lambda i, be, nu, sr: (be[i], 0, 0))
    w2_spec = pl.BlockSpec((None, None, F, D), lambda i, be, nu, sr: (layer, be[i], 0, 0))
    return pl.pallas_call(
        _expert_kernel,
        grid_spec=pltpu.PrefetchScalarGridSpec(
            num_scalar_prefetch=3,
            grid=(n_tiles,),
            in_specs=[pl.BlockSpec(memory_space=pl.ANY),
                      per_e((D, 2 * F)), per_e((1, 2 * F)), w2_spec, per_e((1, D))],
            out_specs=pl.BlockSpec(memory_space=pl.ANY),
            scratch_shapes=[pltpu.VMEM((tm * R, V7X_LANES), F32)] * 4
                           + [pltpu.SemaphoreType.DMA((2,)), pltpu.SemaphoreType.DMA((2,))]),
        out_shape=jax.ShapeDtypeStruct(((T * TOP_K + 2 * tm) * R, V7X_LANES), F32),
        compiler_params=_cparams("arbitrary"),
    )(blk_e, n_used, src, h_tiles, w1, b1, w2_layers, b2)


def _combine_kernel(y0_ref, y1_ref, y2_ref, y3_ref, h_ref, rw_ref, g_ref, b_ref, o_ref):
    rw = rw_ref[...]
    moe = rw[:, 0:1] * _load_row_tiles(y0_ref)
    for kk, y_ref in enumerate((y1_ref, y2_ref, y3_ref), start=1):
        moe = moe + rw[:, kk:kk + 1] * _load_row_tiles(y_ref)
    o_ref[...] = _layer_norm(DN_ALPHA * h_ref[...] + moe, g_ref[...], b_ref[...])


def _combine(yk, h2, rw, g, b):
    T, D = h2.shape
    tm = TOKEN_TILE
    n = T // tm
    choice = lambda kk: pl.BlockSpec((tm * ROW_TILE, V7X_LANES), lambda i: (kk * n + i, 0))
    return pl.pallas_call(
        _combine_kernel,
        grid=(n,),
        in_specs=[choice(kk) for kk in range(TOP_K)]
                 + [pl.BlockSpec((tm, D), lambda i: (i, 0)),
                    pl.BlockSpec((tm, ROUTE_LANES), lambda i: (i, 0)),
                    _full((1, D)), _full((1, D))],
        out_specs=pl.BlockSpec((tm, D), lambda i: (i, 0)),
        out_shape=jax.ShapeDtypeStruct((T, D), F32),
        compiler_params=_cparams("parallel"),
    )(yk, yk, yk, yk, h2, rw, g, b)


def _s5_discretise(a_re, a_im, b_re, b_im, c_re, c_im, log_dt):
    G, P, C = S5_GROUPS, S5_STATE, S5_CH
    dt = jnp.exp(log_dt)[:, None]
    mag = jnp.exp(a_re * dt)
    abar_re = mag * jnp.cos(a_im * dt)
    abar_im = mag * jnp.sin(a_im * dt)
    den = a_re * a_re + a_im * a_im
    num_re = abar_re - 1.0
    coef_re = (num_re * a_re + abar_im * a_im) / den
    coef_im = (abar_im * a_re - num_re * a_im) / den
    bbar_re = coef_re[..., None] * b_re - coef_im[..., None] * b_im
    bbar_im = coef_re[..., None] * b_im + coef_im[..., None] * b_re
    eye = jnp.eye(G, dtype=F32)
    to_in = lambda m: jnp.einsum('gpc,gh->gchp', m, eye).reshape(G * C, G * P)
    to_out = lambda m: jnp.einsum('gcp,gh->gphc', m, eye).reshape(G * P, G * C)
    return (to_in(bbar_re).astype(BF16), to_in(bbar_im).astype(BF16),
            abar_re.reshape(1, G * P), abar_im.reshape(1, G * P),
            to_out(c_re).astype(BF16), to_out(c_im).astype(BF16))


def _moe_layer(h2, h_tiles, layer, router_w, router_b, w1_layers, b1, w2_layers, b2, g, b):
    T, D = h2.shape
    E = N_EXPERTS
    tm = TOKEN_TILE
    route, rw, counts = _router(h2, router_w, router_b.reshape(1, E))
    counts = counts[0].astype(jnp.int32)
    padded = (counts + MOE_TILE - 1) // MOE_TILE * MOE_TILE
    pad_end = jnp.cumsum(padded)
    pad_start = pad_end - padded
    n_tiles = T * TOP_K // MOE_TILE + E
    n_slots = n_tiles * MOE_TILE
    tile_start = jnp.arange(n_tiles, dtype=jnp.int32) * MOE_TILE
    blk_e = jnp.minimum(jnp.sum(tile_start[:, None] >= pad_end[None, :], axis=1), E - 1).astype(jnp.int32)
    n_used = (pad_end[E - 1:] // MOE_TILE).astype(jnp.int32)
    dest = _dest_slots(route, pad_start.astype(jnp.int32))
    dest_tiles = dest[:, :TOP_K].reshape(T // tm, tm * TOP_K)
    fill_lo = jnp.concatenate([pad_start + counts, pad_end[E - 1:]]).astype(jnp.int32)
    fill_hi = jnp.concatenate([pad_end, jnp.full((1,), n_slots)]).astype(jnp.int32)
    src = _slot_sources(fill_lo, fill_hi, dest_tiles, n_slots)
    yk = _experts(blk_e, n_used, src, h_tiles, _unzip_gate_up(w1_layers, layer), _unzip_bias(b1),
                  w2_layers, layer, b2[:, None, :])
    return _combine(yk, h2, rw, g.reshape(1, D), b.reshape(1, D))


def kernel(x, mem, ln_in_g, ln_in_b, w_in, conv_w, sg_norm_g, sg_norm_b, sg_w, sg_b, rw_mu, rw_w0, rw_w_up, rw_a0, rw_a_up, rw_g_up, rw_k_k, rw_k_a, rw_r_k, rw_ln_g, rw_ln_b, s5_a_re, s5_a_im, s5_b_re, s5_b_im, s5_c_re, s5_c_im, s5_d, s5_log_dt, s5_glu_w, s5_glu_b, br_proj, gate_b, w_out, ln1_g, ln1_b, xa_wq, xa_wk, xa_wv, xa_wo, ln2_g, ln2_b, router_w, router_b, ex_w1, ex_b1, ex_w2, ex_b2, ln3_g, ln3_b):
    B, S, D = x.shape
    M = mem.shape[1]
    T = B * S
    W = BRANCH_W
    row = lambda v: v.reshape(1, -1)
    pos = jnp.arange(SG_BLOCK)
    sg_mask = (pos[None, :] // CHUNK) <= (pos[:, None] // CHUNK)
    head_ones = jnp.kron(jnp.eye(RW_HEADS, dtype=F32), jnp.ones((RW_HEAD_DIM, RW_HEAD_DIM), F32))
    head_ones = jnp.concatenate([head_ones, head_ones], axis=0).astype(BF16)
    mem2 = mem.reshape(B * M, D)

    h = _input_ln(x.reshape(T, D), ln_in_g, ln_in_b)
    for l in range(DEPTH):
        w_l = w_in[l].astype(BF16)
        h3 = h.reshape(B, S, D)
        sg_wm = jnp.where(sg_mask[None], sg_w[l], 0.0).astype(BF16)
        sg_bias = jnp.repeat(sg_b[l].T, W // SG_GROUPS, axis=1)
        o_a, o_b = _mixers_ab(h3, w_l[:, :OFF_C], conv_w[l], row(sg_norm_g[l]), row(sg_norm_b[l]),
                              sg_wm, sg_bias)
        zero = jnp.zeros((RW_LORA, W), F32)
        lora_w = jnp.concatenate([jnp.concatenate([rw_w_up[l], zero], axis=1),
                                  jnp.concatenate([zero, rw_a_up[l]], axis=1)], axis=0)
        o_c = _mixer_rwkv(h3, w_l[:, OFF_C:OFF_D], row(rw_mu[l]), row(rw_w0[l]), row(rw_a0[l]),
                          lora_w, rw_g_up[l].astype(BF16), row(rw_k_k[l]), row(rw_k_a[l]),
                          row(rw_r_k[l]),
                          row(rw_ln_g[l]), row(rw_ln_b[l]), head_ones)
        s5p = _s5_discretise(s5_a_re[l], s5_a_im[l], s5_b_re[l], s5_b_im[l], s5_c_re[l],
                             s5_c_im[l], s5_log_dt[l])
        o_d = _mixer_s5(h3, w_l[:, OFF_D:OFF_G], *s5p, row(s5_d[l]),
                        s5_glu_w[l].astype(BF16), row(s5_glu_b[l]))
        flat = lambda o: o.reshape(T, W)
        h = _merge(h, flat(o_a), flat(o_b), flat(o_c), flat(o_d), w_l[:, OFF_G:], gate_b[l],
                   br_proj[l].astype(BF16), w_out[l].astype(BF16), row(ln1_g[l]), row(ln1_b[l]))
        kmem, vmem = _kv_proj(mem2, xa_wk[l].astype(BF16), xa_wv[l].astype(BF16))
        h, h_tiles = _cross_attention(h.reshape(B, S, D), kmem.reshape(B, M, D),
                                      vmem.reshape(B, M, D), xa_wq[l].astype(BF16),
                                      xa_wo[l].astype(BF16), row(ln2_g[l]), row(ln2_b[l]))
        h = _moe_layer(h.reshape(T, D), h_tiles, l, router_w[l], router_b[l], ex_w1, ex_b1[l],
                       ex_w2, ex_b2[l], ln3_g[l], ln3_b[l])
    return h.reshape(B, S, D)
```

```python
import functools
import math

import jax
import jax.numpy as jnp
from jax import lax
from jax.experimental import pallas as pl
from jax.experimental.pallas import tpu as pltpu

F32 = jnp.float32
BF16 = jnp.bfloat16

D_MODEL = 1024
DEPTH = 2
CHUNK = 64
BRANCH_W = 256
N_BRANCH = 4
CONV_W = 3
SG_BLOCK = 128
SG_GROUPS = 4
RW_HEADS = 4
RW_HEAD_DIM = 64
RW_LORA = 64
RW_GATE_LORA = 128
RW_IN = 3 * BRANCH_W + 2 * RW_LORA + RW_GATE_LORA
RW_LN_EPS = 64e-5
S5_CH = 16
S5_GROUPS = BRANCH_W // S5_CH
S5_STATE = 64
S5_N = S5_GROUPS * S5_STATE
OFF_B = 3 * BRANCH_W
OFF_C = OFF_B + 2 * BRANCH_W
OFF_D = OFF_C + RW_IN
OFF_G = OFF_D + BRANCH_W
XA_HEADS = 4
XA_HEAD_DIM = D_MODEL // XA_HEADS
N_EXPERTS = 32
TOP_K = 4
TOP_K_BITS = 2
D_FF = D_MODEL
SWIGLU_LIMIT = 7.0
SWIGLU_ALPHA = 1.702
LN_EPS = 1e-5
DN_ALPHA = (2 * DEPTH) ** 0.25

V7X_LANES = 128
V7X_SUBLANES = 8
V7X_VMEM_LIMIT_BYTES = 56 * 1024 * 1024
V7X_DMA_PRIORITIES = 2
DMA_PRIORITIES = V7X_DMA_PRIORITIES
TOKEN_TILE = 512
SEQ_TILE = 256
SLOT_MAP_CHUNK = 8192
RW_CHUNK = 64
MOE_TILE = 512
ROUTE_LANES = 128


def _cparams(*sem):
    return pltpu.CompilerParams(dimension_semantics=sem,
                                vmem_limit_bytes=V7X_VMEM_LIMIT_BYTES)


def _full(shape):
    nd = len(shape)
    return pl.BlockSpec(shape, lambda *_: (0,) * nd)


def _layer_norm(x, g, b, eps=LN_EPS):
    mu = jnp.mean(x, axis=-1, keepdims=True)
    xc = x - mu
    var = jnp.mean(xc * xc, axis=-1, keepdims=True)
    return xc * lax.rsqrt(var + eps) * g + b


def _contract(a, b, dims, **kw):
    return lax.dot_general(a, b, (dims, ((), ())), preferred_element_type=F32, **kw)


def _dot(a, b, **kw):
    return _contract(a, b, ((1,), (0,)), **kw)


def _bf16_terms(x, n):
    terms = []
    for _ in range(n):
        t = x.astype(BF16)
        terms.append(t)
        x = x - t.astype(F32)
    return terms


def _dot_split(a, b):
    ah, al = _bf16_terms(a, 2)
    bh, bl = _bf16_terms(b, 2)
    return _dot(jnp.concatenate([ah, ah, al], axis=1), jnp.concatenate([bh, bl, bh], axis=0))


def _dot_select(x, sel_stack):
    n = sel_stack.shape[0] // x.shape[1]
    return _dot(jnp.concatenate(_bf16_terms(x, n), axis=1), sel_stack)


def _dot_nt(a, b, **kw):
    return _contract(a, b, ((1,), (1,)), **kw)


def _dot_tn(a, b, **kw):
    return _contract(a, b, ((0,), (0,)), **kw)


ROW_TILE = D_MODEL // V7X_LANES


def _store_row_tiles(ref, x):
    n = x.shape[0]
    for c in range(ROW_TILE):
        ref[pl.ds(c, n, stride=ROW_TILE), :] = x[:, c * V7X_LANES:(c + 1) * V7X_LANES]


def _load_row_tiles(ref):
    n = ref.shape[0] // ROW_TILE
    return jnp.concatenate([ref[pl.ds(c, n, stride=ROW_TILE), :] for c in range(ROW_TILE)], axis=-1)


def _shift_rows(x, n, tail):
    rolled = pltpu.roll(x, n, 0)
    row = lax.broadcasted_iota(jnp.int32, x.shape, 0)
    out = rolled
    for i in range(n):
        src = tail[V7X_SUBLANES - n + i:V7X_SUBLANES - n + i + 1, :]
        out = jnp.where(row == i, src, out)
    return out


def _ln_kernel(x_ref, g_ref, b_ref, o_ref):
    o_ref[...] = _layer_norm(x_ref[...], g_ref[...], b_ref[...])


def _input_ln(x2, g, b):
    T, D = x2.shape
    tm = TOKEN_TILE * 2
    return pl.pallas_call(
        _ln_kernel,
        grid=(T // tm,),
        in_specs=[pl.BlockSpec((tm, D), lambda i: (i, 0)), _full((1, D)), _full((1, D))],
        out_specs=pl.BlockSpec((tm, D), lambda i: (i, 0)),
        out_shape=jax.ShapeDtypeStruct((T, D), F32),
        compiler_params=_cparams("parallel"),
    )(x2, g.reshape(1, D), b.reshape(1, D))


def _ab_kernel(h_ref, w_ref, cw_ref, ng_ref, nb_ref, sw_ref, sb_ref,
               oa_ref, ob_ref, tail_ref):
    @pl.when(pl.program_id(1) == 0)
    def _():
        tail_ref[...] = jnp.zeros_like(tail_ref)

    tm = h_ref.shape[0]
    W = BRANCH_W
    z = _dot(h_ref[...].astype(BF16), w_ref[...])
    ch = z[:, W:2 * W] * z[:, 2 * W:3 * W]
    tail = tail_ref[...]
    s1 = _shift_rows(ch, 1, tail)
    s2 = _shift_rows(ch, 2, tail)
    tail_ref[...] = ch[tm - V7X_SUBLANES:, :]
    cw = cw_ref[...]
    y = cw[2:3, :] * ch + cw[1:2, :] * s1 + cw[0:1, :] * s2
    oa_ref[...] = (z[:, 0:W] * y).astype(oa_ref.dtype)
    u = z[:, 3 * W:4 * W]
    v = _layer_norm(z[:, 4 * W:5 * W], ng_ref[...], nb_ref[...]).astype(BF16)
    gw = W // SG_GROUPS
    grp = lax.broadcasted_iota(jnp.int32, (SG_BLOCK, W), 1) // gw
    for blk in range(tm // SG_BLOCK):
        rows = slice(blk * SG_BLOCK, (blk + 1) * SG_BLOCK)
        vb = v[rows, :]
        sv = sb_ref[...]
        for g in range(SG_GROUPS):
            sv = sv + jnp.where(grp == g, _dot(sw_ref[g], vb), 0.0)
        ob_ref[rows, :] = (u[rows, :] * sv).astype(ob_ref.dtype)


def _mixers_ab(h, w_ab, conv_w, ng, nb, sg_wm, sg_bias):
    B, S, D = h.shape
    tm = TOKEN_TILE
    W = BRANCH_W
    blk = lambda b, j: (b, j, 0)
    return pl.pallas_call(
        _ab_kernel,
        grid=(B, S // tm),
        in_specs=[pl.BlockSpec((None, tm, D), blk),
                  _full((D, 5 * W)), _full((CONV_W, W)), _full((1, W)), _full((1, W)),
                  _full((SG_GROUPS, SG_BLOCK, SG_BLOCK)), _full((SG_BLOCK, W))],
        out_specs=[pl.BlockSpec((None, tm, W), blk), pl.BlockSpec((None, tm, W), blk)],
        out_shape=[jax.ShapeDtypeStruct((B, S, W), BF16)] * 2,
        scratch_shapes=[pltpu.VMEM((V7X_SUBLANES, W), F32)],
        compiler_params=_cparams("parallel", "arbitrary"),
    )(h, w_ab, conv_w, ng, nb, sg_wm, sg_bias)


def _softplus(x):
    return jnp.maximum(x, 0.0) + jnp.log(1.0 + jnp.exp(-jnp.abs(x)))


def _rwkv_kernel(h_ref, w_ref, mu_ref, w0_ref, a0_ref, lora_ref, gup_ref, kk_ref, ka_ref,
                 rk_ref, lng_ref, lnb_ref, ones_ref, o_ref,
                 tail_ref, state_ref, v_s, y_s, gend_s, ar_s, bk_s, inv_s, rb_s, mk_s):
    @pl.when(pl.program_id(1) == 0)
    def _():
        tail_ref[...] = jnp.zeros_like(tail_ref)
        state_ref[...] = jnp.zeros_like(state_ref)

    tm = h_ref.shape[0]
    W = BRANCH_W
    N = RW_HEAD_DIM
    L = RW_CHUNK
    z = _dot(h_ref[...].astype(BF16), w_ref[...])
    zprev = _shift_rows(z, 1, tail_ref[...])
    tail_ref[...] = z[tm - V7X_SUBLANES:, :]
    z = z + (zprev - z) * mu_ref[...]
    r = z[:, 0:W]
    k = z[:, W:2 * W]
    v = z[:, 2 * W:3 * W]
    xwa = z[:, 3 * W:3 * W + 2 * RW_LORA]
    xg = z[:, 3 * W + 2 * RW_LORA:]
    lane = lax.broadcasted_iota(jnp.int32, xwa.shape, 1)
    lora_in = jnp.where(lane < RW_LORA, jnp.tanh(xwa), xwa)
    lora = _dot_split(lora_in, lora_ref[...])
    w = -_softplus(-(w0_ref[...] + lora[:, 0:W])) - 0.5
    log_decay = -jnp.exp(w)
    a = jax.nn.sigmoid(a0_ref[...] + lora[:, W:2 * W])
    g = _dot(jax.nn.sigmoid(xg).astype(BF16), gup_ref[...])
    ones = ones_ref[...]
    kk = k * kk_ref[...]
    kk = kk / jnp.maximum(jnp.sqrt(_dot_select(kk * kk, ones)), 1e-12)
    k = k * (1.0 + (a - 1.0) * ka_ref[...])
    bonus = _dot_select(r * k * rk_ref[...], ones) * v

    ti = lax.broadcasted_iota(jnp.int32, (L, L), 0)
    tj = lax.broadcasted_iota(jnp.int32, (L, L), 1)
    tri3 = jnp.concatenate([(ti >= tj).astype(BF16)] * 3, axis=1)
    eye = (ti == tj).astype(F32)
    t2 = lax.broadcasted_iota(jnp.int32, (2 * L, L), 0)
    s2 = lax.broadcasted_iota(jnp.int32, (2 * L, L), 1)
    mask2 = ((t2 < L) & (t2 > s2)) | (t2 - L >= s2)
    n_chunks = tm // L

    alpha = -kk
    beta = kk * a
    v_s[...] = v.astype(BF16)
    for c in range(n_chunks):
        rows = slice(c * L, (c + 1) * L)
        ld = log_decay[rows, :]
        cum = _dot(tri3, jnp.concatenate(_bf16_terms(ld, 3), axis=0))
        gam = jnp.exp(cum)
        gam_inv = jnp.exp(-cum)
        gend_s[c * V7X_SUBLANES:(c + 1) * V7X_SUBLANES, :] = jnp.broadcast_to(
            gam[L - 1:L, :], (V7X_SUBLANES, W))
        ar_all = jnp.concatenate([alpha[rows, :] * jnp.exp(cum - ld), r[rows, :] * gam],
                                 axis=0).astype(BF16)
        bk_all = jnp.concatenate([beta[rows, :] * gam_inv, k[rows, :] * gam_inv],
                                 axis=0).astype(BF16)
        ar_s[c] = ar_all
        bk_s[c] = bk_all

    heads = [(c, hd) for c in range(n_chunks) for hd in range(RW_HEADS)]
    hcols = lambda hd: slice(hd * N, (hd + 1) * N)
    invs, ps = [], []
    for c, hd in heads:
        pair = _dot_nt(ar_s[c, :, hcols(hd)], bk_s[c, :, hcols(hd)])
        m_b = jnp.where(mask2, pair[:, :L], 0.0)
        mk_s[c * RW_HEADS + hd] = jnp.where(mask2, pair[:, L:], 0.0).astype(BF16)
        rb_s[c * RW_HEADS + hd] = m_b[L:].astype(BF16)
        invs.append(eye + m_b[:L])
        ps.append(m_b[:L].astype(BF16))
    ps = [_dot(pb, pb) for pb in ps]
    for _ in range(int(math.log2(L)) - 2):
        both = [_dot(jnp.concatenate([p.astype(BF16), inv.astype(BF16)], axis=0), p.astype(BF16))
                for p, inv in zip(ps, invs)]
        ps = [b2[:L] for b2 in both]
        invs = [inv + b2[L:] for inv, b2 in zip(invs, both)]
    for i, (inv, p) in enumerate(zip(invs, ps)):
        inv_s[i] = (inv + _dot(inv.astype(BF16), p.astype(BF16))).astype(BF16)

    def chunk(c, carry):
        rows = pl.ds(pl.multiple_of(c * L, L), L)
        ar_all = ar_s[c]
        bk_all = bk_s[c]
        v_all = v_s[rows, :]
        gend = gend_s[pl.ds(pl.multiple_of(c * V7X_SUBLANES, V7X_SUBLANES), 1), :]
        hs = range(RW_HEADS)
        sts = [state_ref[hd] for hd in hs]
        bases = [_dot_nt(ar_all[:, hcols(hd)], sts[hd].astype(BF16))
                 + _dot(mk_s[c * RW_HEADS + hd], v_all[:, hcols(hd)]) for hd in hs]
        us = [_dot(inv_s[c * RW_HEADS + hd], bases[hd][:L].astype(BF16)).astype(BF16) for hd in hs]
        for hd in hs:
            y_s[rows, hcols(hd)] = bases[hd][L:] + _dot(rb_s[c * RW_HEADS + hd], us[hd])
        for hd in hs:
            st = sts[hd] + _dot_tn(jnp.concatenate([us[hd], v_all[:, hcols(hd)]], axis=0),
                                   bk_all[:, hcols(hd)])
            state_ref[hd] = st * gend[:, hcols(hd)]
        return carry

    lax.fori_loop(0, n_chunks, chunk, 0)

    y = y_s[...]
    inv_n = 1.0 / N
    m = _dot_select(y, ones) * inv_n
    yc = y - m
    var = _dot_select(yc * yc, ones) * inv_n
    yn = yc * lax.rsqrt(var + RW_LN_EPS) * lng_ref[...] + lnb_ref[...]
    o_ref[...] = ((yn + bonus) * g).astype(o_ref.dtype)


def _mixer_rwkv(h, w_c, mu, w0, a0, lora_w, g_up, k_k, k_a, r_k, ln_g, ln_b, head_ones):
    B, S, D = h.shape
    tm = SEQ_TILE
    W = BRANCH_W
    L = RW_CHUNK
    nc = tm // L
    blk = lambda b, j: (b, j, 0)
    row = lambda n: _full((1, n))
    return pl.pallas_call(
        _rwkv_kernel,
        grid=(B, S // tm),
        in_specs=[pl.BlockSpec((None, tm, D), blk), _full((D, RW_IN)), row(RW_IN), row(W), row(W),
                  _full((2 * RW_LORA, 2 * W)), _full((RW_GATE_LORA, W)), row(W), row(W), row(W),
                  row(W), row(W), _full(head_ones.shape)],
        out_specs=pl.BlockSpec((None, tm, W), blk),
        out_shape=jax.ShapeDtypeStruct((B, S, W), BF16),
        scratch_shapes=[pltpu.VMEM((V7X_SUBLANES, RW_IN), F32),
                        pltpu.VMEM((RW_HEADS, RW_HEAD_DIM, RW_HEAD_DIM), F32),
                        pltpu.VMEM((tm, W), BF16),
                        pltpu.VMEM((tm, W), F32),
                        pltpu.VMEM((nc * V7X_SUBLANES, W), F32),
                        pltpu.VMEM((nc, 2 * L, W), BF16),
                        pltpu.VMEM((nc, 2 * L, W), BF16),
                        pltpu.VMEM((nc * RW_HEADS, L, L), BF16),
                        pltpu.VMEM((nc * RW_HEADS, L, L), BF16),
                        pltpu.VMEM((nc * RW_HEADS, 2 * L, L), BF16)],
        compiler_params=_cparams("parallel", "arbitrary"),
    )(h, w_c, mu, w0, a0, lora_w, g_up, k_k, k_a, r_k, ln_g, ln_b, head_ones)


def _s5_kernel(h_ref, w_ref, bre_ref, bim_ref, are_ref, aim_ref, cre_ref, cim_ref,
               d_ref, gw_ref, gb_ref, o_ref, xr_s, xi_s, st_s):
    @pl.when(pl.program_id(1) == 0)
    def _():
        st_s[...] = jnp.zeros_like(st_s)

    tm = h_ref.shape[0]
    u = _dot(h_ref[...].astype(BF16), w_ref[...])
    ub = u.astype(BF16)
    xr_s[...] = _dot(ub, bre_ref[...])
    xi_s[...] = _dot(ub, bim_ref[...])
    ar = are_ref[...]
    ai = aim_ref[...]

    def step(t, carry):
        xr, xi = carry
        row = pl.ds(t, 1)
        nr = ar * xr - ai * xi + xr_s[row, :]
        ni = ar * xi + ai * xr + xi_s[row, :]
        xr_s[row, :] = nr
        xi_s[row, :] = ni
        return nr, ni

    xr, xi = lax.fori_loop(0, tm, step, (st_s[0:1, :], st_s[1:2, :]), unroll=8)
    st_s[0:1, :] = xr
    st_s[1:2, :] = xi
    y = (_dot(xr_s[...].astype(BF16), cre_ref[...]) - _dot(xi_s[...].astype(BF16), cim_ref[...])
         + d_ref[...] * u)
    y = jax.nn.gelu(y)
    gate = jax.nn.sigmoid(_dot(y.astype(BF16), gw_ref[...]) + gb_ref[...])
    o_ref[...] = (y * gate).astype(o_ref.dtype)


def _mixer_s5(h, w_d, bre, bim, are, aim, cre, cim, d, glu_w, glu_b):
    B, S, D = h.shape
    tm = TOKEN_TILE
    W = BRANCH_W
    blk = lambda b, j: (b, j, 0)
    return pl.pallas_call(
        _s5_kernel,
        grid=(B, S // tm),
        in_specs=[pl.BlockSpec((None, tm, D), blk), _full((D, W)),
                  _full((W, S5_N)), _full((W, S5_N)), _full(are.shape), _full(aim.shape),
                  _full((S5_N, W)), _full((S5_N, W)), _full((1, W)), _full((W, W)), _full((1, W))],
        out_specs=pl.BlockSpec((None, tm, W), blk),
        out_shape=jax.ShapeDtypeStruct((B, S, W), BF16),
        scratch_shapes=[pltpu.VMEM((tm, S5_N), F32), pltpu.VMEM((tm, S5_N), F32),
                        pltpu.VMEM((V7X_SUBLANES, S5_N), F32)],
        compiler_params=_cparams("parallel", "arbitrary"),
    )(h, w_d, bre, bim, are, aim, cre, cim, d, glu_w, glu_b)


def _merge_kernel(h_ref, oa_ref, ob_ref, oc_ref, od_ref, wg_ref, gb_ref, br_ref, wo_ref,
                  g_ref, b_ref, o_ref):
    D = D_MODEL
    h = h_ref[...]
    hb = h.astype(BF16)
    merged = None
    for i, br_in in enumerate((oa_ref, ob_ref, oc_ref, od_ref)):
        gate = jax.nn.sigmoid(_dot(hb, wg_ref[:, i * D:(i + 1) * D]) + gb_ref[i:i + 1, :])
        term = gate * _dot(br_in[...], br_ref[i])
        merged = term if merged is None else merged + term
    y = _dot(merged.astype(BF16), wo_ref[...])
    o_ref[...] = _layer_norm(DN_ALPHA * h + y, g_ref[...], b_ref[...])


def _merge(h2, oa, ob, oc, od, w_g, gate_b, br_proj, w_out, g, b):
    T, D = h2.shape
    tm = TOKEN_TILE
    W = BRANCH_W
    tok = lambda n: pl.BlockSpec((tm, n), lambda i: (i, 0))
    return pl.pallas_call(
        _merge_kernel,
        grid=(T // tm,),
        in_specs=[tok(D), tok(W), tok(W), tok(W), tok(W),
                  _full((D, N_BRANCH * D)), _full((N_BRANCH, D)), _full((N_BRANCH, W, D)),
                  _full((D, D)), _full((1, D)), _full((1, D))],
        out_specs=tok(D),
        out_shape=jax.ShapeDtypeStruct((T, D), F32),
        compiler_params=_cparams("parallel"),
    )(h2, oa, ob, oc, od, w_g, gate_b, br_proj, w_out, g, b)


def _kv_kernel(m_ref, wk_ref, wv_ref, k_ref, v_ref):
    mb = m_ref[...].astype(BF16)
    k_ref[...] = _dot(mb, wk_ref[...]).astype(k_ref.dtype)
    v_ref[...] = _dot(mb, wv_ref[...]).astype(v_ref.dtype)


def _kv_proj(mem2, wk, wv):
    R, D = mem2.shape
    tm = TOKEN_TILE
    tok = pl.BlockSpec((tm, D), lambda i: (i, 0))
    return pl.pallas_call(
        _kv_kernel,
        grid=(R // tm,),
        in_specs=[tok, _full((D, D)), _full((D, D))],
        out_specs=[tok, tok],
        out_shape=[jax.ShapeDtypeStruct((R, D), BF16)] * 2,
        compiler_params=_cparams("parallel"),
    )(mem2, wk, wv)


def _attn_kernel(h_ref, wq_ref, k_ref, v_ref, wo_ref, g_ref, b_ref, o_ref, ot_ref):
    h = h_ref[...]
    q = _dot(h.astype(BF16), wq_ref[...]).astype(BF16)
    scale = XA_HEAD_DIM ** -0.5
    outs = []
    for hd in range(XA_HEADS):
        cols = slice(hd * XA_HEAD_DIM, (hd + 1) * XA_HEAD_DIM)
        s = _dot_nt(q[:, cols], k_ref[:, cols]) * scale
        s = s - jnp.max(s, axis=-1, keepdims=True)
        p = jnp.exp(s)
        p = p / jnp.sum(p, axis=-1, keepdims=True)
        outs.append(_dot(p.astype(BF16), v_ref[:, cols]))
    o = jnp.concatenate(outs, axis=-1).astype(BF16)
    y = _dot(o, wo_ref[...])
    out = _layer_norm(DN_ALPHA * h + y, g_ref[...], b_ref[...])
    o_ref[...] = out
    _store_row_tiles(ot_ref, out)


def _cross_attention(h, kmem, vmem, wq, wo, g, b):
    B, S, D = h.shape
    M = kmem.shape[1]
    tm = TOKEN_TILE
    n = S // tm
    blk = lambda bi, j: (bi, j, 0)
    mem = lambda bi, j: (bi, 0, 0)
    return pl.pallas_call(
        _attn_kernel,
        grid=(B, n),
        in_specs=[pl.BlockSpec((None, tm, D), blk), _full((D, D)),
                  pl.BlockSpec((None, M, D), mem), pl.BlockSpec((None, M, D), mem),
                  _full((D, D)), _full((1, D)), _full((1, D))],
        out_specs=[pl.BlockSpec((None, tm, D), blk),
                   pl.BlockSpec((tm * ROW_TILE, V7X_LANES), lambda bi, j: (bi * n + j, 0))],
        out_shape=[jax.ShapeDtypeStruct((B, S, D), F32),
                   jax.ShapeDtypeStruct((B * S * ROW_TILE, V7X_LANES), F32)],
        compiler_params=_cparams("parallel", "parallel"),
    )(h, wq, kmem, vmem, wo, g, b)


def _router_kernel(h_ref, w_ref, b_ref, route_ref, rw_ref, cnt_ref, carry_ref):
    @pl.when(pl.program_id(0) == 0)
    def _():
        carry_ref[...] = jnp.zeros_like(carry_ref)

    tm = h_ref.shape[0]
    E = N_EXPERTS
    logits = _dot_split(h_ref[...], w_ref[...]) + b_ref[...]
    lane = lax.broadcasted_iota(jnp.int32, (tm, E), 1)
    work = logits
    vals, hots, idxs = [], [], []
    for _ in range(TOP_K):
        m = jnp.max(work, axis=-1, keepdims=True)
        idx = jnp.min(jnp.where(work == m, lane, E), axis=-1, keepdims=True)
        hot = lane == idx
        work = jnp.where(hot, -jnp.inf, work)
        vals.append(m)
        hots.append(hot)
        idxs.append(idx)
    exps = [jnp.exp(v - vals[0]) for v in vals]
    denom = exps[0] + exps[1] + exps[2] + exps[3]
    multi = (hots[0] | hots[1] | hots[2] | hots[3]).astype(F32)
    ti = lax.broadcasted_iota(jnp.int32, (tm, tm), 0)
    tj = lax.broadcasted_iota(jnp.int32, (tm, tm), 1)
    before = (ti > tj).astype(BF16)
    prefix = _dot(before, multi.astype(BF16)) + carry_ref[0:1, :]
    out_lane = lax.broadcasted_iota(jnp.int32, (tm, ROUTE_LANES), 1)
    route = jnp.zeros((tm, ROUTE_LANES), jnp.int32)
    rw = jnp.zeros((tm, ROUTE_LANES), F32)
    for kk in range(TOP_K):
        rank = jnp.sum(jnp.where(hots[kk], prefix, 0.0), axis=-1, keepdims=True).astype(jnp.int32)
        route = jnp.where(out_lane == kk, idxs[kk], route)
        route = jnp.where(out_lane == TOP_K + kk, rank, route)
        rw = jnp.where(out_lane == kk, exps[kk] / denom, rw)
    route_ref[...] = route
    rw_ref[...] = rw
    carry_ref[0:1, :] = carry_ref[0:1, :] + jnp.sum(multi, axis=0, keepdims=True)
    cnt_ref[...] = carry_ref[...]


def _router(h2, router_w, router_b):
    T, D = h2.shape
    tm = TOKEN_TILE
    E = N_EXPERTS
    return pl.pallas_call(
        _router_kernel,
        grid=(T // tm,),
        in_specs=[pl.BlockSpec((tm, D), lambda i: (i, 0)), _full((D, E)), _full((1, E))],
        out_specs=[pl.BlockSpec((tm, ROUTE_LANES), lambda i: (i, 0)),
                   pl.BlockSpec((tm, ROUTE_LANES), lambda i: (i, 0)),
                   _full((V7X_SUBLANES, E))],
        out_shape=[jax.ShapeDtypeStruct((T, ROUTE_LANES), jnp.int32),
                   jax.ShapeDtypeStruct((T, ROUTE_LANES), F32),
                   jax.ShapeDtypeStruct((V7X_SUBLANES, E), F32)],
        scratch_shapes=[pltpu.VMEM((V7X_SUBLANES, E), F32)],
        compiler_params=_cparams("arbitrary"),
    )(h2, router_w, router_b)


def _dest_kernel(route_ref, start_ref, dest_ref):
    route = route_ref[...]
    tm = route.shape[0]
    lane = lax.broadcasted_iota(jnp.int32, (tm, N_EXPERTS), 1)
    out_lane = lax.broadcasted_iota(jnp.int32, (tm, ROUTE_LANES), 1)
    dest = jnp.zeros((tm, ROUTE_LANES), jnp.int32)
    start = start_ref[...]
    for kk in range(TOP_K):
        idx = route[:, kk:kk + 1]
        base = jnp.sum(jnp.where(lane == idx, start, 0), axis=-1, keepdims=True)
        dest = jnp.where(out_lane == kk, base + route[:, TOP_K + kk:TOP_K + kk + 1], dest)
    dest_ref[...] = dest


def _dest_slots(route, pad_start):
    T = route.shape[0]
    tm = TOKEN_TILE * 2
    return pl.pallas_call(
        _dest_kernel,
        grid=(T // tm,),
        in_specs=[pl.BlockSpec((tm, ROUTE_LANES), lambda i: (i, 0)), _full((1, N_EXPERTS))],
        out_specs=pl.BlockSpec((tm, ROUTE_LANES), lambda i: (i, 0)),
        out_shape=jax.ShapeDtypeStruct((T, ROUTE_LANES), jnp.int32),
        compiler_params=_cparams("parallel"),
    )(route, pad_start.reshape(1, N_EXPERTS))


def _slot_source_kernel(fill_lo_ref, fill_hi_ref, dest_hbm, row_ref, dest_s, sem, *, n_tok):
    i = pl.program_id(0)
    n = dest_s.shape[0]

    @pl.when(i == 0)
    def _():
        def fill(s, carry):
            parity = (s // MOE_TILE + 1) % 2
            row_ref[s] = n_tok * TOP_K + parity * MOE_TILE + s % MOE_TILE
            return carry

        lax.fori_loop(0, MOE_TILE, fill, 0)
        for e in range(fill_lo_ref.shape[0]):
            lax.fori_loop(MOE_TILE + fill_lo_ref[e], MOE_TILE + fill_hi_ref[e], fill, 0)

    idx_copy = pltpu.make_async_copy(dest_hbm.at[i], dest_s, sem)
    idx_copy.start()
    idx_copy.wait()
    slots = row_ref.at[pl.ds(MOE_TILE, row_ref.shape[0] - MOE_TILE)]
    first_tok = i * (n // TOP_K)

    def body(t, carry):
        for kk in range(TOP_K):
            slots[dest_s[t * TOP_K + kk]] = (kk * n_tok + first_tok) + t
        return carry

    lax.fori_loop(0, n // TOP_K, body, 0, unroll=8)


def _slot_sources(fill_lo, fill_hi, dest_tiles, n_slots):
    n_tok_tiles, n = dest_tiles.shape
    return pl.pallas_call(
        functools.partial(_slot_source_kernel, n_tok=n_tok_tiles * n // TOP_K),
        grid_spec=pltpu.PrefetchScalarGridSpec(
            num_scalar_prefetch=2,
            grid=(n_tok_tiles,),
            in_specs=[pl.BlockSpec(memory_space=pl.ANY)],
            out_specs=pl.BlockSpec(memory_space=pltpu.SMEM),
            scratch_shapes=[pltpu.SMEM((n,), jnp.int32), pltpu.SemaphoreType.DMA]),
        out_shape=jax.ShapeDtypeStruct((MOE_TILE + n_slots,), jnp.int32),
        compiler_params=_cparams("arbitrary"),
    )(fill_lo, fill_hi, dest_tiles)


GU_BLOCK = 2 * V7X_LANES


def _unzip_kernel(w_ref, p_ref, o_ref):
    for c in range(w_ref.shape[1] // GU_BLOCK):
        cols = slice(c * GU_BLOCK, (c + 1) * GU_BLOCK)
        o_ref[:, cols] = _dot(w_ref[:, cols].astype(BF16), p_ref[...]).astype(o_ref.dtype)


def _unzip_gate_up(w1_layers, layer):
    _, E, D, F2 = w1_layers.shape
    tr = TOKEN_TILE
    j = jnp.arange(GU_BLOCK)
    perm = (j[:, None] == jnp.where(j < V7X_LANES, 2 * j, 2 * (j - V7X_LANES) + 1)[None, :])
    return pl.pallas_call(
        _unzip_kernel,
        grid=(E, D // tr),
        in_specs=[pl.BlockSpec((None, None, tr, F2), lambda e, r: (layer, e, r, 0)),
                  _full((GU_BLOCK, GU_BLOCK))],
        out_specs=pl.BlockSpec((None, tr, F2), lambda e, r: (e, r, 0)),
        out_shape=jax.ShapeDtypeStruct((E, D, F2), BF16),
        compiler_params=_cparams("parallel", "parallel"),
    )(w1_layers, perm.astype(BF16))


def _unzip_bias(b1):
    E, F2 = b1.shape
    return b1.reshape(E, F2 // GU_BLOCK, V7X_LANES, 2).transpose(0, 1, 3, 2).reshape(E, 1, F2)


def _expert_ffn(x, w1_ref, b1_ref, w2_ref, b2_ref):
    hid = _dot(x, w1_ref[...]) + b1_ref[...]
    acts = []
    for c in range(hid.shape[1] // GU_BLOCK):
        gate = jnp.minimum(hid[:, c * GU_BLOCK:c * GU_BLOCK + V7X_LANES], SWIGLU_LIMIT)
        up = jnp.clip(hid[:, c * GU_BLOCK + V7X_LANES:(c + 1) * GU_BLOCK],
                      -SWIGLU_LIMIT, SWIGLU_LIMIT)
        acts.append(((up + 1.0) * gate * jax.nn.sigmoid(SWIGLU_ALPHA * gate)).astype(BF16))
    return _dot(jnp.concatenate(acts, axis=-1), w2_ref[...].astype(BF16)) + b2_ref[...]


def _expert_kernel(blk_e_ref, n_used_ref, src_ref, h_hbm, w1_ref, b1_ref, w2_ref, b2_ref, yk_hbm,
                   xbuf0, xbuf1, ybuf0, ybuf1, gsem, ssem):
    del blk_e_ref
    i = pl.program_id(0)
    n_used = n_used_ref[0]
    xbufs = (xbuf0, xbuf1)
    ybufs = (ybuf0, ybuf1)
    R = ROW_TILE
    tm = xbuf0.shape[0] // R
    T = h_hbm.shape[0] // R
    n_assign = T * TOP_K
    def tile_rows(r):
        return pl.ds(r * R if isinstance(r, int) else pl.multiple_of(r * R, R), R)

    def gather_row(tile, b, j, priority):
        row = src_ref[(tile + 1) * tm + j]
        tok = row & (T - 1) if T & (T - 1) == 0 else lax.rem(row, T)
        pltpu.make_async_copy(h_hbm.at[tile_rows(tok)], xbufs[b].at[tile_rows(j)],
                              gsem.at[b]).start(priority=priority)

    def scatter_row(tile, b, j, priority):
        row = src_ref[(tile + 1) * tm + j]
        pltpu.make_async_copy(ybufs[b].at[tile_rows(j)], yk_hbm.at[tile_rows(row)],
                              ssem.at[b]).start(priority=priority)

    def wait_gather(b):
        pltpu.make_async_copy(h_hbm.at[pl.ds(0, tm * R)], xbufs[b], gsem.at[b]).wait()

    def wait_scatter(b):
        pltpu.make_async_copy(ybufs[b], yk_hbm.at[pl.ds(0, tm * R)], ssem.at[b]).wait()

    def rows_loop(fn):
        def body(j, carry):
            fn(j)
            return carry
        lax.fori_loop(0, tm, body, 0)

    @pl.when(i == 0)
    def _():
        rows_loop(lambda j: gather_row(0, 0, j, 0))
        ybuf0[...] = jnp.zeros_like(ybuf0)
        ybuf1[...] = jnp.zeros_like(ybuf1)
        pltpu.make_async_copy(ybuf0, yk_hbm.at[pl.ds(n_assign * R, tm * R)], ssem.at[0]).start()

    for b in range(2):
        @pl.when((i < n_used) & (i % 2 == b))
        def _(b=b):
            wait_gather(b)
            wait_scatter(b)
            for j in range(tm):
                gather_row(i + 1, 1 - b, j, 0)
                scatter_row(i - 1, 1 - b, j, 1)
            x = _load_row_tiles(xbufs[b]).astype(BF16)
            _store_row_tiles(ybufs[b], _expert_ffn(x, w1_ref, b1_ref, w2_ref, b2_ref))

        @pl.when((i == n_used) & (i % 2 == b))
        def _(b=b):
            wait_gather(b)
            wait_scatter(b)
            rows_loop(lambda j: scatter_row(i - 1, 1 - b, j, 0))
            wait_scatter(1 - b)


def _experts(blk_e, n_used, src, h_tiles, w1, b1, w2_layers, layer, b2):
    R = ROW_TILE
    T = h_tiles.shape[0] // R
    D = D_MODEL
    tm = MOE_TILE
    F = D_FF
    n_tiles = src.shape[0] // tm - 1
    per_e = lambda shape: pl.BlockSpec((None,) + shape, lambda i, be, nu, sr: (be[i], 0, 0))
    w2_spec = pl.BlockSpec((None, None, F, D), lambda i, be, nu, sr: (layer, be[i], 0, 0))
    return pl.pallas_call(
        _expert_kernel,
        grid_spec=pltpu.PrefetchScalarGridSpec(
            num_scalar_prefetch=3,
            grid=(n_tiles,),
            in_specs=[pl.BlockSpec(memory_space=pl.ANY),
                      per_e((D, 2 * F)), per_e((1, 2 * F)), w2_spec, per_e((1, D))],
            out_specs=pl.BlockSpec(memory_space=pl.ANY),
            scratch_shapes=[pltpu.VMEM((tm * R, V7X_LANES), F32)] * 4
                           + [pltpu.SemaphoreType.DMA((2,)), pltpu.SemaphoreType.DMA((2,))]),
        out_shape=jax.ShapeDtypeStruct(((T * TOP_K + 2 * tm) * R, V7X_LANES), F32),
        compiler_params=_cparams("arbitrary"),
    )(blk_e, n_used, src, h_tiles, w1, b1, w2_layers, b2)


def _combine_kernel(y0_ref, y1_ref, y2_ref, y3_ref, h_ref, rw_ref, g_ref, b_ref, o_ref):
    rw = rw_ref[...]
    moe = rw[:, 0:1] * _load_row_tiles(y0_ref)
    for kk, y_ref in enumerate((y1_ref, y2_ref, y3_ref), start=1):
        moe = moe + rw[:, kk:kk + 1] * _load_row_tiles(y_ref)
    o_ref[...] = _layer_norm(DN_ALPHA * h_ref[...] + moe, g_ref[...], b_ref[...])


def _combine(yk, h2, rw, g, b):
    T, D = h2.shape
    tm = TOKEN_TILE
    n = T // tm
    choice = lambda kk: pl.BlockSpec((tm * ROW_TILE, V7X_LANES), lambda i: (kk * n + i, 0))
    return pl.pallas_call(
        _combine_kernel,
        grid=(n,),
        in_specs=[choice(kk) for kk in range(TOP_K)]
                 + [pl.BlockSpec((tm, D), lambda i: (i, 0)),
                    pl.BlockSpec((tm, ROUTE_LANES), lambda i: (i, 0)),
                    _full((1, D)), _full((1, D))],
        out_specs=pl.BlockSpec((tm, D), lambda i: (i, 0)),
        out_shape=jax.ShapeDtypeStruct((T, D), F32),
        compiler_params=_cparams("parallel"),
    )(yk, yk, yk, yk, h2, rw, g, b)


def _s5_discretise(a_re, a_im, b_re, b_im, c_re, c_im, log_dt):
    G, P, C = S5_GROUPS, S5_STATE, S5_CH
    dt = jnp.exp(log_dt)[:, None]
    mag = jnp.exp(a_re * dt)
    abar_re = mag * jnp.cos(a_im * dt)
    abar_im = mag * jnp.sin(a_im * dt)
    den = a_re * a_re + a_im * a_im
    num_re = abar_re - 1.0
    coef_re = (num_re * a_re + abar_im * a_im) / den
    coef_im = (abar_im * a_re - num_re * a_im) / den
    bbar_re = coef_re[..., None] * b_re - coef_im[..., None] * b_im
    bbar_im = coef_re[..., None] * b_im + coef_im[..., None] * b_re
    eye = jnp.eye(G, dtype=F32)
    to_in = lambda m: jnp.einsum('gpc,gh->gchp', m, eye).reshape(G * C, G * P)
    to_out = lambda m: jnp.einsum('gcp,gh->gphc', m, eye).reshape(G * P, G * C)
    return (to_in(bbar_re).astype(BF16), to_in(bbar_im).astype(BF16),
            abar_re.reshape(1, G * P), abar_im.reshape(1, G * P),
            to_out(c_re).astype(BF16), to_out(c_im).astype(BF16))


def _moe_layer(h2, h_tiles, layer, router_w, router_b, w1_layers, b1, w2_layers, b2, g, b):
    T, D = h2.shape
    E = N_EXPERTS
    tm = TOKEN_TILE
    route, rw, counts = _router(h2, router_w, router_b.reshape(1, E))
    counts = counts[0].astype(jnp.int32)
    padded = (counts + MOE_TILE - 1) // MOE_TILE * MOE_TILE
    pad_end = jnp.cumsum(padded)
    pad_start = pad_end - padded
    n_tiles = T * TOP_K // MOE_TILE + E
    n_slots = n_tiles * MOE_TILE
    tile_start = jnp.arange(n_tiles, dtype=jnp.int32) * MOE_TILE
    blk_e = jnp.minimum(jnp.sum(tile_start[:, None] >= pad_end[None, :], axis=1), E - 1).astype(jnp.int32)
    n_used = (pad_end[E - 1:] // MOE_TILE).astype(jnp.int32)
    dest = _dest_slots(route, pad_start.astype(jnp.int32))
    chunk = min(SLOT_MAP_CHUNK, T * TOP_K)
    dest_tiles = dest[:, :TOP_K].reshape(T * TOP_K // chunk, chunk)
    fill_lo = jnp.concatenate([pad_start + counts, pad_end[E - 1:]]).astype(jnp.int32)
    fill_hi = jnp.concatenate([pad_end, jnp.full((1,), n_slots)]).astype(jnp.int32)
    src = _slot_sources(fill_lo, fill_hi, dest_tiles, n_slots)
    yk = _experts(blk_e, n_used, src, h_tiles, _unzip_gate_up(w1_layers, layer), _unzip_bias(b1),
                  w2_layers, layer, b2[:, None, :])
    return _combine(yk, h2, rw, g.reshape(1, D), b.reshape(1, D))


def kernel(x, mem, ln_in_g, ln_in_b, w_in, conv_w, sg_norm_g, sg_norm_b, sg_w, sg_b, rw_mu, rw_w0, rw_w_up, rw_a0, rw_a_up, rw_g_up, rw_k_k, rw_k_a, rw_r_k, rw_ln_g, rw_ln_b, s5_a_re, s5_a_im, s5_b_re, s5_b_im, s5_c_re, s5_c_im, s5_d, s5_log_dt, s5_glu_w, s5_glu_b, br_proj, gate_b, w_out, ln1_g, ln1_b, xa_wq, xa_wk, xa_wv, xa_wo, ln2_g, ln2_b, router_w, router_b, ex_w1, ex_b1, ex_w2, ex_b2, ln3_g, ln3_b):
    B, S, D = x.shape
    M = mem.shape[1]
    T = B * S
    W = BRANCH_W
    row = lambda v: v.reshape(1, -1)
    pos = jnp.arange(SG_BLOCK)
    sg_mask = (pos[None, :] // CHUNK) <= (pos[:, None] // CHUNK)
    head_ones = jnp.kron(jnp.eye(RW_HEADS, dtype=F32), jnp.ones((RW_HEAD_DIM, RW_HEAD_DIM), F32))
    head_ones = jnp.concatenate([head_ones, head_ones], axis=0).astype(BF16)
    mem2 = mem.reshape(B * M, D)

    h = _input_ln(x.reshape(T, D), ln_in_g, ln_in_b)
    for l in range(DEPTH):
        w_l = w_in[l].astype(BF16)
        h3 = h.reshape(B, S, D)
        sg_wm = jnp.where(sg_mask[None], sg_w[l], 0.0).astype(BF16)
        sg_bias = jnp.repeat(sg_b[l].T, W // SG_GROUPS, axis=1)
        o_a, o_b = _mixers_ab(h3, w_l[:, :OFF_C], conv_w[l], row(sg_norm_g[l]), row(sg_norm_b[l]),
                              sg_wm, sg_bias)
        zero = jnp.zeros((RW_LORA, W), F32)
        lora_w = jnp.concatenate([jnp.concatenate([rw_w_up[l], zero], axis=1),
                                  jnp.concatenate([zero, rw_a_up[l]], axis=1)], axis=0)
        o_c = _mixer_rwkv(h3, w_l[:, OFF_C:OFF_D], row(rw_mu[l]), row(rw_w0[l]), row(rw_a0[l]),
                          lora_w, rw_g_up[l].astype(BF16), row(rw_k_k[l]), row(rw_k_a[l]),
                          row(rw_r_k[l]),
                          row(rw_ln_g[l]), row(rw_ln_b[l]), head_ones)
        s5p = _s5_discretise(s5_a_re[l], s5_a_im[l], s5_b_re[l], s5_b_im[l], s5_c_re[l],
                             s5_c_im[l], s5_log_dt[l])
        o_d = _mixer_s5(h3, w_l[:, OFF_D:OFF_G], *s5p, row(s5_d[l]),
                        s5_glu_w[l].astype(BF16), row(s5_glu_b[l]))
        flat = lambda o: o.reshape(T, W)
        h = _merge(h, flat(o_a), flat(o_b), flat(o_c), flat(o_d), w_l[:, OFF_G:], gate_b[l],
                   br_proj[l].astype(BF16), w_out[l].astype(BF16), row(ln1_g[l]), row(ln1_b[l]))
        kmem, vmem = _kv_proj(mem2, xa_wk[l].astype(BF16), xa_wv[l].astype(BF16))
        h, h_tiles = _cross_attention(h.reshape(B, S, D), kmem.reshape(B, M, D),
                                      vmem.reshape(B, M, D), xa_wq[l].astype(BF16),
                                      xa_wo[l].astype(BF16), row(ln2_g[l]), row(ln2_b[l]))
        h = _moe_layer(h.reshape(T, D), h_tiles, l, router_w[l], router_b[l], ex_w1, ex_b1[l],
                       ex_w2, ex_b2[l], ln3_g[l], ln3_b[l])
    return h.reshape(B, S, D)
```

```python
import functools
import math

import jax
import jax.numpy as jnp
from jax import lax
from jax.experimental import pallas as pl
from jax.experimental.pallas import tpu as pltpu

F32 = jnp.float32
BF16 = jnp.bfloat16

D_MODEL = 1024
DEPTH = 2
CHUNK = 64
BRANCH_W = 256
N_BRANCH = 4
CONV_W = 3
SG_BLOCK = 128
SG_GROUPS = 4
RW_HEADS = 4
RW_HEAD_DIM = 64
RW_LORA = 64
RW_GATE_LORA = 128
RW_IN = 3 * BRANCH_W + 2 * RW_LORA + RW_GATE_LORA
RW_LN_EPS = 64e-5
S5_CH = 16
S5_GROUPS = BRANCH_W // S5_CH
S5_STATE = 64
S5_N = S5_GROUPS * S5_STATE
OFF_B = 3 * BRANCH_W
OFF_C = OFF_B + 2 * BRANCH_W
OFF_D = OFF_C + RW_IN
OFF_G = OFF_D + BRANCH_W
XA_HEADS = 4
XA_HEAD_DIM = D_MODEL // XA_HEADS
N_EXPERTS = 32
TOP_K = 4
TOP_K_BITS = 2
D_FF = D_MODEL
SWIGLU_LIMIT = 7.0
SWIGLU_ALPHA = 1.702
LN_EPS = 1e-5
DN_ALPHA = (2 * DEPTH) ** 0.25

V7X_LANES = 128
V7X_SUBLANES = 8
V7X_VMEM_LIMIT_BYTES = 56 * 1024 * 1024
GATHER_PRIORITY, SCATTER_PRIORITY = 0, 1
TOKEN_TILE = 512
SEQ_TILE = 512
SLOT_MAP_CHUNK = 8192
RW_CHUNK = 64
MOE_TILE = 512
ROUTE_LANES = 128


def _cparams(*sem):
    return pltpu.CompilerParams(dimension_semantics=sem,
                                vmem_limit_bytes=V7X_VMEM_LIMIT_BYTES)


def _full(shape):
    nd = len(shape)
    return pl.BlockSpec(shape, lambda *_: (0,) * nd)


def _layer_norm(x, g, b, eps=LN_EPS):
    mu = jnp.mean(x, axis=-1, keepdims=True)
    xc = x - mu
    var = jnp.mean(xc * xc, axis=-1, keepdims=True)
    return xc * lax.rsqrt(var + eps) * g + b


def _contract(a, b, dims, **kw):
    return lax.dot_general(a, b, (dims, ((), ())), preferred_element_type=F32, **kw)


def _dot(a, b, **kw):
    return _contract(a, b, ((1,), (0,)), **kw)


def _bf16_terms(x, n):
    terms = []
    for _ in range(n):
        t = x.astype(BF16)
        terms.append(t)
        x = x - t.astype(F32)
    return terms


def _dot_split(a, b):
    ah, al = _bf16_terms(a, 2)
    bh, bl = _bf16_terms(b, 2)
    return _dot(jnp.concatenate([ah, ah, al], axis=1), jnp.concatenate([bh, bl, bh], axis=0))


def _dot_select(x, sel_stack):
    n = sel_stack.shape[0] // x.shape[1]
    return _dot(jnp.concatenate(_bf16_terms(x, n), axis=1), sel_stack)


def _dot_nt(a, b, **kw):
    return _contract(a, b, ((1,), (1,)), **kw)


def _dot_tn(a, b, **kw):
    return _contract(a, b, ((0,), (0,)), **kw)


ROW_TILE = D_MODEL // V7X_LANES


def _store_row_tiles(ref, x):
    n = x.shape[0]
    for c in range(ROW_TILE):
        ref[pl.ds(c, n, stride=ROW_TILE), :] = x[:, c * V7X_LANES:(c + 1) * V7X_LANES]


def _load_row_tiles(ref):
    n = ref.shape[0] // ROW_TILE
    return jnp.concatenate([ref[pl.ds(c, n, stride=ROW_TILE), :] for c in range(ROW_TILE)], axis=-1)


def _shift_rows(x, n, tail):
    rolled = pltpu.roll(x, n, 0)
    row = lax.broadcasted_iota(jnp.int32, x.shape, 0)
    out = rolled
    for i in range(n):
        src = tail[V7X_SUBLANES - n + i:V7X_SUBLANES - n + i + 1, :]
        out = jnp.where(row == i, src, out)
    return out


def _ln_kernel(x_ref, g_ref, b_ref, o_ref):
    o_ref[...] = _layer_norm(x_ref[...], g_ref[...], b_ref[...])


def _input_ln(x2, g, b):
    T, D = x2.shape
    tm = TOKEN_TILE * 2
    return pl.pallas_call(
        _ln_kernel,
        grid=(T // tm,),
        in_specs=[pl.BlockSpec((tm, D), lambda i: (i, 0)), _full((1, D)), _full((1, D))],
        out_specs=pl.BlockSpec((tm, D), lambda i: (i, 0)),
        out_shape=jax.ShapeDtypeStruct((T, D), F32),
        compiler_params=_cparams("parallel"),
    )(x2, g.reshape(1, D), b.reshape(1, D))


def _ab_kernel(h_ref, w_ref, cw_ref, ng_ref, nb_ref, sw_ref, sb_ref,
               oa_ref, ob_ref, tail_ref):
    @pl.when(pl.program_id(1) == 0)
    def _():
        tail_ref[...] = jnp.zeros_like(tail_ref)

    tm = h_ref.shape[0]
    W = BRANCH_W
    z = _dot(h_ref[...].astype(BF16), w_ref[...])
    ch = z[:, W:2 * W] * z[:, 2 * W:3 * W]
    tail = tail_ref[...]
    s1 = _shift_rows(ch, 1, tail)
    s2 = _shift_rows(ch, 2, tail)
    tail_ref[...] = ch[tm - V7X_SUBLANES:, :]
    cw = cw_ref[...]
    y = cw[2:3, :] * ch + cw[1:2, :] * s1 + cw[0:1, :] * s2
    oa_ref[...] = (z[:, 0:W] * y).astype(oa_ref.dtype)
    u = z[:, 3 * W:4 * W]
    v = _layer_norm(z[:, 4 * W:5 * W], ng_ref[...], nb_ref[...]).astype(BF16)
    gw = W // SG_GROUPS
    grp = lax.broadcasted_iota(jnp.int32, (SG_BLOCK, W), 1) // gw
    for blk in range(tm // SG_BLOCK):
        rows = slice(blk * SG_BLOCK, (blk + 1) * SG_BLOCK)
        vb = v[rows, :]
        sv = sb_ref[...]
        for g in range(SG_GROUPS):
            sv = sv + jnp.where(grp == g, _dot(sw_ref[g], vb), 0.0)
        ob_ref[rows, :] = (u[rows, :] * sv).astype(ob_ref.dtype)


def _mixers_ab(h, w_ab, conv_w, ng, nb, sg_wm, sg_bias):
    B, S, D = h.shape
    tm = TOKEN_TILE
    W = BRANCH_W
    blk = lambda b, j: (b, j, 0)
    return pl.pallas_call(
        _ab_kernel,
        grid=(B, S // tm),
        in_specs=[pl.BlockSpec((None, tm, D), blk),
                  _full((D, 5 * W)), _full((CONV_W, W)), _full((1, W)), _full((1, W)),
                  _full((SG_GROUPS, SG_BLOCK, SG_BLOCK)), _full((SG_BLOCK, W))],
        out_specs=[pl.BlockSpec((None, tm, W), blk), pl.BlockSpec((None, tm, W), blk)],
        out_shape=[jax.ShapeDtypeStruct((B, S, W), BF16)] * 2,
        scratch_shapes=[pltpu.VMEM((V7X_SUBLANES, W), F32)],
        compiler_params=_cparams("parallel", "arbitrary"),
    )(h, w_ab, conv_w, ng, nb, sg_wm, sg_bias)


def _softplus(x):
    return jnp.maximum(x, 0.0) + jnp.log(1.0 + jnp.exp(-jnp.abs(x)))


def _rwkv_kernel(h_ref, w_ref, mu_ref, w0_ref, a0_ref, lora_ref, gup_ref, kk_ref, ka_ref,
                 rk_ref, lng_ref, lnb_ref, ones_ref, o_ref,
                 tail_ref, state_ref, v_s, y_s, gend_s, ar_s, bk_s, inv_s, rb_s, mk_s):
    @pl.when(pl.program_id(1) == 0)
    def _():
        tail_ref[...] = jnp.zeros_like(tail_ref)
        state_ref[...] = jnp.zeros_like(state_ref)

    tm = h_ref.shape[0]
    W = BRANCH_W
    N = RW_HEAD_DIM
    L = RW_CHUNK
    z = _dot(h_ref[...].astype(BF16), w_ref[...])
    zprev = _shift_rows(z, 1, tail_ref[...])
    tail_ref[...] = z[tm - V7X_SUBLANES:, :]
    z = z + (zprev - z) * mu_ref[...]
    r = z[:, 0:W]
    k = z[:, W:2 * W]
    v = z[:, 2 * W:3 * W]
    xwa = z[:, 3 * W:3 * W + 2 * RW_LORA]
    xg = z[:, 3 * W + 2 * RW_LORA:]
    lane = lax.broadcasted_iota(jnp.int32, xwa.shape, 1)
    lora_in = jnp.where(lane < RW_LORA, jnp.tanh(xwa), xwa)
    lora = _dot_split(lora_in, lora_ref[...])
    w = -_softplus(-(w0_ref[...] + lora[:, 0:W])) - 0.5
    log_decay = -jnp.exp(w)
    a = jax.nn.sigmoid(a0_ref[...] + lora[:, W:2 * W])
    g = _dot(jax.nn.sigmoid(xg).astype(BF16), gup_ref[...])
    ones = ones_ref[...]
    kk = k * kk_ref[...]
    kk = kk / jnp.maximum(jnp.sqrt(_dot_select(kk * kk, ones)), 1e-12)
    k = k * (1.0 + (a - 1.0) * ka_ref[...])
    bonus = _dot_select(r * k * rk_ref[...], ones) * v

    ti = lax.broadcasted_iota(jnp.int32, (L, L), 0)
    tj = lax.broadcasted_iota(jnp.int32, (L, L), 1)
    tri3 = jnp.concatenate([(ti >= tj).astype(BF16)] * 3, axis=1)
    eye = (ti == tj).astype(F32)
    t2 = lax.broadcasted_iota(jnp.int32, (2 * L, L), 0)
    s2 = lax.broadcasted_iota(jnp.int32, (2 * L, L), 1)
    mask2 = ((t2 < L) & (t2 > s2)) | (t2 - L >= s2)
    n_chunks = tm // L

    alpha = -kk
    beta = kk * a
    v_s[...] = v.astype(BF16)
    for c in range(n_chunks):
        rows = slice(c * L, (c + 1) * L)
        ld = log_decay[rows, :]
        cum = _dot(tri3, jnp.concatenate(_bf16_terms(ld, 3), axis=0))
        gam = jnp.exp(cum)
        gam_inv = jnp.exp(-cum)
        gend_s[c * V7X_SUBLANES:(c + 1) * V7X_SUBLANES, :] = jnp.broadcast_to(
            gam[L - 1:L, :], (V7X_SUBLANES, W))
        ar_all = jnp.concatenate([alpha[rows, :] * jnp.exp(cum - ld), r[rows, :] * gam],
                                 axis=0).astype(BF16)
        bk_all = jnp.concatenate([beta[rows, :] * gam_inv, k[rows, :] * gam_inv],
                                 axis=0).astype(BF16)
        ar_s[c] = ar_all
        bk_s[c] = bk_all

    heads = [(c, hd) for c in range(n_chunks) for hd in range(RW_HEADS)]
    hcols = lambda hd: slice(hd * N, (hd + 1) * N)
    invs, ps = [], []
    for c, hd in heads:
        pair = _dot_nt(ar_s[c, :, hcols(hd)], bk_s[c, :, hcols(hd)])
        m_b = jnp.where(mask2, pair[:, :L], 0.0)
        mk_s[c * RW_HEADS + hd] = jnp.where(mask2, pair[:, L:], 0.0).astype(BF16)
        rb_s[c * RW_HEADS + hd] = m_b[L:].astype(BF16)
        invs.append(eye + m_b[:L])
        ps.append(m_b[:L].astype(BF16))
    ps = [_dot(pb, pb) for pb in ps]
    for _ in range(int(math.log2(L)) - 2):
        both = [_dot(jnp.concatenate([p.astype(BF16), inv.astype(BF16)], axis=0), p.astype(BF16))
                for p, inv in zip(ps, invs)]
        ps = [b2[:L] for b2 in both]
        invs = [inv + b2[L:] for inv, b2 in zip(invs, both)]
    for i, (inv, p) in enumerate(zip(invs, ps)):
        inv_s[i] = (inv + _dot(inv.astype(BF16), p.astype(BF16))).astype(BF16)

    def chunk(c, carry):
        rows = pl.ds(pl.multiple_of(c * L, L), L)
        ar_all = ar_s[c]
        bk_all = bk_s[c]
        v_all = v_s[rows, :]
        gend = gend_s[pl.ds(pl.multiple_of(c * V7X_SUBLANES, V7X_SUBLANES), 1), :]
        hs = range(RW_HEADS)
        sts = [state_ref[hd] for hd in hs]
        bases = [_dot_nt(ar_all[:, hcols(hd)], sts[hd].astype(BF16))
                 + _dot(mk_s[c * RW_HEADS + hd], v_all[:, hcols(hd)]) for hd in hs]
        us = [_dot(inv_s[c * RW_HEADS + hd], bases[hd][:L].astype(BF16)).astype(BF16) for hd in hs]
        for hd in hs:
            y_s[rows, hcols(hd)] = bases[hd][L:] + _dot(rb_s[c * RW_HEADS + hd], us[hd])
        for hd in hs:
            st = sts[hd] + _dot_tn(jnp.concatenate([us[hd], v_all[:, hcols(hd)]], axis=0),
                                   bk_all[:, hcols(hd)])
            state_ref[hd] = st * gend[:, hcols(hd)]
        return carry

    lax.fori_loop(0, n_chunks, chunk, 0)

    y = y_s[...]
    inv_n = 1.0 / N
    m = _dot_select(y, ones) * inv_n
    yc = y - m
    var = _dot_select(yc * yc, ones) * inv_n
    yn = yc * lax.rsqrt(var + RW_LN_EPS) * lng_ref[...] + lnb_ref[...]
    o_ref[...] = ((yn + bonus) * g).astype(o_ref.dtype)


def _mixer_rwkv(h, w_c, mu, w0, a0, lora_w, g_up, k_k, k_a, r_k, ln_g, ln_b, head_ones):
    B, S, D = h.shape
    tm = SEQ_TILE
    W = BRANCH_W
    L = RW_CHUNK
    nc = tm // L
    blk = lambda b, j: (b, j, 0)
    row = lambda n: _full((1, n))
    return pl.pallas_call(
        _rwkv_kernel,
        grid=(B, S // tm),
        in_specs=[pl.BlockSpec((None, tm, D), blk), _full((D, RW_IN)), row(RW_IN), row(W), row(W),
                  _full((2 * RW_LORA, 2 * W)), _full((RW_GATE_LORA, W)), row(W), row(W), row(W),
                  row(W), row(W), _full(head_ones.shape)],
        out_specs=pl.BlockSpec((None, tm, W), blk),
        out_shape=jax.ShapeDtypeStruct((B, S, W), BF16),
        scratch_shapes=[pltpu.VMEM((V7X_SUBLANES, RW_IN), F32),
                        pltpu.VMEM((RW_HEADS, RW_HEAD_DIM, RW_HEAD_DIM), F32),
                        pltpu.VMEM((tm, W), BF16),
                        pltpu.VMEM((tm, W), F32),
                        pltpu.VMEM((nc * V7X_SUBLANES, W), F32),
                        pltpu.VMEM((nc, 2 * L, W), BF16),
                        pltpu.VMEM((nc, 2 * L, W), BF16),
                        pltpu.VMEM((nc * RW_HEADS, L, L), BF16),
                        pltpu.VMEM((nc * RW_HEADS, L, L), BF16),
                        pltpu.VMEM((nc * RW_HEADS, 2 * L, L), BF16)],
        compiler_params=_cparams("parallel", "arbitrary"),
    )(h, w_c, mu, w0, a0, lora_w, g_up, k_k, k_a, r_k, ln_g, ln_b, head_ones)


def _s5_kernel(h_ref, w_ref, bre_ref, bim_ref, are_ref, aim_ref, cre_ref, cim_ref,
               d_ref, gw_ref, gb_ref, o_ref, xr_s, xi_s, st_s):
    @pl.when(pl.program_id(1) == 0)
    def _():
        st_s[...] = jnp.zeros_like(st_s)

    tm = h_ref.shape[0]
    u = _dot(h_ref[...].astype(BF16), w_ref[...])
    ub = u.astype(BF16)
    xr_s[...] = _dot(ub, bre_ref[...])
    xi_s[...] = _dot(ub, bim_ref[...])
    ar = are_ref[...]
    ai = aim_ref[...]

    def step(t, carry):
        xr, xi = carry
        row = pl.ds(t, 1)
        nr = ar * xr - ai * xi + xr_s[row, :]
        ni = ar * xi + ai * xr + xi_s[row, :]
        xr_s[row, :] = nr
        xi_s[row, :] = ni
        return nr, ni

    xr, xi = lax.fori_loop(0, tm, step, (st_s[0:1, :], st_s[1:2, :]), unroll=8)
    st_s[0:1, :] = xr
    st_s[1:2, :] = xi
    y = (_dot(xr_s[...].astype(BF16), cre_ref[...]) - _dot(xi_s[...].astype(BF16), cim_ref[...])
         + d_ref[...] * u)
    y = jax.nn.gelu(y)
    gate = jax.nn.sigmoid(_dot(y.astype(BF16), gw_ref[...]) + gb_ref[...])
    o_ref[...] = (y * gate).astype(o_ref.dtype)


def _mixer_s5(h, w_d, bre, bim, are, aim, cre, cim, d, glu_w, glu_b):
    B, S, D = h.shape
    tm = TOKEN_TILE
    W = BRANCH_W
    blk = lambda b, j: (b, j, 0)
    return pl.pallas_call(
        _s5_kernel,
        grid=(B, S // tm),
        in_specs=[pl.BlockSpec((None, tm, D), blk), _full((D, W)),
                  _full((W, S5_N)), _full((W, S5_N)), _full(are.shape), _full(aim.shape),
                  _full((S5_N, W)), _full((S5_N, W)), _full((1, W)), _full((W, W)), _full((1, W))],
        out_specs=pl.BlockSpec((None, tm, W), blk),
        out_shape=jax.ShapeDtypeStruct((B, S, W), BF16),
        scratch_shapes=[pltpu.VMEM((tm, S5_N), F32), pltpu.VMEM((tm, S5_N), F32),
                        pltpu.VMEM((V7X_SUBLANES, S5_N), F32)],
        compiler_params=_cparams("parallel", "arbitrary"),
    )(h, w_d, bre, bim, are, aim, cre, cim, d, glu_w, glu_b)


def _merge_kernel(h_ref, oa_ref, ob_ref, oc_ref, od_ref, wg_ref, gb_ref, br_ref, wo_ref,
                  g_ref, b_ref, o_ref):
    D = D_MODEL
    h = h_ref[...]
    hb = h.astype(BF16)
    merged = None
    for i, br_in in enumerate((oa_ref, ob_ref, oc_ref, od_ref)):
        gate = jax.nn.sigmoid(_dot(hb, wg_ref[:, i * D:(i + 1) * D]) + gb_ref[i:i + 1, :])
        term = gate * _dot(br_in[...], br_ref[i])
        merged = term if merged is None else merged + term
    y = _dot(merged.astype(BF16), wo_ref[...])
    o_ref[...] = _layer_norm(DN_ALPHA * h + y, g_ref[...], b_ref[...])


def _merge(h2, oa, ob, oc, od, w_g, gate_b, br_proj, w_out, g, b):
    T, D = h2.shape
    tm = TOKEN_TILE
    W = BRANCH_W
    tok = lambda n: pl.BlockSpec((tm, n), lambda i: (i, 0))
    return pl.pallas_call(
        _merge_kernel,
        grid=(T // tm,),
        in_specs=[tok(D), tok(W), tok(W), tok(W), tok(W),
                  _full((D, N_BRANCH * D)), _full((N_BRANCH, D)), _full((N_BRANCH, W, D)),
                  _full((D, D)), _full((1, D)), _full((1, D))],
        out_specs=tok(D),
        out_shape=jax.ShapeDtypeStruct((T, D), F32),
        compiler_params=_cparams("parallel"),
    )(h2, oa, ob, oc, od, w_g, gate_b, br_proj, w_out, g, b)


def _kv_kernel(m_ref, wk_ref, wv_ref, k_ref, v_ref):
    mb = m_ref[...].astype(BF16)
    k_ref[...] = _dot(mb, wk_ref[...]).astype(k_ref.dtype)
    v_ref[...] = _dot(mb, wv_ref[...]).astype(v_ref.dtype)


def _kv_proj(mem2, wk, wv):
    R, D = mem2.shape
    tm = TOKEN_TILE
    tok = pl.BlockSpec((tm, D), lambda i: (i, 0))
    return pl.pallas_call(
        _kv_kernel,
        grid=(R // tm,),
        in_specs=[tok, _full((D, D)), _full((D, D))],
        out_specs=[tok, tok],
        out_shape=[jax.ShapeDtypeStruct((R, D), BF16)] * 2,
        compiler_params=_cparams("parallel"),
    )(mem2, wk, wv)


def _attn_kernel(h_ref, wq_ref, k_ref, v_ref, wo_ref, g_ref, b_ref, o_ref, ot_ref):
    h = h_ref[...]
    q = _dot(h.astype(BF16), wq_ref[...]).astype(BF16)
    scale = XA_HEAD_DIM ** -0.5
    outs = []
    for hd in range(XA_HEADS):
        cols = slice(hd * XA_HEAD_DIM, (hd + 1) * XA_HEAD_DIM)
        s = _dot_nt(q[:, cols], k_ref[:, cols]) * scale
        s = s - jnp.max(s, axis=-1, keepdims=True)
        p = jnp.exp(s)
        p = p / jnp.sum(p, axis=-1, keepdims=True)
        outs.append(_dot(p.astype(BF16), v_ref[:, cols]))
    o = jnp.concatenate(outs, axis=-1).astype(BF16)
    y = _dot(o, wo_ref[...])
    out = _layer_norm(DN_ALPHA * h + y, g_ref[...], b_ref[...])
    o_ref[...] = out
    _store_row_tiles(ot_ref, out)


def _cross_attention(h, kmem, vmem, wq, wo, g, b):
    B, S, D = h.shape
    M = kmem.shape[1]
    tm = TOKEN_TILE
    n = S // tm
    blk = lambda bi, j: (bi, j, 0)
    mem = lambda bi, j: (bi, 0, 0)
    return pl.pallas_call(
        _attn_kernel,
        grid=(B, n),
        in_specs=[pl.BlockSpec((None, tm, D), blk), _full((D, D)),
                  pl.BlockSpec((None, M, D), mem), pl.BlockSpec((None, M, D), mem),
                  _full((D, D)), _full((1, D)), _full((1, D))],
        out_specs=[pl.BlockSpec((None, tm, D), blk),
                   pl.BlockSpec((tm * ROW_TILE, V7X_LANES), lambda bi, j: (bi * n + j, 0))],
        out_shape=[jax.ShapeDtypeStruct((B, S, D), F32),
                   jax.ShapeDtypeStruct((B * S * ROW_TILE, V7X_LANES), F32)],
        compiler_params=_cparams("parallel", "parallel"),
    )(h, wq, kmem, vmem, wo, g, b)


def _router_kernel(h_ref, w_ref, b_ref, route_ref, rw_ref, cnt_ref, carry_ref):
    @pl.when(pl.program_id(0) == 0)
    def _():
        carry_ref[...] = jnp.zeros_like(carry_ref)

    tm = h_ref.shape[0]
    E = N_EXPERTS
    logits = _dot_split(h_ref[...], w_ref[...]) + b_ref[...]
    lane = lax.broadcasted_iota(jnp.int32, (tm, E), 1)
    work = logits
    vals, hots, idxs = [], [], []
    for _ in range(TOP_K):
        m = jnp.max(work, axis=-1, keepdims=True)
        idx = jnp.min(jnp.where(work == m, lane, E), axis=-1, keepdims=True)
        hot = lane == idx
        work = jnp.where(hot, -jnp.inf, work)
        vals.append(m)
        hots.append(hot)
        idxs.append(idx)
    exps = [jnp.exp(v - vals[0]) for v in vals]
    denom = exps[0] + exps[1] + exps[2] + exps[3]
    multi = (hots[0] | hots[1] | hots[2] | hots[3]).astype(F32)
    ti = lax.broadcasted_iota(jnp.int32, (tm, tm), 0)
    tj = lax.broadcasted_iota(jnp.int32, (tm, tm), 1)
    before = (ti > tj).astype(BF16)
    prefix = _dot(before, multi.astype(BF16)) + carry_ref[0:1, :]
    out_lane = lax.broadcasted_iota(jnp.int32, (tm, ROUTE_LANES), 1)
    route = jnp.zeros((tm, ROUTE_LANES), jnp.int32)
    rw = jnp.zeros((tm, ROUTE_LANES), F32)
    for kk in range(TOP_K):
        rank = jnp.sum(jnp.where(hots[kk], prefix, 0.0), axis=-1, keepdims=True).astype(jnp.int32)
        route = jnp.where(out_lane == kk, idxs[kk], route)
        route = jnp.where(out_lane == TOP_K + kk, rank, route)
        rw = jnp.where(out_lane == kk, exps[kk] / denom, rw)
    route_ref[...] = route
    rw_ref[...] = rw
    carry_ref[0:1, :] = carry_ref[0:1, :] + jnp.sum(multi, axis=0, keepdims=True)
    cnt_ref[...] = carry_ref[...]


def _router(h2, router_w, router_b):
    T, D = h2.shape
    tm = TOKEN_TILE
    E = N_EXPERTS
    return pl.pallas_call(
        _router_kernel,
        grid=(T // tm,),
        in_specs=[pl.BlockSpec((tm, D), lambda i: (i, 0)), _full((D, E)), _full((1, E))],
        out_specs=[pl.BlockSpec((tm, ROUTE_LANES), lambda i: (i, 0)),
                   pl.BlockSpec((tm, ROUTE_LANES), lambda i: (i, 0)),
                   _full((V7X_SUBLANES, E))],
        out_shape=[jax.ShapeDtypeStruct((T, ROUTE_LANES), jnp.int32),
                   jax.ShapeDtypeStruct((T, ROUTE_LANES), F32),
                   jax.ShapeDtypeStruct((V7X_SUBLANES, E), F32)],
        scratch_shapes=[pltpu.VMEM((V7X_SUBLANES, E), F32)],
        compiler_params=_cparams("arbitrary"),
    )(h2, router_w, router_b)


def _dest_kernel(route_ref, start_ref, dest_ref):
    route = route_ref[...]
    tm = route.shape[0]
    lane = lax.broadcasted_iota(jnp.int32, (tm, N_EXPERTS), 1)
    out_lane = lax.broadcasted_iota(jnp.int32, (tm, ROUTE_LANES), 1)
    dest = jnp.zeros((tm, ROUTE_LANES), jnp.int32)
    start = start_ref[...]
    for kk in range(TOP_K):
        idx = route[:, kk:kk + 1]
        base = jnp.sum(jnp.where(lane == idx, start, 0), axis=-1, keepdims=True)
        dest = jnp.where(out_lane == kk, base + route[:, TOP_K + kk:TOP_K + kk + 1], dest)
    dest_ref[...] = dest


def _dest_slots(route, pad_start):
    T = route.shape[0]
    tm = TOKEN_TILE * 2
    return pl.pallas_call(
        _dest_kernel,
        grid=(T // tm,),
        in_specs=[pl.BlockSpec((tm, ROUTE_LANES), lambda i: (i, 0)), _full((1, N_EXPERTS))],
        out_specs=pl.BlockSpec((tm, ROUTE_LANES), lambda i: (i, 0)),
        out_shape=jax.ShapeDtypeStruct((T, ROUTE_LANES), jnp.int32),
        compiler_params=_cparams("parallel"),
    )(route, pad_start.reshape(1, N_EXPERTS))


def _slot_source_kernel(fill_lo_ref, fill_hi_ref, dest_hbm, row_ref, dest_s, sem, *, n_tok):
    i = pl.program_id(0)
    n = dest_s.shape[0]

    @pl.when(i == 0)
    def _():
        def fill(s, carry):
            parity = (s // MOE_TILE + 1) % 2
            row_ref[s] = n_tok * TOP_K + parity * MOE_TILE + s % MOE_TILE
            return carry

        lax.fori_loop(0, MOE_TILE, fill, 0)
        for e in range(fill_lo_ref.shape[0]):
            lax.fori_loop(MOE_TILE + fill_lo_ref[e], MOE_TILE + fill_hi_ref[e], fill, 0)

    idx_copy = pltpu.make_async_copy(dest_hbm.at[i], dest_s, sem)
    idx_copy.start()
    idx_copy.wait()
    slots = row_ref.at[pl.ds(MOE_TILE, row_ref.shape[0] - MOE_TILE)]
    first_tok = i * (n // TOP_K)

    def body(t, carry):
        for kk in range(TOP_K):
            slots[dest_s[t * TOP_K + kk]] = (kk * n_tok + first_tok) + t
        return carry

    lax.fori_loop(0, n // TOP_K, body, 0, unroll=8)


def _slot_sources(fill_lo, fill_hi, dest_tiles, n_slots):
    n_tok_tiles, n = dest_tiles.shape
    return pl.pallas_call(
        functools.partial(_slot_source_kernel, n_tok=n_tok_tiles * n // TOP_K),
        grid_spec=pltpu.PrefetchScalarGridSpec(
            num_scalar_prefetch=2,
            grid=(n_tok_tiles,),
            in_specs=[pl.BlockSpec(memory_space=pl.ANY)],
            out_specs=pl.BlockSpec(memory_space=pltpu.SMEM),
            scratch_shapes=[pltpu.SMEM((n,), jnp.int32), pltpu.SemaphoreType.DMA]),
        out_shape=jax.ShapeDtypeStruct((MOE_TILE + n_slots,), jnp.int32),
        compiler_params=_cparams("arbitrary"),
    )(fill_lo, fill_hi, dest_tiles)


GU_BLOCK = 2 * V7X_LANES


def _unzip_kernel(w_ref, p_ref, o_ref):
    for c in range(w_ref.shape[1] // GU_BLOCK):
        cols = slice(c * GU_BLOCK, (c + 1) * GU_BLOCK)
        o_ref[:, cols] = _dot(w_ref[:, cols].astype(BF16), p_ref[...]).astype(o_ref.dtype)


def _unzip_gate_up(w1_layers, layer):
    _, E, D, F2 = w1_layers.shape
    tr = TOKEN_TILE
    j = jnp.arange(GU_BLOCK)
    perm = (j[:, None] == jnp.where(j < V7X_LANES, 2 * j, 2 * (j - V7X_LANES) + 1)[None, :])
    return pl.pallas_call(
        _unzip_kernel,
        grid=(E, D // tr),
        in_specs=[pl.BlockSpec((None, None, tr, F2), lambda e, r: (layer, e, r, 0)),
                  _full((GU_BLOCK, GU_BLOCK))],
        out_specs=pl.BlockSpec((None, tr, F2), lambda e, r: (e, r, 0)),
        out_shape=jax.ShapeDtypeStruct((E, D, F2), BF16),
        compiler_params=_cparams("parallel", "parallel"),
    )(w1_layers, perm.astype(BF16))


def _unzip_bias(b1):
    E, F2 = b1.shape
    return b1.reshape(E, F2 // GU_BLOCK, V7X_LANES, 2).transpose(0, 1, 3, 2).reshape(E, 1, F2)


def _expert_ffn(x, w1_ref, b1_ref, w2_ref, b2_ref):
    hid = _dot(x, w1_ref[...]) + b1_ref[...]
    acts = []
    for c in range(hid.shape[1] // GU_BLOCK):
        gate = jnp.minimum(hid[:, c * GU_BLOCK:c * GU_BLOCK + V7X_LANES], SWIGLU_LIMIT)
        up = jnp.clip(hid[:, c * GU_BLOCK + V7X_LANES:(c + 1) * GU_BLOCK],
                      -SWIGLU_LIMIT, SWIGLU_LIMIT)
        acts.append(((up + 1.0) * gate * jax.nn.sigmoid(SWIGLU_ALPHA * gate)).astype(BF16))
    return _dot(jnp.concatenate(acts, axis=-1), w2_ref[...].astype(BF16)) + b2_ref[...]


def _expert_kernel(blk_e_ref, n_used_ref, src_ref, h_hbm, w1_ref, b1_ref, w2_ref, b2_ref, yk_hbm,
                   xbuf0, xbuf1, ybuf0, ybuf1, gsem, ssem):
    del blk_e_ref
    i = pl.program_id(0)
    n_used = n_used_ref[0]
    xbufs = (xbuf0, xbuf1)
    ybufs = (ybuf0, ybuf1)
    R = ROW_TILE
    tm = xbuf0.shape[0] // R
    T = h_hbm.shape[0] // R
    n_assign = T * TOP_K

    def tile_rows(r):
        return pl.ds(r * R if isinstance(r, int) else pl.multiple_of(r * R, R), R)

    def gather_row(tile, b, j):
        row = src_ref[(tile + 1) * tm + j]
        tok = row & (T - 1) if T & (T - 1) == 0 else lax.rem(row, T)
        pltpu.make_async_copy(h_hbm.at[tile_rows(tok)], xbufs[b].at[tile_rows(j)],
                              gsem.at[b]).start(priority=GATHER_PRIORITY)

    def scatter_row(tile, b, j):
        row = src_ref[(tile + 1) * tm + j]
        pltpu.make_async_copy(ybufs[b].at[tile_rows(j)], yk_hbm.at[tile_rows(row)],
                              ssem.at[b]).start(priority=SCATTER_PRIORITY)

    def wait_gather(b):
        pltpu.make_async_copy(h_hbm.at[pl.ds(0, tm * R)], xbufs[b], gsem.at[b]).wait()

    def wait_scatter(b):
        pltpu.make_async_copy(ybufs[b], yk_hbm.at[pl.ds(0, tm * R)], ssem.at[b]).wait()

    def rows_loop(fn):
        def body(j, carry):
            fn(j)
            return carry
        lax.fori_loop(0, tm, body, 0)

    @pl.when(i == 0)
    def _():
        rows_loop(lambda j: gather_row(0, 0, j))
        ybuf0[...] = jnp.zeros_like(ybuf0)
        ybuf1[...] = jnp.zeros_like(ybuf1)
        pltpu.make_async_copy(ybuf0, yk_hbm.at[pl.ds(n_assign * R, tm * R)], ssem.at[0]).start()

    for b in range(2):
        @pl.when((i < n_used) & (i % 2 == b))
        def _(b=b):
            wait_gather(b)
            wait_scatter(b)
            for j in range(tm):
                gather_row(i + 1, 1 - b, j)
                scatter_row(i - 1, 1 - b, j)
            x = _load_row_tiles(xbufs[b]).astype(BF16)
            _store_row_tiles(ybufs[b], _expert_ffn(x, w1_ref, b1_ref, w2_ref, b2_ref))

        @pl.when((i == n_used) & (i % 2 == b))
        def _(b=b):
            wait_gather(b)
            wait_scatter(b)
            rows_loop(lambda j: scatter_row(i - 1, 1 - b, j))
            wait_scatter(1 - b)


def _experts(blk_e, n_used, src, h_tiles, w1, b1, w2_layers, layer, b2):
    R = ROW_TILE
    T = h_tiles.shape[0] // R
    D = D_MODEL
    tm = MOE_TILE
    F = D_FF
    n_tiles = src.shape[0] // tm - 1
    per_e = lambda shape: pl.BlockSpec((None,) + shape, lambda i, be, nu, sr: (be[i], 0, 0))
    w2_spec = pl.BlockSpec((None, None, F, D), lambda i, be, nu, sr: (layer, be[i], 0, 0))
    return pl.pallas_call(
        _expert_kernel,
        grid_spec=pltpu.PrefetchScalarGridSpec(
            num_scalar_prefetch=3,
            grid=(n_tiles,),
            in_specs=[pl.BlockSpec(memory_space=pl.ANY),
                      per_e((D, 2 * F)), per_e((1, 2 * F)), w2_spec, per_e((1, D))],
            out_specs=pl.BlockSpec(memory_space=pl.ANY),
            scratch_shapes=[pltpu.VMEM((tm * R, V7X_LANES), F32)] * 4
                           + [pltpu.SemaphoreType.DMA((2,)), pltpu.SemaphoreType.DMA((2,))]),
        out_shape=jax.ShapeDtypeStruct(((T * TOP_K + 2 * tm) * R, V7X_LANES), F32),
        compiler_params=_cparams("arbitrary"),
    )(blk_e, n_used, src, h_tiles, w1, b1, w2_layers, b2)


def _combine_kernel(y0_ref, y1_ref, y2_ref, y3_ref, h_ref, rw_ref, g_ref, b_ref, o_ref):
    rw = rw_ref[...]
    moe = rw[:, 0:1] * _load_row_tiles(y0_ref)
    for kk, y_ref in enumerate((y1_ref, y2_ref, y3_ref), start=1):
        moe = moe + rw[:, kk:kk + 1] * _load_row_tiles(y_ref)
    o_ref[...] = _layer_norm(DN_ALPHA * h_ref[...] + moe, g_ref[...], b_ref[...])


def _combine(yk, h2, rw, g, b):
    T, D = h2.shape
    tm = TOKEN_TILE
    n = T // tm
    choice = lambda kk: pl.BlockSpec((tm * ROW_TILE, V7X_LANES), lambda i: (kk * n + i, 0))
    return pl.pallas_call(
        _combine_kernel,
        grid=(n,),
        in_specs=[choice(kk) for kk in range(TOP_K)]
                 + [pl.BlockSpec((tm, D), lambda i: (i, 0)),
                    pl.BlockSpec((tm, ROUTE_LANES), lambda i: (i, 0)),
                    _full((1, D)), _full((1, D))],
        out_specs=pl.BlockSpec((tm, D), lambda i: (i, 0)),
        out_shape=jax.ShapeDtypeStruct((T, D), F32),
        compiler_params=_cparams("parallel"),
    )(yk, yk, yk, yk, h2, rw, g, b)


def _s5_discretise(a_re, a_im, b_re, b_im, c_re, c_im, log_dt):
    G, P, C = S5_GROUPS, S5_STATE, S5_CH
    dt = jnp.exp(log_dt)[:, None]
    mag = jnp.exp(a_re * dt)
    abar_re = mag * jnp.cos(a_im * dt)
    abar_im = mag * jnp.sin(a_im * dt)
    den = a_re * a_re + a_im * a_im
    num_re = abar_re - 1.0
    coef_re = (num_re * a_re + abar_im * a_im) / den
    coef_im = (abar_im * a_re - num_re * a_im) / den
    bbar_re = coef_re[..., None] * b_re - coef_im[..., None] * b_im
    bbar_im = coef_re[..., None] * b_im + coef_im[..., None] * b_re
    eye = jnp.eye(G, dtype=F32)
    to_in = lambda m: jnp.einsum('gpc,gh->gchp', m, eye).reshape(G * C, G * P)
    to_out = lambda m: jnp.einsum('gcp,gh->gphc', m, eye).reshape(G * P, G * C)
    return (to_in(bbar_re).astype(BF16), to_in(bbar_im).astype(BF16),
            abar_re.reshape(1, G * P), abar_im.reshape(1, G * P),
            to_out(c_re).astype(BF16), to_out(c_im).astype(BF16))


def _moe_layer(h2, h_tiles, layer, router_w, router_b, w1_layers, b1, w2_layers, b2, g, b):
    T, D = h2.shape
    E = N_EXPERTS
    tm = TOKEN_TILE
    route, rw, counts = _router(h2, router_w, router_b.reshape(1, E))
    counts = counts[0].astype(jnp.int32)
    padded = (counts + MOE_TILE - 1) // MOE_TILE * MOE_TILE
    pad_end = jnp.cumsum(padded)
    pad_start = pad_end - padded
    n_tiles = T * TOP_K // MOE_TILE + E
    n_slots = n_tiles * MOE_TILE
    tile_start = jnp.arange(n_tiles, dtype=jnp.int32) * MOE_TILE
    blk_e = jnp.minimum(jnp.sum(tile_start[:, None] >= pad_end[None, :], axis=1), E - 1).astype(jnp.int32)
    n_used = (pad_end[E - 1:] // MOE_TILE).astype(jnp.int32)
    dest = _dest_slots(route, pad_start.astype(jnp.int32))
    chunk = min(SLOT_MAP_CHUNK, T * TOP_K)
    dest_tiles = dest[:, :TOP_K].reshape(T * TOP_K // chunk, chunk)
    fill_lo = jnp.concatenate([pad_start + counts, pad_end[E - 1:]]).astype(jnp.int32)
    fill_hi = jnp.concatenate([pad_end, jnp.full((1,), n_slots)]).astype(jnp.int32)
    src = _slot_sources(fill_lo, fill_hi, dest_tiles, n_slots)
    yk = _experts(blk_e, n_used, src, h_tiles, _unzip_gate_up(w1_layers, layer), _unzip_bias(b1),
                  w2_layers, layer, b2[:, None, :])
    return _combine(yk, h2, rw, g.reshape(1, D), b.reshape(1, D))


def kernel(x, mem, ln_in_g, ln_in_b, w_in, conv_w, sg_norm_g, sg_norm_b, sg_w, sg_b, rw_mu, rw_w0, rw_w_up, rw_a0, rw_a_up, rw_g_up, rw_k_k, rw_k_a, rw_r_k, rw_ln_g, rw_ln_b, s5_a_re, s5_a_im, s5_b_re, s5_b_im, s5_c_re, s5_c_im, s5_d, s5_log_dt, s5_glu_w, s5_glu_b, br_proj, gate_b, w_out, ln1_g, ln1_b, xa_wq, xa_wk, xa_wv, xa_wo, ln2_g, ln2_b, router_w, router_b, ex_w1, ex_b1, ex_w2, ex_b2, ln3_g, ln3_b):
    B, S, D = x.shape
    M = mem.shape[1]
    T = B * S
    W = BRANCH_W
    row = lambda v: v.reshape(1, -1)
    pos = jnp.arange(SG_BLOCK)
    sg_mask = (pos[None, :] // CHUNK) <= (pos[:, None] // CHUNK)
    head_ones = jnp.kron(jnp.eye(RW_HEADS, dtype=F32), jnp.ones((RW_HEAD_DIM, RW_HEAD_DIM), F32))
    head_ones = jnp.concatenate([head_ones, head_ones], axis=0).astype(BF16)
    mem2 = mem.reshape(B * M, D)

    h = _input_ln(x.reshape(T, D), ln_in_g, ln_in_b)
    for l in range(DEPTH):
        w_l = w_in[l].astype(BF16)
        h3 = h.reshape(B, S, D)
        sg_wm = jnp.where(sg_mask[None], sg_w[l], 0.0).astype(BF16)
        sg_bias = jnp.repeat(sg_b[l].T, W // SG_GROUPS, axis=1)
        o_a, o_b = _mixers_ab(h3, w_l[:, :OFF_C], conv_w[l], row(sg_norm_g[l]), row(sg_norm_b[l]),
                              sg_wm, sg_bias)
        zero = jnp.zeros((RW_LORA, W), F32)
        lora_w = jnp.concatenate([jnp.concatenate([rw_w_up[l], zero], axis=1),
                                  jnp.concatenate([zero, rw_a_up[l]], axis=1)], axis=0)
        o_c = _mixer_rwkv(h3, w_l[:, OFF_C:OFF_D], row(rw_mu[l]), row(rw_w0[l]), row(rw_a0[l]),
                          lora_w, rw_g_up[l].astype(BF16), row(rw_k_k[l]), row(rw_k_a[l]),
                          row(rw_r_k[l]),
                          row(rw_ln_g[l]), row(rw_ln_b[l]), head_ones)
        s5p = _s5_discretise(s5_a_re[l], s5_a_im[l], s5_b_re[l], s5_b_im[l], s5_c_re[l],
                             s5_c_im[l], s5_log_dt[l])
        o_d = _mixer_s5(h3, w_l[:, OFF_D:OFF_G], *s5p, row(s5_d[l]),
                        s5_glu_w[l].astype(BF16), row(s5_glu_b[l]))
        flat = lambda o: o.reshape(T, W)
        h = _merge(h, flat(o_a), flat(o_b), flat(o_c), flat(o_d), w_l[:, OFF_G:], gate_b[l],
                   br_proj[l].astype(BF16), w_out[l].astype(BF16), row(ln1_g[l]), row(ln1_b[l]))
        kmem, vmem = _kv_proj(mem2, xa_wk[l].astype(BF16), xa_wv[l].astype(BF16))
        h, h_tiles = _cross_attention(h.reshape(B, S, D), kmem.reshape(B, M, D),
                                      vmem.reshape(B, M, D), xa_wq[l].astype(BF16),
                                      xa_wo[l].astype(BF16), row(ln2_g[l]), row(ln2_b[l]))
        h = _moe_layer(h.reshape(T, D), h_tiles, l, router_w[l], router_b[l], ex_w1, ex_b1[l],
                       ex_w2, ex_b2[l], ln3_g[l], ln3_b[l])
    return h.reshape(B, S, D)
```

```python
import functools
import math

import jax
import jax.numpy as jnp
from jax import lax
from jax.experimental import pallas as pl
from jax.experimental.pallas import tpu as pltpu

F32 = jnp.float32
BF16 = jnp.bfloat16

D_MODEL = 1024
DEPTH = 2
CHUNK = 64
BRANCH_W = 256
N_BRANCH = 4
CONV_W = 3
SG_BLOCK = 128
SG_GROUPS = 4
RW_HEADS = 4
RW_HEAD_DIM = 64
RW_LORA = 64
RW_GATE_LORA = 128
RW_IN = 3 * BRANCH_W + 2 * RW_LORA + RW_GATE_LORA
RW_LN_EPS = 64e-5
S5_CH = 16
S5_GROUPS = BRANCH_W // S5_CH
S5_STATE = 64
S5_N = S5_GROUPS * S5_STATE
OFF_B = 3 * BRANCH_W
OFF_C = OFF_B + 2 * BRANCH_W
OFF_D = OFF_C + RW_IN
OFF_G = OFF_D + BRANCH_W
XA_HEADS = 4
XA_HEAD_DIM = D_MODEL // XA_HEADS
N_EXPERTS = 32
TOP_K = 4
TOP_K_BITS = 2
D_FF = D_MODEL
SWIGLU_LIMIT = 7.0
SWIGLU_ALPHA = 1.702
LN_EPS = 1e-5
DN_ALPHA = (2 * DEPTH) ** 0.25

V7X_LANES = 128
V7X_SUBLANES = 8
V7X_VMEM_LIMIT_BYTES = 56 * 1024 * 1024
V7X_DMA_PRIORITIES = 2
TOKEN_TILE = 512
SEQ_TILE = 512
SLOT_MAP_CHUNK = 8192
RW_CHUNK = 64
MOE_TILE = 512
ROUTE_LANES = 128


def _cparams(*sem):
    return pltpu.CompilerParams(dimension_semantics=sem,
                                vmem_limit_bytes=V7X_VMEM_LIMIT_BYTES)


def _full(shape):
    nd = len(shape)
    return pl.BlockSpec(shape, lambda *_: (0,) * nd)


def _layer_norm(x, g, b, eps=LN_EPS):
    mu = jnp.mean(x, axis=-1, keepdims=True)
    xc = x - mu
    var = jnp.mean(xc * xc, axis=-1, keepdims=True)
    return xc * lax.rsqrt(var + eps) * g + b


def _contract(a, b, dims, **kw):
    return lax.dot_general(a, b, (dims, ((), ())), preferred_element_type=F32, **kw)


def _dot(a, b, **kw):
    return _contract(a, b, ((1,), (0,)), **kw)


def _bf16_terms(x, n):
    terms = []
    for _ in range(n):
        t = x.astype(BF16)
        terms.append(t)
        x = x - t.astype(F32)
    return terms


def _dot_split(a, b):
    ah, al = _bf16_terms(a, 2)
    bh, bl = _bf16_terms(b, 2)
    return _dot(jnp.concatenate([ah, ah, al], axis=1), jnp.concatenate([bh, bl, bh], axis=0))


def _dot_select(x, sel_stack):
    n = sel_stack.shape[0] // x.shape[1]
    return _dot(jnp.concatenate(_bf16_terms(x, n), axis=1), sel_stack)


def _dot_nt(a, b, **kw):
    return _contract(a, b, ((1,), (1,)), **kw)


def _dot_tn(a, b, **kw):
    return _contract(a, b, ((0,), (0,)), **kw)


ROW_TILE = D_MODEL // V7X_LANES


def _store_row_tiles(ref, x):
    n = x.shape[0]
    for c in range(ROW_TILE):
        ref[pl.ds(c, n, stride=ROW_TILE), :] = x[:, c * V7X_LANES:(c + 1) * V7X_LANES]


def _load_row_tiles(ref):
    n = ref.shape[0] // ROW_TILE
    return jnp.concatenate([ref[pl.ds(c, n, stride=ROW_TILE), :] for c in range(ROW_TILE)], axis=-1)


def _shift_rows(x, n, tail):
    rolled = pltpu.roll(x, n, 0)
    row = lax.broadcasted_iota(jnp.int32, x.shape, 0)
    out = rolled
    for i in range(n):
        src = tail[V7X_SUBLANES - n + i:V7X_SUBLANES - n + i + 1, :]
        out = jnp.where(row == i, src, out)
    return out


def _ln_kernel(x_ref, g_ref, b_ref, o_ref):
    o_ref[...] = _layer_norm(x_ref[...], g_ref[...], b_ref[...])


def _input_ln(x2, g, b):
    T, D = x2.shape
    tm = TOKEN_TILE * 2
    return pl.pallas_call(
        _ln_kernel,
        grid=(T // tm,),
        in_specs=[pl.BlockSpec((tm, D), lambda i: (i, 0)), _full((1, D)), _full((1, D))],
        out_specs=pl.BlockSpec((tm, D), lambda i: (i, 0)),
        out_shape=jax.ShapeDtypeStruct((T, D), F32),
        compiler_params=_cparams("parallel"),
    )(x2, g.reshape(1, D), b.reshape(1, D))


def _ab_kernel(h_ref, w_ref, cw_ref, ng_ref, nb_ref, sw_ref, sb_ref,
               oa_ref, ob_ref, tail_ref):
    @pl.when(pl.program_id(1) == 0)
    def _():
        tail_ref[...] = jnp.zeros_like(tail_ref)

    tm = h_ref.shape[0]
    W = BRANCH_W
    z = _dot(h_ref[...].astype(BF16), w_ref[...])
    ch = z[:, W:2 * W] * z[:, 2 * W:3 * W]
    tail = tail_ref[...]
    s1 = _shift_rows(ch, 1, tail)
    s2 = _shift_rows(ch, 2, tail)
    tail_ref[...] = ch[tm - V7X_SUBLANES:, :]
    cw = cw_ref[...]
    y = cw[2:3, :] * ch + cw[1:2, :] * s1 + cw[0:1, :] * s2
    oa_ref[...] = (z[:, 0:W] * y).astype(oa_ref.dtype)
    u = z[:, 3 * W:4 * W]
    v = _layer_norm(z[:, 4 * W:5 * W], ng_ref[...], nb_ref[...]).astype(BF16)
    gw = W // SG_GROUPS
    grp = lax.broadcasted_iota(jnp.int32, (SG_BLOCK, W), 1) // gw
    for blk in range(tm // SG_BLOCK):
        rows = slice(blk * SG_BLOCK, (blk + 1) * SG_BLOCK)
        vb = v[rows, :]
        sv = sb_ref[...]
        for g in range(SG_GROUPS):
            sv = sv + jnp.where(grp == g, _dot(sw_ref[g], vb), 0.0)
        ob_ref[rows, :] = (u[rows, :] * sv).astype(ob_ref.dtype)


def _mixers_ab(h, w_ab, conv_w, ng, nb, sg_wm, sg_bias):
    B, S, D = h.shape
    tm = TOKEN_TILE
    W = BRANCH_W
    blk = lambda b, j: (b, j, 0)
    return pl.pallas_call(
        _ab_kernel,
        grid=(B, S // tm),
        in_specs=[pl.BlockSpec((None, tm, D), blk),
                  _full((D, 5 * W)), _full((CONV_W, W)), _full((1, W)), _full((1, W)),
                  _full((SG_GROUPS, SG_BLOCK, SG_BLOCK)), _full((SG_BLOCK, W))],
        out_specs=[pl.BlockSpec((None, tm, W), blk), pl.BlockSpec((None, tm, W), blk)],
        out_shape=[jax.ShapeDtypeStruct((B, S, W), BF16)] * 2,
        scratch_shapes=[pltpu.VMEM((V7X_SUBLANES, W), F32)],
        compiler_params=_cparams("parallel", "arbitrary"),
    )(h, w_ab, conv_w, ng, nb, sg_wm, sg_bias)


def _softplus(x):
    return jnp.maximum(x, 0.0) + jnp.log(1.0 + jnp.exp(-jnp.abs(x)))


def _rwkv_kernel(h_ref, w_ref, mu_ref, w0_ref, a0_ref, lora_ref, gup_ref, kk_ref, ka_ref,
                 rk_ref, lng_ref, lnb_ref, ones_ref, o_ref,
                 tail_ref, state_ref, v_s, y_s, gend_s, ar_s, bk_s, inv_s, rb_s, mk_s):
    @pl.when(pl.program_id(1) == 0)
    def _():
        tail_ref[...] = jnp.zeros_like(tail_ref)
        state_ref[...] = jnp.zeros_like(state_ref)

    tm = h_ref.shape[0]
    W = BRANCH_W
    N = RW_HEAD_DIM
    L = RW_CHUNK
    z = _dot(h_ref[...].astype(BF16), w_ref[...])
    zprev = _shift_rows(z, 1, tail_ref[...])
    tail_ref[...] = z[tm - V7X_SUBLANES:, :]
    z = z + (zprev - z) * mu_ref[...]
    r = z[:, 0:W]
    k = z[:, W:2 * W]
    v = z[:, 2 * W:3 * W]
    xwa = z[:, 3 * W:3 * W + 2 * RW_LORA]
    xg = z[:, 3 * W + 2 * RW_LORA:]
    lane = lax.broadcasted_iota(jnp.int32, xwa.shape, 1)
    lora_in = jnp.where(lane < RW_LORA, jnp.tanh(xwa), xwa)
    lora = _dot_split(lora_in, lora_ref[...])
    w = -_softplus(-(w0_ref[...] + lora[:, 0:W])) - 0.5
    log_decay = -jnp.exp(w)
    a = jax.nn.sigmoid(a0_ref[...] + lora[:, W:2 * W])
    g = _dot(jax.nn.sigmoid(xg).astype(BF16), gup_ref[...])
    ones = ones_ref[...]
    kk = k * kk_ref[...]
    kk = kk / jnp.maximum(jnp.sqrt(_dot_select(kk * kk, ones)), 1e-12)
    k = k * (1.0 + (a - 1.0) * ka_ref[...])
    bonus = _dot_select(r * k * rk_ref[...], ones) * v

    ti = lax.broadcasted_iota(jnp.int32, (L, L), 0)
    tj = lax.broadcasted_iota(jnp.int32, (L, L), 1)
    tri3 = jnp.concatenate([(ti >= tj).astype(BF16)] * 3, axis=1)
    eye = (ti == tj).astype(F32)
    t2 = lax.broadcasted_iota(jnp.int32, (2 * L, L), 0)
    s2 = lax.broadcasted_iota(jnp.int32, (2 * L, L), 1)
    mask2 = ((t2 < L) & (t2 > s2)) | (t2 - L >= s2)
    n_chunks = tm // L

    alpha = -kk
    beta = kk * a
    v_s[...] = v.astype(BF16)
    for c in range(n_chunks):
        rows = slice(c * L, (c + 1) * L)
        ld = log_decay[rows, :]
        cum = _dot(tri3, jnp.concatenate(_bf16_terms(ld, 3), axis=0))
        gam = jnp.exp(cum)
        gam_inv = jnp.exp(-cum)
        gend_s[c * V7X_SUBLANES:(c + 1) * V7X_SUBLANES, :] = jnp.broadcast_to(
            gam[L - 1:L, :], (V7X_SUBLANES, W))
        ar_all = jnp.concatenate([alpha[rows, :] * jnp.exp(cum - ld), r[rows, :] * gam],
                                 axis=0).astype(BF16)
        bk_all = jnp.concatenate([beta[rows, :] * gam_inv, k[rows, :] * gam_inv],
                                 axis=0).astype(BF16)
        ar_s[c] = ar_all
        bk_s[c] = bk_all

    heads = [(c, hd) for c in range(n_chunks) for hd in range(RW_HEADS)]
    hcols = lambda hd: slice(hd * N, (hd + 1) * N)
    invs, ps = [], []
    for c, hd in heads:
        pair = _dot_nt(ar_s[c, :, hcols(hd)], bk_s[c, :, hcols(hd)])
        m_b = jnp.where(mask2, pair[:, :L], 0.0)
        mk_s[c * RW_HEADS + hd] = jnp.where(mask2, pair[:, L:], 0.0).astype(BF16)
        rb_s[c * RW_HEADS + hd] = m_b[L:].astype(BF16)
        invs.append(eye + m_b[:L])
        ps.append(m_b[:L].astype(BF16))
    ps = [_dot(pb, pb) for pb in ps]
    for _ in range(int(math.log2(L)) - 2):
        both = [_dot(jnp.concatenate([p.astype(BF16), inv.astype(BF16)], axis=0), p.astype(BF16))
                for p, inv in zip(ps, invs)]
        ps = [b2[:L] for b2 in both]
        invs = [inv + b2[L:] for inv, b2 in zip(invs, both)]
    for i, (inv, p) in enumerate(zip(invs, ps)):
        inv_s[i] = (inv + _dot(inv.astype(BF16), p.astype(BF16))).astype(BF16)

    def chunk(c, carry):
        rows = pl.ds(pl.multiple_of(c * L, L), L)
        ar_all = ar_s[c]
        bk_all = bk_s[c]
        v_all = v_s[rows, :]
        gend = gend_s[pl.ds(pl.multiple_of(c * V7X_SUBLANES, V7X_SUBLANES), 1), :]
        hs = range(RW_HEADS)
        sts = [state_ref[hd] for hd in hs]
        bases = [_dot_nt(ar_all[:, hcols(hd)], sts[hd].astype(BF16))
                 + _dot(mk_s[c * RW_HEADS + hd], v_all[:, hcols(hd)]) for hd in hs]
        us = [_dot(inv_s[c * RW_HEADS + hd], bases[hd][:L].astype(BF16)).astype(BF16) for hd in hs]
        for hd in hs:
            y_s[rows, hcols(hd)] = bases[hd][L:] + _dot(rb_s[c * RW_HEADS + hd], us[hd])
        for hd in hs:
            st = sts[hd] + _dot_tn(jnp.concatenate([us[hd], v_all[:, hcols(hd)]], axis=0),
                                   bk_all[:, hcols(hd)])
            state_ref[hd] = st * gend[:, hcols(hd)]
        return carry

    lax.fori_loop(0, n_chunks, chunk, 0)

    y = y_s[...]
    inv_n = 1.0 / N
    m = _dot_select(y, ones) * inv_n
    yc = y - m
    var = _dot_select(yc * yc, ones) * inv_n
    yn = yc * lax.rsqrt(var + RW_LN_EPS) * lng_ref[...] + lnb_ref[...]
    o_ref[...] = ((yn + bonus) * g).astype(o_ref.dtype)


def _mixer_rwkv(h, w_c, mu, w0, a0, lora_w, g_up, k_k, k_a, r_k, ln_g, ln_b, head_ones):
    B, S, D = h.shape
    tm = SEQ_TILE
    W = BRANCH_W
    L = RW_CHUNK
    nc = tm // L
    blk = lambda b, j: (b, j, 0)
    row = lambda n: _full((1, n))
    return pl.pallas_call(
        _rwkv_kernel,
        grid=(B, S // tm),
        in_specs=[pl.BlockSpec((None, tm, D), blk), _full((D, RW_IN)), row(RW_IN), row(W), row(W),
                  _full((2 * RW_LORA, 2 * W)), _full((RW_GATE_LORA, W)), row(W), row(W), row(W),
                  row(W), row(W), _full(head_ones.shape)],
        out_specs=pl.BlockSpec((None, tm, W), blk),
        out_shape=jax.ShapeDtypeStruct((B, S, W), BF16),
        scratch_shapes=[pltpu.VMEM((V7X_SUBLANES, RW_IN), F32),
                        pltpu.VMEM((RW_HEADS, RW_HEAD_DIM, RW_HEAD_DIM), F32),
                        pltpu.VMEM((tm, W), BF16),
                        pltpu.VMEM((tm, W), F32),
                        pltpu.VMEM((nc * V7X_SUBLANES, W), F32),
                        pltpu.VMEM((nc, 2 * L, W), BF16),
                        pltpu.VMEM((nc, 2 * L, W), BF16),
                        pltpu.VMEM((nc * RW_HEADS, L, L), BF16),
                        pltpu.VMEM((nc * RW_HEADS, L, L), BF16),
                        pltpu.VMEM((nc * RW_HEADS, 2 * L, L), BF16)],
        compiler_params=_cparams("parallel", "arbitrary"),
    )(h, w_c, mu, w0, a0, lora_w, g_up, k_k, k_a, r_k, ln_g, ln_b, head_ones)


def _s5_kernel(h_ref, w_ref, bre_ref, bim_ref, are_ref, aim_ref, cre_ref, cim_ref,
               d_ref, gw_ref, gb_ref, o_ref, xr_s, xi_s, st_s):
    @pl.when(pl.program_id(1) == 0)
    def _():
        st_s[...] = jnp.zeros_like(st_s)

    tm = h_ref.shape[0]
    u = _dot(h_ref[...].astype(BF16), w_ref[...])
    ub = u.astype(BF16)
    xr_s[...] = _dot(ub, bre_ref[...])
    xi_s[...] = _dot(ub, bim_ref[...])
    ar = are_ref[...]
    ai = aim_ref[...]

    def step(t, carry):
        xr, xi = carry
        row = pl.ds(t, 1)
        nr = ar * xr - ai * xi + xr_s[row, :]
        ni = ar * xi + ai * xr + xi_s[row, :]
        xr_s[row, :] = nr
        xi_s[row, :] = ni
        return nr, ni

    xr, xi = lax.fori_loop(0, tm, step, (st_s[0:1, :], st_s[1:2, :]), unroll=8)
    st_s[0:1, :] = xr
    st_s[1:2, :] = xi
    y = (_dot(xr_s[...].astype(BF16), cre_ref[...]) - _dot(xi_s[...].astype(BF16), cim_ref[...])
         + d_ref[...] * u)
    y = jax.nn.gelu(y)
    gate = jax.nn.sigmoid(_dot(y.astype(BF16), gw_ref[...]) + gb_ref[...])
    o_ref[...] = (y * gate).astype(o_ref.dtype)


def _mixer_s5(h, w_d, bre, bim, are, aim, cre, cim, d, glu_w, glu_b):
    B, S, D = h.shape
    tm = TOKEN_TILE
    W = BRANCH_W
    blk = lambda b, j: (b, j, 0)
    return pl.pallas_call(
        _s5_kernel,
        grid=(B, S // tm),
        in_specs=[pl.BlockSpec((None, tm, D), blk), _full((D, W)),
                  _full((W, S5_N)), _full((W, S5_N)), _full(are.shape), _full(aim.shape),
                  _full((S5_N, W)), _full((S5_N, W)), _full((1, W)), _full((W, W)), _full((1, W))],
        out_specs=pl.BlockSpec((None, tm, W), blk),
        out_shape=jax.ShapeDtypeStruct((B, S, W), BF16),
        scratch_shapes=[pltpu.VMEM((tm, S5_N), F32), pltpu.VMEM((tm, S5_N), F32),
                        pltpu.VMEM((V7X_SUBLANES, S5_N), F32)],
        compiler_params=_cparams("parallel", "arbitrary"),
    )(h, w_d, bre, bim, are, aim, cre, cim, d, glu_w, glu_b)


def _merge_kernel(h_ref, oa_ref, ob_ref, oc_ref, od_ref, wg_ref, gb_ref, br_ref, wo_ref,
                  g_ref, b_ref, o_ref):
    D = D_MODEL
    h = h_ref[...]
    hb = h.astype(BF16)
    merged = None
    for i, br_in in enumerate((oa_ref, ob_ref, oc_ref, od_ref)):
        gate = jax.nn.sigmoid(_dot(hb, wg_ref[:, i * D:(i + 1) * D]) + gb_ref[i:i + 1, :])
        term = gate * _dot(br_in[...], br_ref[i])
        merged = term if merged is None else merged + term
    y = _dot(merged.astype(BF16), wo_ref[...])
    o_ref[...] = _layer_norm(DN_ALPHA * h + y, g_ref[...], b_ref[...])


def _merge(h2, oa, ob, oc, od, w_g, gate_b, br_proj, w_out, g, b):
    T, D = h2.shape
    tm = TOKEN_TILE
    W = BRANCH_W
    tok = lambda n: pl.BlockSpec((tm, n), lambda i: (i, 0))
    return pl.pallas_call(
        _merge_kernel,
        grid=(T // tm,),
        in_specs=[tok(D), tok(W), tok(W), tok(W), tok(W),
                  _full((D, N_BRANCH * D)), _full((N_BRANCH, D)), _full((N_BRANCH, W, D)),
                  _full((D, D)), _full((1, D)), _full((1, D))],
        out_specs=tok(D),
        out_shape=jax.ShapeDtypeStruct((T, D), F32),
        compiler_params=_cparams("parallel"),
    )(h2, oa, ob, oc, od, w_g, gate_b, br_proj, w_out, g, b)


def _kv_kernel(m_ref, wk_ref, wv_ref, k_ref, v_ref):
    mb = m_ref[...].astype(BF16)
    k_ref[...] = _dot(mb, wk_ref[...]).astype(k_ref.dtype)
    v_ref[...] = _dot(mb, wv_ref[...]).astype(v_ref.dtype)


def _kv_proj(mem2, wk, wv):
    R, D = mem2.shape
    tm = TOKEN_TILE
    tok = pl.BlockSpec((tm, D), lambda i: (i, 0))
    return pl.pallas_call(
        _kv_kernel,
        grid=(R // tm,),
        in_specs=[tok, _full((D, D)), _full((D, D))],
        out_specs=[tok, tok],
        out_shape=[jax.ShapeDtypeStruct((R, D), BF16)] * 2,
        compiler_params=_cparams("parallel"),
    )(mem2, wk, wv)


def _attn_kernel(h_ref, wq_ref, k_ref, v_ref, wo_ref, g_ref, b_ref, o_ref, ot_ref):
    h = h_ref[...]
    q = _dot(h.astype(BF16), wq_ref[...]).astype(BF16)
    scale = XA_HEAD_DIM ** -0.5
    outs = []
    for hd in range(XA_HEADS):
        cols = slice(hd * XA_HEAD_DIM, (hd + 1) * XA_HEAD_DIM)
        s = _dot_nt(q[:, cols], k_ref[:, cols]) * scale
        s = s - jnp.max(s, axis=-1, keepdims=True)
        p = jnp.exp(s)
        p = p / jnp.sum(p, axis=-1, keepdims=True)
        outs.append(_dot(p.astype(BF16), v_ref[:, cols]))
    o = jnp.concatenate(outs, axis=-1).astype(BF16)
    y = _dot(o, wo_ref[...])
    out = _layer_norm(DN_ALPHA * h + y, g_ref[...], b_ref[...])
    o_ref[...] = out
    _store_row_tiles(ot_ref, out)


def _cross_attention(h, kmem, vmem, wq, wo, g, b):
    B, S, D = h.shape
    M = kmem.shape[1]
    tm = TOKEN_TILE
    n = S // tm
    blk = lambda bi, j: (bi, j, 0)
    mem = lambda bi, j: (bi, 0, 0)
    return pl.pallas_call(
        _attn_kernel,
        grid=(B, n),
        in_specs=[pl.BlockSpec((None, tm, D), blk), _full((D, D)),
                  pl.BlockSpec((None, M, D), mem), pl.BlockSpec((None, M, D), mem),
                  _full((D, D)), _full((1, D)), _full((1, D))],
        out_specs=[pl.BlockSpec((None, tm, D), blk),
                   pl.BlockSpec((tm * ROW_TILE, V7X_LANES), lambda bi, j: (bi * n + j, 0))],
        out_shape=[jax.ShapeDtypeStruct((B, S, D), F32),
                   jax.ShapeDtypeStruct((B * S * ROW_TILE, V7X_LANES), F32)],
        compiler_params=_cparams("parallel", "parallel"),
    )(h, wq, kmem, vmem, wo, g, b)


def _router_kernel(h_ref, w_ref, b_ref, route_ref, rw_ref, cnt_ref, carry_ref):
    @pl.when(pl.program_id(0) == 0)
    def _():
        carry_ref[...] = jnp.zeros_like(carry_ref)

    tm = h_ref.shape[0]
    E = N_EXPERTS
    logits = _dot_split(h_ref[...], w_ref[...]) + b_ref[...]
    lane = lax.broadcasted_iota(jnp.int32, (tm, E), 1)
    work = logits
    vals, hots, idxs = [], [], []
    for _ in range(TOP_K):
        m = jnp.max(work, axis=-1, keepdims=True)
        idx = jnp.min(jnp.where(work == m, lane, E), axis=-1, keepdims=True)
        hot = lane == idx
        work = jnp.where(hot, -jnp.inf, work)
        vals.append(m)
        hots.append(hot)
        idxs.append(idx)
    exps = [jnp.exp(v - vals[0]) for v in vals]
    denom = exps[0] + exps[1] + exps[2] + exps[3]
    multi = (hots[0] | hots[1] | hots[2] | hots[3]).astype(F32)
    ti = lax.broadcasted_iota(jnp.int32, (tm, tm), 0)
    tj = lax.broadcasted_iota(jnp.int32, (tm, tm), 1)
    before = (ti > tj).astype(BF16)
    prefix = _dot(before, multi.astype(BF16)) + carry_ref[0:1, :]
    out_lane = lax.broadcasted_iota(jnp.int32, (tm, ROUTE_LANES), 1)
    route = jnp.zeros((tm, ROUTE_LANES), jnp.int32)
    rw = jnp.zeros((tm, ROUTE_LANES), F32)
    for kk in range(TOP_K):
        rank = jnp.sum(jnp.where(hots[kk], prefix, 0.0), axis=-1, keepdims=True).astype(jnp.int32)
        route = jnp.where(out_lane == kk, idxs[kk], route)
        route = jnp.where(out_lane == TOP_K + kk, rank, route)
        rw = jnp.where(out_lane == kk, exps[kk] / denom, rw)
    route_ref[...] = route
    rw_ref[...] = rw
    carry_ref[0:1, :] = carry_ref[0:1, :] + jnp.sum(multi, axis=0, keepdims=True)
    cnt_ref[...] = carry_ref[...]


def _router(h2, router_w, router_b):
    T, D = h2.shape
    tm = TOKEN_TILE
    E = N_EXPERTS
    return pl.pallas_call(
        _router_kernel,
        grid=(T // tm,),
        in_specs=[pl.BlockSpec((tm, D), lambda i: (i, 0)), _full((D, E)), _full((1, E))],
        out_specs=[pl.BlockSpec((tm, ROUTE_LANES), lambda i: (i, 0)),
                   pl.BlockSpec((tm, ROUTE_LANES), lambda i: (i, 0)),
                   _full((V7X_SUBLANES, E))],
        out_shape=[jax.ShapeDtypeStruct((T, ROUTE_LANES), jnp.int32),
                   jax.ShapeDtypeStruct((T, ROUTE_LANES), F32),
                   jax.ShapeDtypeStruct((V7X_SUBLANES, E), F32)],
        scratch_shapes=[pltpu.VMEM((V7X_SUBLANES, E), F32)],
        compiler_params=_cparams("arbitrary"),
    )(h2, router_w, router_b)


def _dest_kernel(route_ref, start_ref, dest_ref):
    route = route_ref[...]
    tm = route.shape[0]
    lane = lax.broadcasted_iota(jnp.int32, (tm, N_EXPERTS), 1)
    out_lane = lax.broadcasted_iota(jnp.int32, (tm, ROUTE_LANES), 1)
    dest = jnp.zeros((tm, ROUTE_LANES), jnp.int32)
    start = start_ref[...]
    for kk in range(TOP_K):
        idx = route[:, kk:kk + 1]
        base = jnp.sum(jnp.where(lane == idx, start, 0), axis=-1, keepdims=True)
        dest = jnp.where(out_lane == kk, base + route[:, TOP_K + kk:TOP_K + kk + 1], dest)
    dest_ref[...] = dest


def _dest_slots(route, pad_start):
    T = route.shape[0]
    tm = TOKEN_TILE * 2
    return pl.pallas_call(
        _dest_kernel,
        grid=(T // tm,),
        in_specs=[pl.BlockSpec((tm, ROUTE_LANES), lambda i: (i, 0)), _full((1, N_EXPERTS))],
        out_specs=pl.BlockSpec((tm, ROUTE_LANES), lambda i: (i, 0)),
        out_shape=jax.ShapeDtypeStruct((T, ROUTE_LANES), jnp.int32),
        compiler_params=_cparams("parallel"),
    )(route, pad_start.reshape(1, N_EXPERTS))


def _slot_source_kernel(fill_lo_ref, fill_hi_ref, dest_hbm, row_ref, dest_s, sem, *, n_tok):
    i = pl.program_id(0)
    n = dest_s.shape[0]

    @pl.when(i == 0)
    def _():
        def fill(s, carry):
            parity = (s // MOE_TILE + 1) % 2
            row_ref[s] = n_tok * TOP_K + parity * MOE_TILE + s % MOE_TILE
            return carry

        lax.fori_loop(0, MOE_TILE, fill, 0)
        for e in range(fill_lo_ref.shape[0]):
            lax.fori_loop(MOE_TILE + fill_lo_ref[e], MOE_TILE + fill_hi_ref[e], fill, 0)

    idx_copy = pltpu.make_async_copy(dest_hbm.at[i], dest_s, sem)
    idx_copy.start()
    idx_copy.wait()
    slots = row_ref.at[pl.ds(MOE_TILE, row_ref.shape[0] - MOE_TILE)]
    first_tok = i * (n // TOP_K)

    def body(t, carry):
        for kk in range(TOP_K):
            slots[dest_s[t * TOP_K + kk]] = (kk * n_tok + first_tok) + t
        return carry

    lax.fori_loop(0, n // TOP_K, body, 0, unroll=8)


def _slot_sources(fill_lo, fill_hi, dest_tiles, n_slots):
    n_tok_tiles, n = dest_tiles.shape
    return pl.pallas_call(
        functools.partial(_slot_source_kernel, n_tok=n_tok_tiles * n // TOP_K),
        grid_spec=pltpu.PrefetchScalarGridSpec(
            num_scalar_prefetch=2,
            grid=(n_tok_tiles,),
            in_specs=[pl.BlockSpec(memory_space=pl.ANY)],
            out_specs=pl.BlockSpec(memory_space=pltpu.SMEM),
            scratch_shapes=[pltpu.SMEM((n,), jnp.int32), pltpu.SemaphoreType.DMA]),
        out_shape=jax.ShapeDtypeStruct((MOE_TILE + n_slots,), jnp.int32),
        compiler_params=_cparams("arbitrary"),
    )(fill_lo, fill_hi, dest_tiles)


GU_BLOCK = 2 * V7X_LANES


def _unzip_kernel(w_ref, p_ref, o_ref):
    for c in range(w_ref.shape[1] // GU_BLOCK):
        cols = slice(c * GU_BLOCK, (c + 1) * GU_BLOCK)
        o_ref[:, cols] = _dot(w_ref[:, cols].astype(BF16), p_ref[...]).astype(o_ref.dtype)


def _unzip_gate_up(w1_layers, layer):
    _, E, D, F2 = w1_layers.shape
    tr = TOKEN_TILE
    j = jnp.arange(GU_BLOCK)
    perm = (j[:, None] == jnp.where(j < V7X_LANES, 2 * j, 2 * (j - V7X_LANES) + 1)[None, :])
    return pl.pallas_call(
        _unzip_kernel,
        grid=(E, D // tr),
        in_specs=[pl.BlockSpec((None, None, tr, F2), lambda e, r: (layer, e, r, 0)),
                  _full((GU_BLOCK, GU_BLOCK))],
        out_specs=pl.BlockSpec((None, tr, F2), lambda e, r: (e, r, 0)),
        out_shape=jax.ShapeDtypeStruct((E, D, F2), BF16),
        compiler_params=_cparams("parallel", "parallel"),
    )(w1_layers, perm.astype(BF16))


def _unzip_bias(b1):
    E, F2 = b1.shape
    return b1.reshape(E, F2 // GU_BLOCK, V7X_LANES, 2).transpose(0, 1, 3, 2).reshape(E, 1, F2)


def _expert_ffn(x, w1_ref, b1_ref, w2_ref, b2_ref):
    hid = _dot(x, w1_ref[...]) + b1_ref[...]
    acts = []
    for c in range(hid.shape[1] // GU_BLOCK):
        gate = jnp.minimum(hid[:, c * GU_BLOCK:c * GU_BLOCK + V7X_LANES], SWIGLU_LIMIT)
        up = jnp.clip(hid[:, c * GU_BLOCK + V7X_LANES:(c + 1) * GU_BLOCK],
                      -SWIGLU_LIMIT, SWIGLU_LIMIT)
        acts.append(((up + 1.0) * gate * jax.nn.sigmoid(SWIGLU_ALPHA * gate)).astype(BF16))
    return _dot(jnp.concatenate(acts, axis=-1), w2_ref[...].astype(BF16)) + b2_ref[...]


def _expert_kernel(blk_e_ref, n_used_ref, src_ref, h_hbm, w1_ref, b1_ref, w2_ref, b2_ref, yk_hbm,
                   xbuf0, xbuf1, ybuf0, ybuf1, gsem, ssem):
    del blk_e_ref
    i = pl.program_id(0)
    n_used = n_used_ref[0]
    xbufs = (xbuf0, xbuf1)
    ybufs = (ybuf0, ybuf1)
    R = ROW_TILE
    tm = xbuf0.shape[0] // R
    T = h_hbm.shape[0] // R
    n_assign = T * TOP_K

    def tile_rows(r):
        return pl.ds(r * R if isinstance(r, int) else pl.multiple_of(r * R, R), R)

    def queue_of(j):
        return j % V7X_DMA_PRIORITIES if isinstance(j, int) else 0

    def gather_row(tile, b, j):
        row = src_ref[(tile + 1) * tm + j]
        tok = row & (T - 1) if T & (T - 1) == 0 else lax.rem(row, T)
        pltpu.make_async_copy(h_hbm.at[tile_rows(tok)], xbufs[b].at[tile_rows(j)],
                              gsem.at[b]).start(priority=queue_of(j))

    def scatter_row(tile, b, j):
        row = src_ref[(tile + 1) * tm + j]
        pltpu.make_async_copy(ybufs[b].at[tile_rows(j)], yk_hbm.at[tile_rows(row)],
                              ssem.at[b]).start(priority=queue_of(j))

    def wait_gather(b):
        pltpu.make_async_copy(h_hbm.at[pl.ds(0, tm * R)], xbufs[b], gsem.at[b]).wait()

    def wait_scatter(b):
        pltpu.make_async_copy(ybufs[b], yk_hbm.at[pl.ds(0, tm * R)], ssem.at[b]).wait()

    def rows_loop(fn):
        def body(j, carry):
            fn(j)
            return carry
        lax.fori_loop(0, tm, body, 0)

    @pl.when(i == 0)
    def _():
        rows_loop(lambda j: gather_row(0, 0, j))
        ybuf0[...] = jnp.zeros_like(ybuf0)
        ybuf1[...] = jnp.zeros_like(ybuf1)
        pltpu.make_async_copy(ybuf0, yk_hbm.at[pl.ds(n_assign * R, tm * R)], ssem.at[0]).start()

    for b in range(2):
        @pl.when((i < n_used) & (i % 2 == b))
        def _(b=b):
            wait_gather(b)
            wait_scatter(b)
            for j in range(tm):
                gather_row(i + 1, 1 - b, j)
                scatter_row(i - 1, 1 - b, j)
            x = _load_row_tiles(xbufs[b]).astype(BF16)
            _store_row_tiles(ybufs[b], _expert_ffn(x, w1_ref, b1_ref, w2_ref, b2_ref))

        @pl.when((i == n_used) & (i % 2 == b))
        def _(b=b):
            wait_gather(b)
            wait_scatter(b)
            rows_loop(lambda j: scatter_row(i - 1, 1 - b, j))
            wait_scatter(1 - b)


def _experts(blk_e, n_used, src, h_tiles, w1, b1, w2_layers, layer, b2):
    R = ROW_TILE
    T = h_tiles.shape[0] // R
    D = D_MODEL
    tm = MOE_TILE
    F = D_FF
    n_tiles = src.shape[0] // tm - 1
    per_e = lambda shape: pl.BlockSpec((None,) + shape, lambda i, be, nu, sr: (be[i], 0, 0))
    w2_spec = pl.BlockSpec((None, None, F, D), lambda i, be, nu, sr: (layer, be[i], 0, 0))
    return pl.pallas_call(
        _expert_kernel,
        grid_spec=pltpu.PrefetchScalarGridSpec(
            num_scalar_prefetch=3,
            grid=(n_tiles,),
            in_specs=[pl.BlockSpec(memory_space=pl.ANY),
                      per_e((D, 2 * F)), per_e((1, 2 * F)), w2_spec, per_e((1, D))],
            out_specs=pl.BlockSpec(memory_space=pl.ANY),
            scratch_shapes=[pltpu.VMEM((tm * R, V7X_LANES), F32)] * 4
                           + [pltpu.SemaphoreType.DMA((2,)), pltpu.SemaphoreType.DMA((2,))]),
        out_shape=jax.ShapeDtypeStruct(((T * TOP_K + 2 * tm) * R, V7X_LANES), F32),
        compiler_params=_cparams("arbitrary"),
    )(blk_e, n_used, src, h_tiles, w1, b1, w2_layers, b2)


def _combine_kernel(y0_ref, y1_ref, y2_ref, y3_ref, h_ref, rw_ref, g_ref, b_ref, o_ref):
    rw = rw_ref[...]
    moe = rw[:, 0:1] * _load_row_tiles(y0_ref)
    for kk, y_ref in enumerate((y1_ref, y2_ref, y3_ref), start=1):
        moe = moe + rw[:, kk:kk + 1] * _load_row_tiles(y_ref)
    o_ref[...] = _layer_norm(DN_ALPHA * h_ref[...] + moe, g_ref[...], b_ref[...])


def _combine(yk, h2, rw, g, b):
    T, D = h2.shape
    tm = TOKEN_TILE
    n = T // tm
    choice = lambda kk: pl.BlockSpec((tm * ROW_TILE, V7X_LANES), lambda i: (kk * n + i, 0))
    return pl.pallas_call(
        _combine_kernel,
        grid=(n,),
        in_specs=[choice(kk) for kk in range(TOP_K)]
                 + [pl.BlockSpec((tm, D), lambda i: (i, 0)),
                    pl.BlockSpec((tm, ROUTE_LANES), lambda i: (i, 0)),
                    _full((1, D)), _full((1, D))],
        out_specs=pl.BlockSpec((tm, D), lambda i: (i, 0)),
        out_shape=jax.ShapeDtypeStruct((T, D), F32),
        compiler_params=_cparams("parallel"),
    )(yk, yk, yk, yk, h2, rw, g, b)


def _s5_discretise(a_re, a_im, b_re, b_im, c_re, c_im, log_dt):
    G, P, C = S5_GROUPS, S5_STATE, S5_CH
    dt = jnp.exp(log_dt)[:, None]
    mag = jnp.exp(a_re * dt)
    abar_re = mag * jnp.cos(a_im * dt)
    abar_im = mag * jnp.sin(a_im * dt)
    den = a_re * a_re + a_im * a_im
    num_re = abar_re - 1.0
    coef_re = (num_re * a_re + abar_im * a_im) / den
    coef_im = (abar_im * a_re - num_re * a_im) / den
    bbar_re = coef_re[..., None] * b_re - coef_im[..., None] * b_im
    bbar_im = coef_re[..., None] * b_im + coef_im[..., None] * b_re
    eye = jnp.eye(G, dtype=F32)
    to_in = lambda m: jnp.einsum('gpc,gh->gchp', m, eye).reshape(G * C, G * P)
    to_out = lambda m: jnp.einsum('gcp,gh->gphc', m, eye).reshape(G * P, G * C)
    return (to_in(bbar_re).astype(BF16), to_in(bbar_im).astype(BF16),
            abar_re.reshape(1, G * P), abar_im.reshape(1, G * P),
            to_out(c_re).astype(BF16), to_out(c_im).astype(BF16))


def _moe_layer(h2, h_tiles, layer, router_w, router_b, w1_layers, b1, w2_layers, b2, g, b):
    T, D = h2.shape
    E = N_EXPERTS
    tm = TOKEN_TILE
    route, rw, counts = _router(h2, router_w, router_b.reshape(1, E))
    counts = counts[0].astype(jnp.int32)
    padded = (counts + MOE_TILE - 1) // MOE_TILE * MOE_TILE
    pad_end = jnp.cumsum(padded)
    pad_start = pad_end - padded
    n_tiles = T * TOP_K // MOE_TILE + E
    n_slots = n_tiles * MOE_TILE
    tile_start = jnp.arange(n_tiles, dtype=jnp.int32) * MOE_TILE
    blk_e = jnp.minimum(jnp.sum(tile_start[:, None] >= pad_end[None, :], axis=1), E - 1).astype(jnp.int32)
    n_used = (pad_end[E - 1:] // MOE_TILE).astype(jnp.int32)
    dest = _dest_slots(route, pad_start.astype(jnp.int32))
    chunk = min(SLOT_MAP_CHUNK, T * TOP_K)
    dest_tiles = dest[:, :TOP_K].reshape(T * TOP_K // chunk, chunk)
    fill_lo = jnp.concatenate([pad_start + counts, pad_end[E - 1:]]).astype(jnp.int32)
    fill_hi = jnp.concatenate([pad_end, jnp.full((1,), n_slots)]).astype(jnp.int32)
    src = _slot_sources(fill_lo, fill_hi, dest_tiles, n_slots)
    yk = _experts(blk_e, n_used, src, h_tiles, _unzip_gate_up(w1_layers, layer), _unzip_bias(b1),
                  w2_layers, layer, b2[:, None, :])
    return _combine(yk, h2, rw, g.reshape(1, D), b.reshape(1, D))


def kernel(x, mem, ln_in_g, ln_in_b, w_in, conv_w, sg_norm_g, sg_norm_b, sg_w, sg_b, rw_mu, rw_w0, rw_w_up, rw_a0, rw_a_up, rw_g_up, rw_k_k, rw_k_a, rw_r_k, rw_ln_g, rw_ln_b, s5_a_re, s5_a_im, s5_b_re, s5_b_im, s5_c_re, s5_c_im, s5_d, s5_log_dt, s5_glu_w, s5_glu_b, br_proj, gate_b, w_out, ln1_g, ln1_b, xa_wq, xa_wk, xa_wv, xa_wo, ln2_g, ln2_b, router_w, router_b, ex_w1, ex_b1, ex_w2, ex_b2, ln3_g, ln3_b):
    B, S, D = x.shape
    M = mem.shape[1]
    T = B * S
    W = BRANCH_W
    row = lambda v: v.reshape(1, -1)
    pos = jnp.arange(SG_BLOCK)
    sg_mask = (pos[None, :] // CHUNK) <= (pos[:, None] // CHUNK)
    head_ones = jnp.kron(jnp.eye(RW_HEADS, dtype=F32), jnp.ones((RW_HEAD_DIM, RW_HEAD_DIM), F32))
    head_ones = jnp.concatenate([head_ones, head_ones], axis=0).astype(BF16)
    mem2 = mem.reshape(B * M, D)

    h = _input_ln(x.reshape(T, D), ln_in_g, ln_in_b)
    for l in range(DEPTH):
        w_l = w_in[l].astype(BF16)
        h3 = h.reshape(B, S, D)
        sg_wm = jnp.where(sg_mask[None], sg_w[l], 0.0).astype(BF16)
        sg_bias = jnp.repeat(sg_b[l].T, W // SG_GROUPS, axis=1)
        o_a, o_b = _mixers_ab(h3, w_l[:, :OFF_C], conv_w[l], row(sg_norm_g[l]), row(sg_norm_b[l]),
                              sg_wm, sg_bias)
        zero = jnp.zeros((RW_LORA, W), F32)
        lora_w = jnp.concatenate([jnp.concatenate([rw_w_up[l], zero], axis=1),
                                  jnp.concatenate([zero, rw_a_up[l]], axis=1)], axis=0)
        o_c = _mixer_rwkv(h3, w_l[:, OFF_C:OFF_D], row(rw_mu[l]), row(rw_w0[l]), row(rw_a0[l]),
                          lora_w, rw_g_up[l].astype(BF16), row(rw_k_k[l]), row(rw_k_a[l]),
                          row(rw_r_k[l]),
                          row(rw_ln_g[l]), row(rw_ln_b[l]), head_ones)
        s5p = _s5_discretise(s5_a_re[l], s5_a_im[l], s5_b_re[l], s5_b_im[l], s5_c_re[l],
                             s5_c_im[l], s5_log_dt[l])
        o_d = _mixer_s5(h3, w_l[:, OFF_D:OFF_G], *s5p, row(s5_d[l]),
                        s5_glu_w[l].astype(BF16), row(s5_glu_b[l]))
        flat = lambda o: o.reshape(T, W)
        h = _merge(h, flat(o_a), flat(o_b), flat(o_c), flat(o_d), w_l[:, OFF_G:], gate_b[l],
                   br_proj[l].astype(BF16), w_out[l].astype(BF16), row(ln1_g[l]), row(ln1_b[l]))
        kmem, vmem = _kv_proj(mem2, xa_wk[l].astype(BF16), xa_wv[l].astype(BF16))
        h, h_tiles = _cross_attention(h.reshape(B, S, D), kmem.reshape(B, M, D),
                                      vmem.reshape(B, M, D), xa_wq[l].astype(BF16),
                                      xa_wo[l].astype(BF16), row(ln2_g[l]), row(ln2_b[l]))
        h = _moe_layer(h.reshape(T, D), h_tiles, l, router_w[l], router_b[l], ex_w1, ex_b1[l],
                       ex_w2, ex_b2[l], ln3_g[l], ln3_b[l])
    return h.reshape(B, S, D)
```

```python
import functools
import math

import jax
import jax.numpy as jnp
from jax import lax
from jax.experimental import pallas as pl
from jax.experimental.pallas import tpu as pltpu

F32 = jnp.float32
BF16 = jnp.bfloat16

D_MODEL = 1024
DEPTH = 2
CHUNK = 64
BRANCH_W = 256
N_BRANCH = 4
CONV_W = 3
SG_BLOCK = 128
SG_GROUPS = 4
RW_HEADS = 4
RW_HEAD_DIM = 64
RW_LORA = 64
RW_GATE_LORA = 128
RW_IN = 3 * BRANCH_W + 2 * RW_LORA + RW_GATE_LORA
RW_LN_EPS = 64e-5
S5_CH = 16
S5_GROUPS = BRANCH_W // S5_CH
S5_STATE = 64
S5_N = S5_GROUPS * S5_STATE
OFF_B = 3 * BRANCH_W
OFF_C = OFF_B + 2 * BRANCH_W
OFF_D = OFF_C + RW_IN
OFF_G = OFF_D + BRANCH_W
XA_HEADS = 4
XA_HEAD_DIM = D_MODEL // XA_HEADS
N_EXPERTS = 32
TOP_K = 4
TOP_K_BITS = 2
D_FF = D_MODEL
SWIGLU_LIMIT = 7.0
SWIGLU_ALPHA = 1.702
LN_EPS = 1e-5
DN_ALPHA = (2 * DEPTH) ** 0.25

V7X_LANES = 128
V7X_SUBLANES = 8
V7X_VMEM_LIMIT_BYTES = 56 * 1024 * 1024
V7X_DMA_PRIORITIES = 2
TOKEN_TILE = 512
SEQ_TILE = 512
RW_CHUNK = 64
MOE_TILE = 512
ROUTE_LANES = 128


def _cparams(*sem):
    return pltpu.CompilerParams(dimension_semantics=sem,
                                vmem_limit_bytes=V7X_VMEM_LIMIT_BYTES)


def _full(shape):
    nd = len(shape)
    return pl.BlockSpec(shape, lambda *_: (0,) * nd)


def _layer_norm(x, g, b, eps=LN_EPS):
    mu = jnp.mean(x, axis=-1, keepdims=True)
    xc = x - mu
    var = jnp.mean(xc * xc, axis=-1, keepdims=True)
    return xc * lax.rsqrt(var + eps) * g + b


def _contract(a, b, dims, **kw):
    return lax.dot_general(a, b, (dims, ((), ())), preferred_element_type=F32, **kw)


def _dot(a, b, **kw):
    return _contract(a, b, ((1,), (0,)), **kw)


def _bf16_terms(x, n):
    terms = []
    for _ in range(n):
        t = x.astype(BF16)
        terms.append(t)
        x = x - t.astype(F32)
    return terms


def _dot_split(a, b):
    ah, al = _bf16_terms(a, 2)
    bh, bl = _bf16_terms(b, 2)
    return _dot(jnp.concatenate([ah, ah, al], axis=1), jnp.concatenate([bh, bl, bh], axis=0))


def _dot_select(x, sel_stack):
    n = sel_stack.shape[0] // x.shape[1]
    return _dot(jnp.concatenate(_bf16_terms(x, n), axis=1), sel_stack)


def _dot_nt(a, b, **kw):
    return _contract(a, b, ((1,), (1,)), **kw)


def _dot_tn(a, b, **kw):
    return _contract(a, b, ((0,), (0,)), **kw)


ROW_TILE = D_MODEL // V7X_LANES


def _store_row_tiles(ref, x):
    n = x.shape[0]
    for c in range(ROW_TILE):
        ref[pl.ds(c, n, stride=ROW_TILE), :] = x[:, c * V7X_LANES:(c + 1) * V7X_LANES]


def _load_row_tiles(ref):
    n = ref.shape[0] // ROW_TILE
    return jnp.concatenate([ref[pl.ds(c, n, stride=ROW_TILE), :] for c in range(ROW_TILE)], axis=-1)


def _shift_rows(x, n, tail):
    rolled = pltpu.roll(x, n, 0)
    row = lax.broadcasted_iota(jnp.int32, x.shape, 0)
    out = rolled
    for i in range(n):
        src = tail[V7X_SUBLANES - n + i:V7X_SUBLANES - n + i + 1, :]
        out = jnp.where(row == i, src, out)
    return out


def _ln_kernel(x_ref, g_ref, b_ref, o_ref):
    o_ref[...] = _layer_norm(x_ref[...], g_ref[...], b_ref[...])


def _input_ln(x2, g, b):
    T, D = x2.shape
    tm = TOKEN_TILE * 2
    return pl.pallas_call(
        _ln_kernel,
        grid=(T // tm,),
        in_specs=[pl.BlockSpec((tm, D), lambda i: (i, 0)), _full((1, D)), _full((1, D))],
        out_specs=pl.BlockSpec((tm, D), lambda i: (i, 0)),
        out_shape=jax.ShapeDtypeStruct((T, D), F32),
        compiler_params=_cparams("parallel"),
    )(x2, g.reshape(1, D), b.reshape(1, D))


def _ab_kernel(h_ref, w_ref, cw_ref, ng_ref, nb_ref, sw_ref, sb_ref,
               oa_ref, ob_ref, tail_ref):
    @pl.when(pl.program_id(1) == 0)
    def _():
        tail_ref[...] = jnp.zeros_like(tail_ref)

    tm = h_ref.shape[0]
    W = BRANCH_W
    z = _dot(h_ref[...].astype(BF16), w_ref[...])
    ch = z[:, W:2 * W] * z[:, 2 * W:3 * W]
    tail = tail_ref[...]
    s1 = _shift_rows(ch, 1, tail)
    s2 = _shift_rows(ch, 2, tail)
    tail_ref[...] = ch[tm - V7X_SUBLANES:, :]
    cw = cw_ref[...]
    y = cw[2:3, :] * ch + cw[1:2, :] * s1 + cw[0:1, :] * s2
    oa_ref[...] = (z[:, 0:W] * y).astype(oa_ref.dtype)
    u = z[:, 3 * W:4 * W]
    v = _layer_norm(z[:, 4 * W:5 * W], ng_ref[...], nb_ref[...]).astype(BF16)
    gw = W // SG_GROUPS
    grp = lax.broadcasted_iota(jnp.int32, (SG_BLOCK, W), 1) // gw
    for blk in range(tm // SG_BLOCK):
        rows = slice(blk * SG_BLOCK, (blk + 1) * SG_BLOCK)
        vb = v[rows, :]
        sv = sb_ref[...]
        for g in range(SG_GROUPS):
            sv = sv + jnp.where(grp == g, _dot(sw_ref[g], vb), 0.0)
        ob_ref[rows, :] = (u[rows, :] * sv).astype(ob_ref.dtype)


def _mixers_ab(h, w_ab, conv_w, ng, nb, sg_wm, sg_bias):
    B, S, D = h.shape
    tm = TOKEN_TILE
    W = BRANCH_W
    blk = lambda b, j: (b, j, 0)
    return pl.pallas_call(
        _ab_kernel,
        grid=(B, S // tm),
        in_specs=[pl.BlockSpec((None, tm, D), blk),
                  _full((D, 5 * W)), _full((CONV_W, W)), _full((1, W)), _full((1, W)),
                  _full((SG_GROUPS, SG_BLOCK, SG_BLOCK)), _full((SG_BLOCK, W))],
        out_specs=[pl.BlockSpec((None, tm, W), blk), pl.BlockSpec((None, tm, W), blk)],
        out_shape=[jax.ShapeDtypeStruct((B, S, W), BF16)] * 2,
        scratch_shapes=[pltpu.VMEM((V7X_SUBLANES, W), F32)],
        compiler_params=_cparams("parallel", "arbitrary"),
    )(h, w_ab, conv_w, ng, nb, sg_wm, sg_bias)


def _softplus(x):
    return jnp.maximum(x, 0.0) + jnp.log(1.0 + jnp.exp(-jnp.abs(x)))


def _rwkv_kernel(h_ref, w_ref, mu_ref, w0_ref, a0_ref, lora_ref, gup_ref, kk_ref, ka_ref,
                 rk_ref, lng_ref, lnb_ref, ones_ref, o_ref,
                 tail_ref, state_ref, v_s, y_s, gend_s, ar_s, bk_s, inv_s, rb_s, mk_s):
    @pl.when(pl.program_id(1) == 0)
    def _():
        tail_ref[...] = jnp.zeros_like(tail_ref)
        state_ref[...] = jnp.zeros_like(state_ref)

    tm = h_ref.shape[0]
    W = BRANCH_W
    N = RW_HEAD_DIM
    L = RW_CHUNK
    z = _dot(h_ref[...].astype(BF16), w_ref[...])
    zprev = _shift_rows(z, 1, tail_ref[...])
    tail_ref[...] = z[tm - V7X_SUBLANES:, :]
    z = z + (zprev - z) * mu_ref[...]
    r = z[:, 0:W]
    k = z[:, W:2 * W]
    v = z[:, 2 * W:3 * W]
    xwa = z[:, 3 * W:3 * W + 2 * RW_LORA]
    xg = z[:, 3 * W + 2 * RW_LORA:]
    lane = lax.broadcasted_iota(jnp.int32, xwa.shape, 1)
    lora_in = jnp.where(lane < RW_LORA, jnp.tanh(xwa), xwa)
    lora = _dot_split(lora_in, lora_ref[...])
    w = -_softplus(-(w0_ref[...] + lora[:, 0:W])) - 0.5
    log_decay = -jnp.exp(w)
    a = jax.nn.sigmoid(a0_ref[...] + lora[:, W:2 * W])
    g = _dot(jax.nn.sigmoid(xg).astype(BF16), gup_ref[...])
    ones = ones_ref[...]
    kk = k * kk_ref[...]
    kk = kk / jnp.maximum(jnp.sqrt(_dot_select(kk * kk, ones)), 1e-12)
    k = k * (1.0 + (a - 1.0) * ka_ref[...])
    bonus = _dot_select(r * k * rk_ref[...], ones) * v

    ti = lax.broadcasted_iota(jnp.int32, (L, L), 0)
    tj = lax.broadcasted_iota(jnp.int32, (L, L), 1)
    tri3 = jnp.concatenate([(ti >= tj).astype(BF16)] * 3, axis=1)
    eye = (ti == tj).astype(F32)
    t2 = lax.broadcasted_iota(jnp.int32, (2 * L, L), 0)
    s2 = lax.broadcasted_iota(jnp.int32, (2 * L, L), 1)
    mask2 = ((t2 < L) & (t2 > s2)) | (t2 - L >= s2)
    n_chunks = tm // L

    alpha = -kk
    beta = kk * a
    v_s[...] = v.astype(BF16)
    for c in range(n_chunks):
        rows = slice(c * L, (c + 1) * L)
        ld = log_decay[rows, :]
        cum = _dot(tri3, jnp.concatenate(_bf16_terms(ld, 3), axis=0))
        gam = jnp.exp(cum)
        gam_inv = jnp.exp(-cum)
        gend_s[c * V7X_SUBLANES:(c + 1) * V7X_SUBLANES, :] = jnp.broadcast_to(
            gam[L - 1:L, :], (V7X_SUBLANES, W))
        ar_all = jnp.concatenate([alpha[rows, :] * jnp.exp(cum - ld), r[rows, :] * gam],
                                 axis=0).astype(BF16)
        bk_all = jnp.concatenate([beta[rows, :] * gam_inv, k[rows, :] * gam_inv],
                                 axis=0).astype(BF16)
        ar_s[c] = ar_all
        bk_s[c] = bk_all

    heads = [(c, hd) for c in range(n_chunks) for hd in range(RW_HEADS)]
    hcols = lambda hd: slice(hd * N, (hd + 1) * N)
    invs, ps = [], []
    for c, hd in heads:
        pair = _dot_nt(ar_s[c, :, hcols(hd)], bk_s[c, :, hcols(hd)])
        m_b = jnp.where(mask2, pair[:, :L], 0.0)
        mk_s[c * RW_HEADS + hd] = jnp.where(mask2, pair[:, L:], 0.0).astype(BF16)
        rb_s[c * RW_HEADS + hd] = m_b[L:].astype(BF16)
        invs.append(eye + m_b[:L])
        ps.append(m_b[:L].astype(BF16))
    ps = [_dot(pb, pb) for pb in ps]
    for _ in range(int(math.log2(L)) - 2):
        both = [_dot(jnp.concatenate([p.astype(BF16), inv.astype(BF16)], axis=0), p.astype(BF16))
                for p, inv in zip(ps, invs)]
        ps = [b2[:L] for b2 in both]
        invs = [inv + b2[L:] for inv, b2 in zip(invs, both)]
    for i, (inv, p) in enumerate(zip(invs, ps)):
        inv_s[i] = (inv + _dot(inv.astype(BF16), p.astype(BF16))).astype(BF16)

    def chunk(c, carry):
        rows = pl.ds(pl.multiple_of(c * L, L), L)
        ar_all = ar_s[c]
        bk_all = bk_s[c]
        v_all = v_s[rows, :]
        gend = gend_s[pl.ds(pl.multiple_of(c * V7X_SUBLANES, V7X_SUBLANES), 1), :]
        hs = range(RW_HEADS)
        sts = [state_ref[hd] for hd in hs]
        bases = [_dot_nt(ar_all[:, hcols(hd)], sts[hd].astype(BF16))
                 + _dot(mk_s[c * RW_HEADS + hd], v_all[:, hcols(hd)]) for hd in hs]
        us = [_dot(inv_s[c * RW_HEADS + hd], bases[hd][:L].astype(BF16)).astype(BF16) for hd in hs]
        for hd in hs:
            y_s[rows, hcols(hd)] = bases[hd][L:] + _dot(rb_s[c * RW_HEADS + hd], us[hd])
        for hd in hs:
            st = sts[hd] + _dot_tn(jnp.concatenate([us[hd], v_all[:, hcols(hd)]], axis=0),
                                   bk_all[:, hcols(hd)])
            state_ref[hd] = st * gend[:, hcols(hd)]
        return carry

    lax.fori_loop(0, n_chunks, chunk, 0)

    y = y_s[...]
    inv_n = 1.0 / N
    m = _dot_select(y, ones) * inv_n
    yc = y - m
    var = _dot_select(yc * yc, ones) * inv_n
    yn = yc * lax.rsqrt(var + RW_LN_EPS) * lng_ref[...] + lnb_ref[...]
    o_ref[...] = ((yn + bonus) * g).astype(o_ref.dtype)


def _mixer_rwkv(h, w_c, mu, w0, a0, lora_w, g_up, k_k, k_a, r_k, ln_g, ln_b, head_ones):
    B, S, D = h.shape
    tm = SEQ_TILE
    W = BRANCH_W
    L = RW_CHUNK
    nc = tm // L
    blk = lambda b, j: (b, j, 0)
    row = lambda n: _full((1, n))
    return pl.pallas_call(
        _rwkv_kernel,
        grid=(B, S // tm),
        in_specs=[pl.BlockSpec((None, tm, D), blk), _full((D, RW_IN)), row(RW_IN), row(W), row(W),
                  _full((2 * RW_LORA, 2 * W)), _full((RW_GATE_LORA, W)), row(W), row(W), row(W),
                  row(W), row(W), _full(head_ones.shape)],
        out_specs=pl.BlockSpec((None, tm, W), blk),
        out_shape=jax.ShapeDtypeStruct((B, S, W), BF16),
        scratch_shapes=[pltpu.VMEM((V7X_SUBLANES, RW_IN), F32),
                        pltpu.VMEM((RW_HEADS, RW_HEAD_DIM, RW_HEAD_DIM), F32),
                        pltpu.VMEM((tm, W), BF16),
                        pltpu.VMEM((tm, W), F32),
                        pltpu.VMEM((nc * V7X_SUBLANES, W), F32),
                        pltpu.VMEM((nc, 2 * L, W), BF16),
                        pltpu.VMEM((nc, 2 * L, W), BF16),
                        pltpu.VMEM((nc * RW_HEADS, L, L), BF16),
                        pltpu.VMEM((nc * RW_HEADS, L, L), BF16),
                        pltpu.VMEM((nc * RW_HEADS, 2 * L, L), BF16)],
        compiler_params=_cparams("parallel", "arbitrary"),
    )(h, w_c, mu, w0, a0, lora_w, g_up, k_k, k_a, r_k, ln_g, ln_b, head_ones)


def _s5_kernel(h_ref, w_ref, bre_ref, bim_ref, are_ref, aim_ref, cre_ref, cim_ref,
               d_ref, gw_ref, gb_ref, o_ref, xr_s, xi_s, st_s):
    @pl.when(pl.program_id(1) == 0)
    def _():
        st_s[...] = jnp.zeros_like(st_s)

    tm = h_ref.shape[0]
    u = _dot(h_ref[...].astype(BF16), w_ref[...])
    ub = u.astype(BF16)
    xr_s[...] = _dot(ub, bre_ref[...])
    xi_s[...] = _dot(ub, bim_ref[...])
    ar = are_ref[...]
    ai = aim_ref[...]

    def step(t, carry):
        xr, xi = carry
        row = pl.ds(t, 1)
        nr = ar * xr - ai * xi + xr_s[row, :]
        ni = ar * xi + ai * xr + xi_s[row, :]
        xr_s[row, :] = nr
        xi_s[row, :] = ni
        return nr, ni

    xr, xi = lax.fori_loop(0, tm, step, (st_s[0:1, :], st_s[1:2, :]), unroll=8)
    st_s[0:1, :] = xr
    st_s[1:2, :] = xi
    y = (_dot(xr_s[...].astype(BF16), cre_ref[...]) - _dot(xi_s[...].astype(BF16), cim_ref[...])
         + d_ref[...] * u)
    y = jax.nn.gelu(y)
    gate = jax.nn.sigmoid(_dot(y.astype(BF16), gw_ref[...]) + gb_ref[...])
    o_ref[...] = (y * gate).astype(o_ref.dtype)


def _mixer_s5(h, w_d, bre, bim, are, aim, cre, cim, d, glu_w, glu_b):
    B, S, D = h.shape
    tm = TOKEN_TILE
    W = BRANCH_W
    blk = lambda b, j: (b, j, 0)
    return pl.pallas_call(
        _s5_kernel,
        grid=(B, S // tm),
        in_specs=[pl.BlockSpec((None, tm, D), blk), _full((D, W)),
                  _full((W, S5_N)), _full((W, S5_N)), _full(are.shape), _full(aim.shape),
                  _full((S5_N, W)), _full((S5_N, W)), _full((1, W)), _full((W, W)), _full((1, W))],
        out_specs=pl.BlockSpec((None, tm, W), blk),
        out_shape=jax.ShapeDtypeStruct((B, S, W), BF16),
        scratch_shapes=[pltpu.VMEM((tm, S5_N), F32), pltpu.VMEM((tm, S5_N), F32),
                        pltpu.VMEM((V7X_SUBLANES, S5_N), F32)],
        compiler_params=_cparams("parallel", "arbitrary"),
    )(h, w_d, bre, bim, are, aim, cre, cim, d, glu_w, glu_b)


def _merge_kernel(h_ref, oa_ref, ob_ref, oc_ref, od_ref, wg_ref, gb_ref, br_ref, wo_ref,
                  g_ref, b_ref, o_ref):
    D = D_MODEL
    h = h_ref[...]
    hb = h.astype(BF16)
    merged = None
    for i, br_in in enumerate((oa_ref, ob_ref, oc_ref, od_ref)):
        gate = jax.nn.sigmoid(_dot(hb, wg_ref[:, i * D:(i + 1) * D]) + gb_ref[i:i + 1, :])
        term = gate * _dot(br_in[...], br_ref[i])
        merged = term if merged is None else merged + term
    y = _dot(merged.astype(BF16), wo_ref[...])
    o_ref[...] = _layer_norm(DN_ALPHA * h + y, g_ref[...], b_ref[...])


def _merge(h2, oa, ob, oc, od, w_g, gate_b, br_proj, w_out, g, b):
    T, D = h2.shape
    tm = TOKEN_TILE
    W = BRANCH_W
    tok = lambda n: pl.BlockSpec((tm, n), lambda i: (i, 0))
    return pl.pallas_call(
        _merge_kernel,
        grid=(T // tm,),
        in_specs=[tok(D), tok(W), tok(W), tok(W), tok(W),
                  _full((D, N_BRANCH * D)), _full((N_BRANCH, D)), _full((N_BRANCH, W, D)),
                  _full((D, D)), _full((1, D)), _full((1, D))],
        out_specs=tok(D),
        out_shape=jax.ShapeDtypeStruct((T, D), F32),
        compiler_params=_cparams("parallel"),
    )(h2, oa, ob, oc, od, w_g, gate_b, br_proj, w_out, g, b)


def _kv_kernel(m_ref, wk_ref, wv_ref, k_ref, v_ref):
    mb = m_ref[...].astype(BF16)
    k_ref[...] = _dot(mb, wk_ref[...]).astype(k_ref.dtype)
    v_ref[...] = _dot(mb, wv_ref[...]).astype(v_ref.dtype)


def _kv_proj(mem2, wk, wv):
    R, D = mem2.shape
    tm = TOKEN_TILE
    tok = pl.BlockSpec((tm, D), lambda i: (i, 0))
    return pl.pallas_call(
        _kv_kernel,
        grid=(R // tm,),
        in_specs=[tok, _full((D, D)), _full((D, D))],
        out_specs=[tok, tok],
        out_shape=[jax.ShapeDtypeStruct((R, D), BF16)] * 2,
        compiler_params=_cparams("parallel"),
    )(mem2, wk, wv)


def _attn_kernel(h_ref, wq_ref, k_ref, v_ref, wo_ref, g_ref, b_ref, o_ref, ot_ref):
    h = h_ref[...]
    q = _dot(h.astype(BF16), wq_ref[...]).astype(BF16)
    scale = XA_HEAD_DIM ** -0.5
    outs = []
    for hd in range(XA_HEADS):
        cols = slice(hd * XA_HEAD_DIM, (hd + 1) * XA_HEAD_DIM)
        s = _dot_nt(q[:, cols], k_ref[:, cols]) * scale
        s = s - jnp.max(s, axis=-1, keepdims=True)
        p = jnp.exp(s)
        p = p / jnp.sum(p, axis=-1, keepdims=True)
        outs.append(_dot(p.astype(BF16), v_ref[:, cols]))
    o = jnp.concatenate(outs, axis=-1).astype(BF16)
    y = _dot(o, wo_ref[...])
    out = _layer_norm(DN_ALPHA * h + y, g_ref[...], b_ref[...])
    o_ref[...] = out
    _store_row_tiles(ot_ref, out)


def _cross_attention(h, kmem, vmem, wq, wo, g, b):
    B, S, D = h.shape
    M = kmem.shape[1]
    tm = TOKEN_TILE
    n = S // tm
    blk = lambda bi, j: (bi, j, 0)
    mem = lambda bi, j: (bi, 0, 0)
    return pl.pallas_call(
        _attn_kernel,
        grid=(B, n),
        in_specs=[pl.BlockSpec((None, tm, D), blk), _full((D, D)),
                  pl.BlockSpec((None, M, D), mem), pl.BlockSpec((None, M, D), mem),
                  _full((D, D)), _full((1, D)), _full((1, D))],
        out_specs=[pl.BlockSpec((None, tm, D), blk),
                   pl.BlockSpec((tm * ROW_TILE, V7X_LANES), lambda bi, j: (bi * n + j, 0))],
        out_shape=[jax.ShapeDtypeStruct((B, S, D), F32),
                   jax.ShapeDtypeStruct((B * S * ROW_TILE, V7X_LANES), F32)],
        compiler_params=_cparams("parallel", "parallel"),
    )(h, wq, kmem, vmem, wo, g, b)


def _router_kernel(h_ref, w_ref, b_ref, route_ref, rw_ref, cnt_ref, carry_ref):
    @pl.when(pl.program_id(0) == 0)
    def _():
        carry_ref[...] = jnp.zeros_like(carry_ref)

    tm = h_ref.shape[0]
    E = N_EXPERTS
    logits = _dot_split(h_ref[...], w_ref[...]) + b_ref[...]
    lane = lax.broadcasted_iota(jnp.int32, (tm, E), 1)
    work = logits
    vals, hots, idxs = [], [], []
    for _ in range(TOP_K):
        m = jnp.max(work, axis=-1, keepdims=True)
        idx = jnp.min(jnp.where(work == m, lane, E), axis=-1, keepdims=True)
        hot = lane == idx
        work = jnp.where(hot, -jnp.inf, work)
        vals.append(m)
        hots.append(hot)
        idxs.append(idx)
    exps = [jnp.exp(v - vals[0]) for v in vals]
    denom = exps[0] + exps[1] + exps[2] + exps[3]
    multi = (hots[0] | hots[1] | hots[2] | hots[3]).astype(F32)
    ti = lax.broadcasted_iota(jnp.int32, (tm, tm), 0)
    tj = lax.broadcasted_iota(jnp.int32, (tm, tm), 1)
    before = (ti > tj).astype(BF16)
    prefix = _dot(before, multi.astype(BF16)) + carry_ref[0:1, :]
    out_lane = lax.broadcasted_iota(jnp.int32, (tm, ROUTE_LANES), 1)
    route = jnp.zeros((tm, ROUTE_LANES), jnp.int32)
    rw = jnp.zeros((tm, ROUTE_LANES), F32)
    for kk in range(TOP_K):
        rank = jnp.sum(jnp.where(hots[kk], prefix, 0.0), axis=-1, keepdims=True).astype(jnp.int32)
        route = jnp.where(out_lane == kk, idxs[kk], route)
        route = jnp.where(out_lane == TOP_K + kk, rank, route)
        rw = jnp.where(out_lane == kk, exps[kk] / denom, rw)
    route_ref[...] = route
    rw_ref[...] = rw
    carry_ref[0:1, :] = carry_ref[0:1, :] + jnp.sum(multi, axis=0, keepdims=True)
    cnt_ref[...] = carry_ref[...]


def _router(h2, router_w, router_b):
    T, D = h2.shape
    tm = TOKEN_TILE
    E = N_EXPERTS
    return pl.pallas_call(
        _router_kernel,
        grid=(T // tm,),
        in_specs=[pl.BlockSpec((tm, D), lambda i: (i, 0)), _full((D, E)), _full((1, E))],
        out_specs=[pl.BlockSpec((tm, ROUTE_LANES), lambda i: (i, 0)),
                   pl.BlockSpec((tm, ROUTE_LANES), lambda i: (i, 0)),
                   _full((V7X_SUBLANES, E))],
        out_shape=[jax.ShapeDtypeStruct((T, ROUTE_LANES), jnp.int32),
                   jax.ShapeDtypeStruct((T, ROUTE_LANES), F32),
                   jax.ShapeDtypeStruct((V7X_SUBLANES, E), F32)],
        scratch_shapes=[pltpu.VMEM((V7X_SUBLANES, E), F32)],
        compiler_params=_cparams("arbitrary"),
    )(h2, router_w, router_b)


def _dest_kernel(route_ref, start_ref, dest_ref):
    route = route_ref[...]
    tm = route.shape[0]
    lane = lax.broadcasted_iota(jnp.int32, (tm, N_EXPERTS), 1)
    out_lane = lax.broadcasted_iota(jnp.int32, (tm, ROUTE_LANES), 1)
    dest = jnp.zeros((tm, ROUTE_LANES), jnp.int32)
    start = start_ref[...]
    for kk in range(TOP_K):
        idx = route[:, kk:kk + 1]
        base = jnp.sum(jnp.where(lane == idx, start, 0), axis=-1, keepdims=True)
        dest = jnp.where(out_lane == kk, base + route[:, TOP_K + kk:TOP_K + kk + 1], dest)
    dest_ref[...] = dest


def _dest_slots(route, pad_start):
    T = route.shape[0]
    tm = TOKEN_TILE * 2
    return pl.pallas_call(
        _dest_kernel,
        grid=(T // tm,),
        in_specs=[pl.BlockSpec((tm, ROUTE_LANES), lambda i: (i, 0)), _full((1, N_EXPERTS))],
        out_specs=pl.BlockSpec((tm, ROUTE_LANES), lambda i: (i, 0)),
        out_shape=jax.ShapeDtypeStruct((T, ROUTE_LANES), jnp.int32),
        compiler_params=_cparams("parallel"),
    )(route, pad_start.reshape(1, N_EXPERTS))


def _slot_source_kernel(fill_lo_ref, fill_hi_ref, dest_hbm, w_ref, p_ref, row_ref, o_ref,
                        dest_s, sem, *, n_tok):
    for c in range(w_ref.shape[1] // GU_BLOCK):
        cols = slice(c * GU_BLOCK, (c + 1) * GU_BLOCK)
        o_ref[:, cols] = _dot(w_ref[:, cols].astype(BF16), p_ref[...]).astype(o_ref.dtype)

    i = pl.program_id(0) * pl.num_programs(1) + pl.program_id(1)
    n = dest_s.shape[0]

    @pl.when(i == 0)
    def _():
        def fill(s, carry):
            parity = (s // MOE_TILE + 1) % 2
            row_ref[s] = n_tok * TOP_K + parity * MOE_TILE + s % MOE_TILE
            return carry

        lax.fori_loop(0, MOE_TILE, fill, 0)
        for e in range(fill_lo_ref.shape[0]):
            lax.fori_loop(MOE_TILE + fill_lo_ref[e], MOE_TILE + fill_hi_ref[e], fill, 0)

    idx_copy = pltpu.make_async_copy(dest_hbm.at[i], dest_s, sem)
    idx_copy.start()
    idx_copy.wait()
    slots = row_ref.at[pl.ds(MOE_TILE, row_ref.shape[0] - MOE_TILE)]
    first_tok = i * (n // TOP_K)

    def body(t, carry):
        for kk in range(TOP_K):
            slots[dest_s[t * TOP_K + kk]] = (kk * n_tok + first_tok) + t
        return carry

    lax.fori_loop(0, n // TOP_K, body, 0, unroll=8)


def _slot_sources_and_weights(fill_lo, fill_hi, dest, n_slots, w1_layers, layer):
    _, E, D, F2 = w1_layers.shape
    tr = TOKEN_TILE
    steps = E * (D // tr)
    n = dest.size // steps
    j = jnp.arange(GU_BLOCK)
    perm = (j[:, None] == jnp.where(j < V7X_LANES, 2 * j, 2 * (j - V7X_LANES) + 1)[None, :])
    return pl.pallas_call(
        functools.partial(_slot_source_kernel, n_tok=dest.size // TOP_K),
        grid_spec=pltpu.PrefetchScalarGridSpec(
            num_scalar_prefetch=2,
            grid=(E, D // tr),
            in_specs=[pl.BlockSpec(memory_space=pl.ANY),
                      pl.BlockSpec((None, None, tr, F2), lambda e, r, lo, hi: (layer, e, r, 0)),
                      pl.BlockSpec((GU_BLOCK, GU_BLOCK), lambda e, r, lo, hi: (0, 0))],
            out_specs=[pl.BlockSpec(memory_space=pltpu.SMEM),
                       pl.BlockSpec((None, tr, F2), lambda e, r, lo, hi: (e, r, 0))],
            scratch_shapes=[pltpu.SMEM((n,), jnp.int32), pltpu.SemaphoreType.DMA]),
        out_shape=[jax.ShapeDtypeStruct((MOE_TILE + n_slots,), jnp.int32),
                   jax.ShapeDtypeStruct((E, D, F2), BF16)],
        compiler_params=_cparams("arbitrary", "arbitrary"),
    )(fill_lo, fill_hi, dest.reshape(steps, n), w1_layers, perm.astype(BF16))


GU_BLOCK = 2 * V7X_LANES


def _unzip_bias(b1):
    E, F2 = b1.shape
    return b1.reshape(E, F2 // GU_BLOCK, V7X_LANES, 2).transpose(0, 1, 3, 2).reshape(E, 1, F2)


def _expert_ffn(x, w1_ref, b1_ref, w2_ref, b2_ref):
    hid = _dot(x, w1_ref[...]) + b1_ref[...]
    acts = []
    for c in range(hid.shape[1] // GU_BLOCK):
        gate = jnp.minimum(hid[:, c * GU_BLOCK:c * GU_BLOCK + V7X_LANES], SWIGLU_LIMIT)
        up = jnp.clip(hid[:, c * GU_BLOCK + V7X_LANES:(c + 1) * GU_BLOCK],
                      -SWIGLU_LIMIT, SWIGLU_LIMIT)
        acts.append(((up + 1.0) * gate * jax.nn.sigmoid(SWIGLU_ALPHA * gate)).astype(BF16))
    return _dot(jnp.concatenate(acts, axis=-1), w2_ref[...].astype(BF16)) + b2_ref[...]


def _expert_kernel(blk_e_ref, n_used_ref, src_ref, h_hbm, w1_ref, b1_ref, w2_ref, b2_ref, yk_hbm,
                   xbuf0, xbuf1, ybuf0, ybuf1, gsem, ssem):
    del blk_e_ref
    i = pl.program_id(0)
    n_used = n_used_ref[0]
    xbufs = (xbuf0, xbuf1)
    ybufs = (ybuf0, ybuf1)
    R = ROW_TILE
    tm = xbuf0.shape[0] // R
    T = h_hbm.shape[0] // R
    n_assign = T * TOP_K

    def tile_rows(r):
        return pl.ds(r * R if isinstance(r, int) else pl.multiple_of(r * R, R), R)

    def queue_of(j):
        return j % V7X_DMA_PRIORITIES if isinstance(j, int) else 0

    def gather_row(tile, b, j):
        row = src_ref[(tile + 1) * tm + j]
        tok = row & (T - 1) if T & (T - 1) == 0 else lax.rem(row, T)
        pltpu.make_async_copy(h_hbm.at[tile_rows(tok)], xbufs[b].at[tile_rows(j)],
                              gsem.at[b]).start(priority=queue_of(j))

    def scatter_row(tile, b, j):
        row = src_ref[(tile + 1) * tm + j]
        pltpu.make_async_copy(ybufs[b].at[tile_rows(j)], yk_hbm.at[tile_rows(row)],
                              ssem.at[b]).start(priority=queue_of(j))

    def wait_gather(b):
        pltpu.make_async_copy(h_hbm.at[pl.ds(0, tm * R)], xbufs[b], gsem.at[b]).wait()

    def wait_scatter(b):
        pltpu.make_async_copy(ybufs[b], yk_hbm.at[pl.ds(0, tm * R)], ssem.at[b]).wait()

    def rows_loop(fn):
        def body(j, carry):
            fn(j)
            return carry
        lax.fori_loop(0, tm, body, 0)

    @pl.when(i == 0)
    def _():
        rows_loop(lambda j: gather_row(0, 0, j))
        ybuf0[...] = jnp.zeros_like(ybuf0)
        ybuf1[...] = jnp.zeros_like(ybuf1)
        pltpu.make_async_copy(ybuf0, yk_hbm.at[pl.ds(n_assign * R, tm * R)], ssem.at[0]).start()

    for b in range(2):
        @pl.when((i < n_used) & (i % 2 == b))
        def _(b=b):
            wait_gather(b)
            wait_scatter(b)
            for j in range(tm):
                gather_row(i + 1, 1 - b, j)
                scatter_row(i - 1, 1 - b, j)
            x = _load_row_tiles(xbufs[b]).astype(BF16)
            _store_row_tiles(ybufs[b], _expert_ffn(x, w1_ref, b1_ref, w2_ref, b2_ref))

        @pl.when((i == n_used) & (i % 2 == b))
        def _(b=b):
            wait_gather(b)
            wait_scatter(b)
            rows_loop(lambda j: scatter_row(i - 1, 1 - b, j))
            wait_scatter(1 - b)


def _experts(blk_e, n_used, src, h_tiles, w1, b1, w2_layers, layer, b2):
    R = ROW_TILE
    T = h_tiles.shape[0] // R
    D = D_MODEL
    tm = MOE_TILE
    F = D_FF
    n_tiles = src.shape[0] // tm - 1
    per_e = lambda shape: pl.BlockSpec((None,) + shape, lambda i, be, nu, sr: (be[i], 0, 0))
    w2_spec = pl.BlockSpec((None, None, F, D), lambda i, be, nu, sr: (layer, be[i], 0, 0))
    return pl.pallas_call(
        _expert_kernel,
        grid_spec=pltpu.PrefetchScalarGridSpec(
            num_scalar_prefetch=3,
            grid=(n_tiles,),
            in_specs=[pl.BlockSpec(memory_space=pl.ANY),
                      per_e((D, 2 * F)), per_e((1, 2 * F)), w2_spec, per_e((1, D))],
            out_specs=pl.BlockSpec(memory_space=pl.ANY),
            scratch_shapes=[pltpu.VMEM((tm * R, V7X_LANES), F32)] * 4
                           + [pltpu.SemaphoreType.DMA((2,)), pltpu.SemaphoreType.DMA((2,))]),
        out_shape=jax.ShapeDtypeStruct(((T * TOP_K + 2 * tm) * R, V7X_LANES), F32),
        compiler_params=_cparams("arbitrary"),
    )(blk_e, n_used, src, h_tiles, w1, b1, w2_layers, b2)


def _combine_kernel(y0_ref, y1_ref, y2_ref, y3_ref, h_ref, rw_ref, g_ref, b_ref, o_ref):
    rw = rw_ref[...]
    moe = rw[:, 0:1] * _load_row_tiles(y0_ref)
    for kk, y_ref in enumerate((y1_ref, y2_ref, y3_ref), start=1):
        moe = moe + rw[:, kk:kk + 1] * _load_row_tiles(y_ref)
    o_ref[...] = _layer_norm(DN_ALPHA * h_ref[...] + moe, g_ref[...], b_ref[...])


def _combine(yk, h2, rw, g, b):
    T, D = h2.shape
    tm = TOKEN_TILE
    n = T // tm
    choice = lambda kk: pl.BlockSpec((tm * ROW_TILE, V7X_LANES), lambda i: (kk * n + i, 0))
    return pl.pallas_call(
        _combine_kernel,
        grid=(n,),
        in_specs=[choice(kk) for kk in range(TOP_K)]
                 + [pl.BlockSpec((tm, D), lambda i: (i, 0)),
                    pl.BlockSpec((tm, ROUTE_LANES), lambda i: (i, 0)),
                    _full((1, D)), _full((1, D))],
        out_specs=pl.BlockSpec((tm, D), lambda i: (i, 0)),
        out_shape=jax.ShapeDtypeStruct((T, D), F32),
        compiler_params=_cparams("parallel"),
    )(yk, yk, yk, yk, h2, rw, g, b)


def _s5_discretise(a_re, a_im, b_re, b_im, c_re, c_im, log_dt):
    G, P, C = S5_GROUPS, S5_STATE, S5_CH
    dt = jnp.exp(log_dt)[:, None]
    mag = jnp.exp(a_re * dt)
    abar_re = mag * jnp.cos(a_im * dt)
    abar_im = mag * jnp.sin(a_im * dt)
    den = a_re * a_re + a_im * a_im
    num_re = abar_re - 1.0
    coef_re = (num_re * a_re + abar_im * a_im) / den
    coef_im = (abar_im * a_re - num_re * a_im) / den
    bbar_re = coef_re[..., None] * b_re - coef_im[..., None] * b_im
    bbar_im = coef_re[..., None] * b_im + coef_im[..., None] * b_re
    eye = jnp.eye(G, dtype=F32)
    to_in = lambda m: jnp.einsum('gpc,gh->gchp', m, eye).reshape(G * C, G * P)
    to_out = lambda m: jnp.einsum('gcp,gh->gphc', m, eye).reshape(G * P, G * C)
    return (to_in(bbar_re).astype(BF16), to_in(bbar_im).astype(BF16),
            abar_re.reshape(1, G * P), abar_im.reshape(1, G * P),
            to_out(c_re).astype(BF16), to_out(c_im).astype(BF16))


def _moe_layer(h2, h_tiles, layer, router_w, router_b, w1_layers, b1, w2_layers, b2, g, b):
    T, D = h2.shape
    E = N_EXPERTS
    tm = TOKEN_TILE
    route, rw, counts = _router(h2, router_w, router_b.reshape(1, E))
    counts = counts[0].astype(jnp.int32)
    padded = (counts + MOE_TILE - 1) // MOE_TILE * MOE_TILE
    pad_end = jnp.cumsum(padded)
    pad_start = pad_end - padded
    n_tiles = T * TOP_K // MOE_TILE + E
    n_slots = n_tiles * MOE_TILE
    tile_start = jnp.arange(n_tiles, dtype=jnp.int32) * MOE_TILE
    blk_e = jnp.minimum(jnp.sum(tile_start[:, None] >= pad_end[None, :], axis=1), E - 1).astype(jnp.int32)
    n_used = (pad_end[E - 1:] // MOE_TILE).astype(jnp.int32)
    dest = _dest_slots(route, pad_start.astype(jnp.int32))
    fill_lo = jnp.concatenate([pad_start + counts, pad_end[E - 1:]]).astype(jnp.int32)
    fill_hi = jnp.concatenate([pad_end, jnp.full((1,), n_slots)]).astype(jnp.int32)
    src, w1 = _slot_sources_and_weights(fill_lo, fill_hi, dest[:, :TOP_K], n_slots,
                                        w1_layers, layer)
    yk = _experts(blk_e, n_used, src, h_tiles, w1, _unzip_bias(b1), w2_layers, layer,
                  b2[:, None, :])
    return _combine(yk, h2, rw, g.reshape(1, D), b.reshape(1, D))


def kernel(x, mem, ln_in_g, ln_in_b, w_in, conv_w, sg_norm_g, sg_norm_b, sg_w, sg_b, rw_mu, rw_w0, rw_w_up, rw_a0, rw_a_up, rw_g_up, rw_k_k, rw_k_a, rw_r_k, rw_ln_g, rw_ln_b, s5_a_re, s5_a_im, s5_b_re, s5_b_im, s5_c_re, s5_c_im, s5_d, s5_log_dt, s5_glu_w, s5_glu_b, br_proj, gate_b, w_out, ln1_g, ln1_b, xa_wq, xa_wk, xa_wv, xa_wo, ln2_g, ln2_b, router_w, router_b, ex_w1, ex_b1, ex_w2, ex_b2, ln3_g, ln3_b):
    B, S, D = x.shape
    M = mem.shape[1]
    T = B * S
    W = BRANCH_W
    row = lambda v: v.reshape(1, -1)
    pos = jnp.arange(SG_BLOCK)
    sg_mask = (pos[None, :] // CHUNK) <= (pos[:, None] // CHUNK)
    head_ones = jnp.kron(jnp.eye(RW_HEADS, dtype=F32), jnp.ones((RW_HEAD_DIM, RW_HEAD_DIM), F32))
    head_ones = jnp.concatenate([head_ones, head_ones], axis=0).astype(BF16)
    mem2 = mem.reshape(B * M, D)

    h = _input_ln(x.reshape(T, D), ln_in_g, ln_in_b)
    for l in range(DEPTH):
        w_l = w_in[l].astype(BF16)
        h3 = h.reshape(B, S, D)
        sg_wm = jnp.where(sg_mask[None], sg_w[l], 0.0).astype(BF16)
        sg_bias = jnp.repeat(sg_b[l].T, W // SG_GROUPS, axis=1)
        o_a, o_b = _mixers_ab(h3, w_l[:, :OFF_C], conv_w[l], row(sg_norm_g[l]), row(sg_norm_b[l]),
                              sg_wm, sg_bias)
        zero = jnp.zeros((RW_LORA, W), F32)
        lora_w = jnp.concatenate([jnp.concatenate([rw_w_up[l], zero], axis=1),
                                  jnp.concatenate([zero, rw_a_up[l]], axis=1)], axis=0)
        o_c = _mixer_rwkv(h3, w_l[:, OFF_C:OFF_D], row(rw_mu[l]), row(rw_w0[l]), row(rw_a0[l]),
                          lora_w, rw_g_up[l].astype(BF16), row(rw_k_k[l]), row(rw_k_a[l]),
                          row(rw_r_k[l]),
                          row(rw_ln_g[l]), row(rw_ln_b[l]), head_ones)
        s5p = _s5_discretise(s5_a_re[l], s5_a_im[l], s5_b_re[l], s5_b_im[l], s5_c_re[l],
                             s5_c_im[l], s5_log_dt[l])
        o_d = _mixer_s5(h3, w_l[:, OFF_D:OFF_G], *s5p, row(s5_d[l]),
                        s5_glu_w[l].astype(BF16), row(s5_glu_b[l]))
        flat = lambda o: o.reshape(T, W)
        h = _merge(h, flat(o_a), flat(o_b), flat(o_c), flat(o_d), w_l[:, OFF_G:], gate_b[l],
                   br_proj[l].astype(BF16), w_out[l].astype(BF16), row(ln1_g[l]), row(ln1_b[l]))
        kmem, vmem = _kv_proj(mem2, xa_wk[l].astype(BF16), xa_wv[l].astype(BF16))
        h, h_tiles = _cross_attention(h.reshape(B, S, D), kmem.reshape(B, M, D),
                                      vmem.reshape(B, M, D), xa_wq[l].astype(BF16),
                                      xa_wo[l].astype(BF16), row(ln2_g[l]), row(ln2_b[l]))
        h = _moe_layer(h.reshape(T, D), h_tiles, l, router_w[l], router_b[l], ex_w1, ex_b1[l],
                       ex_w2, ex_b2[l], ln3_g[l], ln3_b[l])
    return h.reshape(B, S, D)
```

```python
import functools
import math

import jax
import jax.numpy as jnp
from jax import lax
from jax.experimental import pallas as pl
from jax.experimental.pallas import tpu as pltpu

F32 = jnp.float32
BF16 = jnp.bfloat16

D_MODEL = 1024
DEPTH = 2
CHUNK = 64
BRANCH_W = 256
N_BRANCH = 4
CONV_W = 3
SG_BLOCK = 128
SG_GROUPS = 4
RW_HEADS = 4
RW_HEAD_DIM = 64
RW_LORA = 64
RW_GATE_LORA = 128
RW_IN = 3 * BRANCH_W + 2 * RW_LORA + RW_GATE_LORA
RW_LN_EPS = 64e-5
S5_CH = 16
S5_GROUPS = BRANCH_W // S5_CH
S5_STATE = 64
S5_N = S5_GROUPS * S5_STATE
OFF_B = 3 * BRANCH_W
OFF_C = OFF_B + 2 * BRANCH_W
OFF_D = OFF_C + RW_IN
OFF_G = OFF_D + BRANCH_W
XA_HEADS = 4
XA_HEAD_DIM = D_MODEL // XA_HEADS
N_EXPERTS = 32
TOP_K = 4
TOP_K_BITS = 2
D_FF = D_MODEL
SWIGLU_LIMIT = 7.0
SWIGLU_ALPHA = 1.702
LN_EPS = 1e-5
DN_ALPHA = (2 * DEPTH) ** 0.25

V7X_LANES = 128
V7X_SUBLANES = 8
V7X_VMEM_LIMIT_BYTES = 56 * 1024 * 1024
V7X_DMA_PRIORITIES = 2
TOKEN_TILE = 512
SEQ_TILE = 512
SLOT_MAP_CHUNK = 8192
RW_CHUNK = 64
MOE_TILE = 512
ROUTE_LANES = 128


def _cparams(*sem):
    return pltpu.CompilerParams(dimension_semantics=sem,
                                vmem_limit_bytes=V7X_VMEM_LIMIT_BYTES)


def _full(shape):
    nd = len(shape)
    return pl.BlockSpec(shape, lambda *_: (0,) * nd)


def _layer_norm(x, g, b, eps=LN_EPS):
    mu = jnp.mean(x, axis=-1, keepdims=True)
    xc = x - mu
    var = jnp.mean(xc * xc, axis=-1, keepdims=True)
    return xc * lax.rsqrt(var + eps) * g + b


def _contract(a, b, dims, **kw):
    return lax.dot_general(a, b, (dims, ((), ())), preferred_element_type=F32, **kw)


def _dot(a, b, **kw):
    return _contract(a, b, ((1,), (0,)), **kw)


def _bf16_terms(x, n):
    terms = []
    for _ in range(n):
        t = x.astype(BF16)
        terms.append(t)
        x = x - t.astype(F32)
    return terms


def _dot_split(a, b):
    ah, al = _bf16_terms(a, 2)
    bh, bl = _bf16_terms(b, 2)
    return _dot(jnp.concatenate([ah, ah, al], axis=1), jnp.concatenate([bh, bl, bh], axis=0))


def _dot_select(x, sel_stack):
    n = sel_stack.shape[0] // x.shape[1]
    return _dot(jnp.concatenate(_bf16_terms(x, n), axis=1), sel_stack)


def _dot_nt(a, b, **kw):
    return _contract(a, b, ((1,), (1,)), **kw)


def _dot_tn(a, b, **kw):
    return _contract(a, b, ((0,), (0,)), **kw)


ROW_TILE = D_MODEL // V7X_LANES


def _store_row_tiles(ref, x):
    n = x.shape[0]
    for c in range(ROW_TILE):
        ref[pl.ds(c, n, stride=ROW_TILE), :] = x[:, c * V7X_LANES:(c + 1) * V7X_LANES]


def _load_row_tiles(ref):
    n = ref.shape[0] // ROW_TILE
    return jnp.concatenate([ref[pl.ds(c, n, stride=ROW_TILE), :] for c in range(ROW_TILE)], axis=-1)


def _shift_rows(x, n, tail):
    rolled = pltpu.roll(x, n, 0)
    row = lax.broadcasted_iota(jnp.int32, x.shape, 0)
    out = rolled
    for i in range(n):
        src = tail[V7X_SUBLANES - n + i:V7X_SUBLANES - n + i + 1, :]
        out = jnp.where(row == i, src, out)
    return out


def _ln_kernel(x_ref, g_ref, b_ref, o_ref):
    o_ref[...] = _layer_norm(x_ref[...], g_ref[...], b_ref[...])


def _input_ln(x2, g, b):
    T, D = x2.shape
    tm = TOKEN_TILE * 2
    return pl.pallas_call(
        _ln_kernel,
        grid=(T // tm,),
        in_specs=[pl.BlockSpec((tm, D), lambda i: (i, 0)), _full((1, D)), _full((1, D))],
        out_specs=pl.BlockSpec((tm, D), lambda i: (i, 0)),
        out_shape=jax.ShapeDtypeStruct((T, D), F32),
        compiler_params=_cparams("parallel"),
    )(x2, g.reshape(1, D), b.reshape(1, D))


def _ab_kernel(h_ref, w_ref, cw_ref, ng_ref, nb_ref, sw_ref, sb_ref,
               oa_ref, ob_ref, tail_ref):
    @pl.when(pl.program_id(1) == 0)
    def _():
        tail_ref[...] = jnp.zeros_like(tail_ref)

    tm = h_ref.shape[0]
    W = BRANCH_W
    z = _dot(h_ref[...].astype(BF16), w_ref[...])
    ch = z[:, W:2 * W] * z[:, 2 * W:3 * W]
    tail = tail_ref[...]
    s1 = _shift_rows(ch, 1, tail)
    s2 = _shift_rows(ch, 2, tail)
    tail_ref[...] = ch[tm - V7X_SUBLANES:, :]
    cw = cw_ref[...]
    y = cw[2:3, :] * ch + cw[1:2, :] * s1 + cw[0:1, :] * s2
    oa_ref[...] = (z[:, 0:W] * y).astype(oa_ref.dtype)
    u = z[:, 3 * W:4 * W]
    v = _layer_norm(z[:, 4 * W:5 * W], ng_ref[...], nb_ref[...]).astype(BF16)
    gw = W // SG_GROUPS
    grp = lax.broadcasted_iota(jnp.int32, (SG_BLOCK, W), 1) // gw
    for blk in range(tm // SG_BLOCK):
        rows = slice(blk * SG_BLOCK, (blk + 1) * SG_BLOCK)
        vb = v[rows, :]
        sv = sb_ref[...]
        for g in range(SG_GROUPS):
            sv = sv + jnp.where(grp == g, _dot(sw_ref[g], vb), 0.0)
        ob_ref[rows, :] = (u[rows, :] * sv).astype(ob_ref.dtype)


def _mixers_ab(h, w_ab, conv_w, ng, nb, sg_wm, sg_bias):
    B, S, D = h.shape
    tm = TOKEN_TILE
    W = BRANCH_W
    blk = lambda b, j: (b, j, 0)
    return pl.pallas_call(
        _ab_kernel,
        grid=(B, S // tm),
        in_specs=[pl.BlockSpec((None, tm, D), blk),
                  _full((D, 5 * W)), _full((CONV_W, W)), _full((1, W)), _full((1, W)),
                  _full((SG_GROUPS, SG_BLOCK, SG_BLOCK)), _full((SG_BLOCK, W))],
        out_specs=[pl.BlockSpec((None, tm, W), blk), pl.BlockSpec((None, tm, W), blk)],
        out_shape=[jax.ShapeDtypeStruct((B, S, W), BF16)] * 2,
        scratch_shapes=[pltpu.VMEM((V7X_SUBLANES, W), F32)],
        compiler_params=_cparams("parallel", "arbitrary"),
    )(h, w_ab, conv_w, ng, nb, sg_wm, sg_bias)


def _softplus(x):
    return jnp.maximum(x, 0.0) + jnp.log(1.0 + jnp.exp(-jnp.abs(x)))


def _rwkv_kernel(h_ref, w_ref, mu_ref, w0_ref, a0_ref, lora_ref, gup_ref, kk_ref, ka_ref,
                 rk_ref, lng_ref, lnb_ref, ones_ref, o_ref,
                 tail_ref, state_ref, v_s, y_s, gend_s, ar_s, bk_s, inv_s, rb_s, mk_s):
    @pl.when(pl.program_id(1) == 0)
    def _():
        tail_ref[...] = jnp.zeros_like(tail_ref)
        state_ref[...] = jnp.zeros_like(state_ref)

    tm = h_ref.shape[0]
    W = BRANCH_W
    N = RW_HEAD_DIM
    L = RW_CHUNK
    z = _dot(h_ref[...].astype(BF16), w_ref[...])
    zprev = _shift_rows(z, 1, tail_ref[...])
    tail_ref[...] = z[tm - V7X_SUBLANES:, :]
    z = z + (zprev - z) * mu_ref[...]
    r = z[:, 0:W]
    k = z[:, W:2 * W]
    v = z[:, 2 * W:3 * W]
    xwa = z[:, 3 * W:3 * W + 2 * RW_LORA]
    xg = z[:, 3 * W + 2 * RW_LORA:]
    lane = lax.broadcasted_iota(jnp.int32, xwa.shape, 1)
    lora_in = jnp.where(lane < RW_LORA, jnp.tanh(xwa), xwa)
    lora = _dot_split(lora_in, lora_ref[...])
    w = -_softplus(-(w0_ref[...] + lora[:, 0:W])) - 0.5
    log_decay = -jnp.exp(w)
    a = jax.nn.sigmoid(a0_ref[...] + lora[:, W:2 * W])
    g = _dot(jax.nn.sigmoid(xg).astype(BF16), gup_ref[...])
    ones = ones_ref[...]
    kk = k * kk_ref[...]
    kk = kk / jnp.maximum(jnp.sqrt(_dot_select(kk * kk, ones)), 1e-12)
    k = k * (1.0 + (a - 1.0) * ka_ref[...])
    bonus = _dot_select(r * k * rk_ref[...], ones) * v

    ti = lax.broadcasted_iota(jnp.int32, (L, L), 0)
    tj = lax.broadcasted_iota(jnp.int32, (L, L), 1)
    tri3 = jnp.concatenate([(ti >= tj).astype(BF16)] * 3, axis=1)
    eye = (ti == tj).astype(F32)
    t2 = lax.broadcasted_iota(jnp.int32, (2 * L, L), 0)
    s2 = lax.broadcasted_iota(jnp.int32, (2 * L, L), 1)
    mask2 = ((t2 < L) & (t2 > s2)) | (t2 - L >= s2)
    n_chunks = tm // L

    alpha = -kk
    beta = kk * a
    v_s[...] = v.astype(BF16)
    for c in range(n_chunks):
        rows = slice(c * L, (c + 1) * L)
        ld = log_decay[rows, :]
        cum = _dot(tri3, jnp.concatenate(_bf16_terms(ld, 3), axis=0))
        gam = jnp.exp(cum)
        gam_inv = jnp.exp(-cum)
        gend_s[c * V7X_SUBLANES:(c + 1) * V7X_SUBLANES, :] = jnp.broadcast_to(
            gam[L - 1:L, :], (V7X_SUBLANES, W))
        ar_all = jnp.concatenate([alpha[rows, :] * jnp.exp(cum - ld), r[rows, :] * gam],
                                 axis=0).astype(BF16)
        bk_all = jnp.concatenate([beta[rows, :] * gam_inv, k[rows, :] * gam_inv],
                                 axis=0).astype(BF16)
        ar_s[c] = ar_all
        bk_s[c] = bk_all

    heads = [(c, hd) for c in range(n_chunks) for hd in range(RW_HEADS)]
    hcols = lambda hd: slice(hd * N, (hd + 1) * N)
    invs, ps = [], []
    for c, hd in heads:
        pair = _dot_nt(ar_s[c, :, hcols(hd)], bk_s[c, :, hcols(hd)])
        m_b = jnp.where(mask2, pair[:, :L], 0.0)
        mk_s[c * RW_HEADS + hd] = jnp.where(mask2, pair[:, L:], 0.0).astype(BF16)
        rb_s[c * RW_HEADS + hd] = m_b[L:].astype(BF16)
        invs.append(eye + m_b[:L])
        ps.append(m_b[:L].astype(BF16))
    ps = [_dot(pb, pb) for pb in ps]
    for _ in range(int(math.log2(L)) - 2):
        both = [_dot(jnp.concatenate([p.astype(BF16), inv.astype(BF16)], axis=0), p.astype(BF16))
                for p, inv in zip(ps, invs)]
        ps = [b2[:L] for b2 in both]
        invs = [inv + b2[L:] for inv, b2 in zip(invs, both)]
    for i, (inv, p) in enumerate(zip(invs, ps)):
        inv_s[i] = (inv + _dot(inv.astype(BF16), p.astype(BF16))).astype(BF16)

    def chunk(c, carry):
        rows = pl.ds(pl.multiple_of(c * L, L), L)
        ar_all = ar_s[c]
        bk_all = bk_s[c]
        v_all = v_s[rows, :]
        gend = gend_s[pl.ds(pl.multiple_of(c * V7X_SUBLANES, V7X_SUBLANES), 1), :]
        hs = range(RW_HEADS)
        sts = [state_ref[hd] for hd in hs]
        bases = [_dot_nt(ar_all[:, hcols(hd)], sts[hd].astype(BF16))
                 + _dot(mk_s[c * RW_HEADS + hd], v_all[:, hcols(hd)]) for hd in hs]
        us = [_dot(inv_s[c * RW_HEADS + hd], bases[hd][:L].astype(BF16)).astype(BF16) for hd in hs]
        for hd in hs:
            y_s[rows, hcols(hd)] = bases[hd][L:] + _dot(rb_s[c * RW_HEADS + hd], us[hd])
        for hd in hs:
            st = sts[hd] + _dot_tn(jnp.concatenate([us[hd], v_all[:, hcols(hd)]], axis=0),
                                   bk_all[:, hcols(hd)])
            state_ref[hd] = st * gend[:, hcols(hd)]
        return carry

    lax.fori_loop(0, n_chunks, chunk, 0)

    y = y_s[...]
    inv_n = 1.0 / N
    m = _dot_select(y, ones) * inv_n
    yc = y - m
    var = _dot_select(yc * yc, ones) * inv_n
    yn = yc * lax.rsqrt(var + RW_LN_EPS) * lng_ref[...] + lnb_ref[...]
    o_ref[...] = ((yn + bonus) * g).astype(o_ref.dtype)


def _mixer_rwkv(h, w_c, mu, w0, a0, lora_w, g_up, k_k, k_a, r_k, ln_g, ln_b, head_ones):
    B, S, D = h.shape
    tm = SEQ_TILE
    W = BRANCH_W
    L = RW_CHUNK
    nc = tm // L
    blk = lambda b, j: (b, j, 0)
    row = lambda n: _full((1, n))
    return pl.pallas_call(
        _rwkv_kernel,
        grid=(B, S // tm),
        in_specs=[pl.BlockSpec((None, tm, D), blk), _full((D, RW_IN)), row(RW_IN), row(W), row(W),
                  _full((2 * RW_LORA, 2 * W)), _full((RW_GATE_LORA, W)), row(W), row(W), row(W),
                  row(W), row(W), _full(head_ones.shape)],
        out_specs=pl.BlockSpec((None, tm, W), blk),
        out_shape=jax.ShapeDtypeStruct((B, S, W), BF16),
        scratch_shapes=[pltpu.VMEM((V7X_SUBLANES, RW_IN), F32),
                        pltpu.VMEM((RW_HEADS, RW_HEAD_DIM, RW_HEAD_DIM), F32),
                        pltpu.VMEM((tm, W), BF16),
                        pltpu.VMEM((tm, W), F32),
                        pltpu.VMEM((nc * V7X_SUBLANES, W), F32),
                        pltpu.VMEM((nc, 2 * L, W), BF16),
                        pltpu.VMEM((nc, 2 * L, W), BF16),
                        pltpu.VMEM((nc * RW_HEADS, L, L), BF16),
                        pltpu.VMEM((nc * RW_HEADS, L, L), BF16),
                        pltpu.VMEM((nc * RW_HEADS, 2 * L, L), BF16)],
        compiler_params=_cparams("parallel", "arbitrary"),
    )(h, w_c, mu, w0, a0, lora_w, g_up, k_k, k_a, r_k, ln_g, ln_b, head_ones)


def _s5_kernel(h_ref, w_ref, bre_ref, bim_ref, are_ref, aim_ref, cre_ref, cim_ref,
               d_ref, gw_ref, gb_ref, o_ref, xr_s, xi_s, st_s):
    @pl.when(pl.program_id(1) == 0)
    def _():
        st_s[...] = jnp.zeros_like(st_s)

    tm = h_ref.shape[0]
    u = _dot(h_ref[...].astype(BF16), w_ref[...])
    ub = u.astype(BF16)
    xr_s[...] = _dot(ub, bre_ref[...])
    xi_s[...] = _dot(ub, bim_ref[...])
    ar = are_ref[...]
    ai = aim_ref[...]

    def step(t, carry):
        xr, xi = carry
        row = pl.ds(t, 1)
        nr = ar * xr - ai * xi + xr_s[row, :]
        ni = ar * xi + ai * xr + xi_s[row, :]
        xr_s[row, :] = nr
        xi_s[row, :] = ni
        return nr, ni

    xr, xi = lax.fori_loop(0, tm, step, (st_s[0:1, :], st_s[1:2, :]), unroll=8)
    st_s[0:1, :] = xr
    st_s[1:2, :] = xi
    y = (_dot(xr_s[...].astype(BF16), cre_ref[...]) - _dot(xi_s[...].astype(BF16), cim_ref[...])
         + d_ref[...] * u)
    y = jax.nn.gelu(y)
    gate = jax.nn.sigmoid(_dot(y.astype(BF16), gw_ref[...]) + gb_ref[...])
    o_ref[...] = (y * gate).astype(o_ref.dtype)


def _mixer_s5(h, w_d, bre, bim, are, aim, cre, cim, d, glu_w, glu_b):
    B, S, D = h.shape
    tm = TOKEN_TILE
    W = BRANCH_W
    blk = lambda b, j: (b, j, 0)
    return pl.pallas_call(
        _s5_kernel,
        grid=(B, S // tm),
        in_specs=[pl.BlockSpec((None, tm, D), blk), _full((D, W)),
                  _full((W, S5_N)), _full((W, S5_N)), _full(are.shape), _full(aim.shape),
                  _full((S5_N, W)), _full((S5_N, W)), _full((1, W)), _full((W, W)), _full((1, W))],
        out_specs=pl.BlockSpec((None, tm, W), blk),
        out_shape=jax.ShapeDtypeStruct((B, S, W), BF16),
        scratch_shapes=[pltpu.VMEM((tm, S5_N), F32), pltpu.VMEM((tm, S5_N), F32),
                        pltpu.VMEM((V7X_SUBLANES, S5_N), F32)],
        compiler_params=_cparams("parallel", "arbitrary"),
    )(h, w_d, bre, bim, are, aim, cre, cim, d, glu_w, glu_b)


def _abd_kernel(h_ref, wab_ref, cw_ref, ng_ref, nb_ref, sw_ref, sb_ref,
                wd_ref, bre_ref, bim_ref, are_ref, aim_ref, cre_ref, cim_ref, d_ref, gw_ref, gb_ref,
                oa_ref, ob_ref, od_ref, tail_ref, xr_s, xi_s, st_s):
    _ab_kernel(h_ref, wab_ref, cw_ref, ng_ref, nb_ref, sw_ref, sb_ref, oa_ref, ob_ref, tail_ref)
    _s5_kernel(h_ref, wd_ref, bre_ref, bim_ref, are_ref, aim_ref, cre_ref, cim_ref,
               d_ref, gw_ref, gb_ref, od_ref, xr_s, xi_s, st_s)


def _mixers_abd(h, ab_args, s5_args):
    B, S, D = h.shape
    tm = TOKEN_TILE
    W = BRANCH_W
    blk = lambda b, j: (b, j, 0)
    out = pl.BlockSpec((None, tm, W), blk)
    args = tuple(ab_args) + tuple(s5_args)
    return pl.pallas_call(
        _abd_kernel,
        grid=(B, S // tm),
        in_specs=[pl.BlockSpec((None, tm, D), blk)] + [_full(a.shape) for a in args],
        out_specs=[out, out, out],
        out_shape=[jax.ShapeDtypeStruct((B, S, W), BF16)] * 3,
        scratch_shapes=[pltpu.VMEM((V7X_SUBLANES, W), F32),
                        pltpu.VMEM((tm, S5_N), F32), pltpu.VMEM((tm, S5_N), F32),
                        pltpu.VMEM((V7X_SUBLANES, S5_N), F32)],
        compiler_params=_cparams("parallel", "arbitrary"),
    )(h, *args)


def _merge_kernel(h_ref, oa_ref, ob_ref, oc_ref, od_ref, wg_ref, gb_ref, br_ref, wo_ref,
                  g_ref, b_ref, o_ref):
    D = D_MODEL
    h = h_ref[...]
    hb = h.astype(BF16)
    merged = None
    for i, br_in in enumerate((oa_ref, ob_ref, oc_ref, od_ref)):
        gate = jax.nn.sigmoid(_dot(hb, wg_ref[:, i * D:(i + 1) * D]) + gb_ref[i:i + 1, :])
        term = gate * _dot(br_in[...], br_ref[i])
        merged = term if merged is None else merged + term
    y = _dot(merged.astype(BF16), wo_ref[...])
    o_ref[...] = _layer_norm(DN_ALPHA * h + y, g_ref[...], b_ref[...])


def _merge(h2, oa, ob, oc, od, w_g, gate_b, br_proj, w_out, g, b):
    T, D = h2.shape
    tm = TOKEN_TILE
    W = BRANCH_W
    tok = lambda n: pl.BlockSpec((tm, n), lambda i: (i, 0))
    return pl.pallas_call(
        _merge_kernel,
        grid=(T // tm,),
        in_specs=[tok(D), tok(W), tok(W), tok(W), tok(W),
                  _full((D, N_BRANCH * D)), _full((N_BRANCH, D)), _full((N_BRANCH, W, D)),
                  _full((D, D)), _full((1, D)), _full((1, D))],
        out_specs=tok(D),
        out_shape=jax.ShapeDtypeStruct((T, D), F32),
        compiler_params=_cparams("parallel"),
    )(h2, oa, ob, oc, od, w_g, gate_b, br_proj, w_out, g, b)


def _kv_kernel(m_ref, wk_ref, wv_ref, k_ref, v_ref):
    mb = m_ref[...].astype(BF16)
    k_ref[...] = _dot(mb, wk_ref[...]).astype(k_ref.dtype)
    v_ref[...] = _dot(mb, wv_ref[...]).astype(v_ref.dtype)


def _kv_proj(mem2, wk, wv):
    R, D = mem2.shape
    tm = TOKEN_TILE
    tok = pl.BlockSpec((tm, D), lambda i: (i, 0))
    return pl.pallas_call(
        _kv_kernel,
        grid=(R // tm,),
        in_specs=[tok, _full((D, D)), _full((D, D))],
        out_specs=[tok, tok],
        out_shape=[jax.ShapeDtypeStruct((R, D), BF16)] * 2,
        compiler_params=_cparams("parallel"),
    )(mem2, wk, wv)


def _attn_kernel(h_ref, wq_ref, k_ref, v_ref, wo_ref, g_ref, b_ref, o_ref, ot_ref):
    h = h_ref[...]
    q = _dot(h.astype(BF16), wq_ref[...]).astype(BF16)
    scale = XA_HEAD_DIM ** -0.5
    outs = []
    for hd in range(XA_HEADS):
        cols = slice(hd * XA_HEAD_DIM, (hd + 1) * XA_HEAD_DIM)
        s = _dot_nt(q[:, cols], k_ref[:, cols]) * scale
        s = s - jnp.max(s, axis=-1, keepdims=True)
        p = jnp.exp(s)
        p = p / jnp.sum(p, axis=-1, keepdims=True)
        outs.append(_dot(p.astype(BF16), v_ref[:, cols]))
    o = jnp.concatenate(outs, axis=-1).astype(BF16)
    y = _dot(o, wo_ref[...])
    out = _layer_norm(DN_ALPHA * h + y, g_ref[...], b_ref[...])
    o_ref[...] = out
    _store_row_tiles(ot_ref, out)


def _cross_attention(h, kmem, vmem, wq, wo, g, b):
    B, S, D = h.shape
    M = kmem.shape[1]
    tm = TOKEN_TILE
    n = S // tm
    blk = lambda bi, j: (bi, j, 0)
    mem = lambda bi, j: (bi, 0, 0)
    return pl.pallas_call(
        _attn_kernel,
        grid=(B, n),
        in_specs=[pl.BlockSpec((None, tm, D), blk), _full((D, D)),
                  pl.BlockSpec((None, M, D), mem), pl.BlockSpec((None, M, D), mem),
                  _full((D, D)), _full((1, D)), _full((1, D))],
        out_specs=[pl.BlockSpec((None, tm, D), blk),
                   pl.BlockSpec((tm * ROW_TILE, V7X_LANES), lambda bi, j: (bi * n + j, 0))],
        out_shape=[jax.ShapeDtypeStruct((B, S, D), F32),
                   jax.ShapeDtypeStruct((B * S * ROW_TILE, V7X_LANES), F32)],
        compiler_params=_cparams("parallel", "parallel"),
    )(h, wq, kmem, vmem, wo, g, b)


def _router_kernel(h_ref, w_ref, b_ref, route_ref, rw_ref, cnt_ref, carry_ref):
    @pl.when(pl.program_id(0) == 0)
    def _():
        carry_ref[...] = jnp.zeros_like(carry_ref)

    tm = h_ref.shape[0]
    E = N_EXPERTS
    logits = _dot_split(h_ref[...], w_ref[...]) + b_ref[...]
    lane = lax.broadcasted_iota(jnp.int32, (tm, E), 1)
    work = logits
    vals, hots, idxs = [], [], []
    for _ in range(TOP_K):
        m = jnp.max(work, axis=-1, keepdims=True)
        idx = jnp.min(jnp.where(work == m, lane, E), axis=-1, keepdims=True)
        hot = lane == idx
        work = jnp.where(hot, -jnp.inf, work)
        vals.append(m)
        hots.append(hot)
        idxs.append(idx)
    exps = [jnp.exp(v - vals[0]) for v in vals]
    denom = exps[0] + exps[1] + exps[2] + exps[3]
    multi = (hots[0] | hots[1] | hots[2] | hots[3]).astype(F32)
    ti = lax.broadcasted_iota(jnp.int32, (tm, tm), 0)
    tj = lax.broadcasted_iota(jnp.int32, (tm, tm), 1)
    before = (ti > tj).astype(BF16)
    prefix = _dot(before, multi.astype(BF16)) + carry_ref[0:1, :]
    out_lane = lax.broadcasted_iota(jnp.int32, (tm, ROUTE_LANES), 1)
    route = jnp.zeros((tm, ROUTE_LANES), jnp.int32)
    rw = jnp.zeros((tm, ROUTE_LANES), F32)
    for kk in range(TOP_K):
        rank = jnp.sum(jnp.where(hots[kk], prefix, 0.0), axis=-1, keepdims=True).astype(jnp.int32)
        route = jnp.where(out_lane == kk, idxs[kk], route)
        route = jnp.where(out_lane == TOP_K + kk, rank, route)
        rw = jnp.where(out_lane == kk, exps[kk] / denom, rw)
    route_ref[...] = route
    rw_ref[...] = rw
    carry_ref[0:1, :] = carry_ref[0:1, :] + jnp.sum(multi, axis=0, keepdims=True)
    cnt_ref[...] = carry_ref[...]


def _router(h2, router_w, router_b):
    T, D = h2.shape
    tm = TOKEN_TILE
    E = N_EXPERTS
    return pl.pallas_call(
        _router_kernel,
        grid=(T // tm,),
        in_specs=[pl.BlockSpec((tm, D), lambda i: (i, 0)), _full((D, E)), _full((1, E))],
        out_specs=[pl.BlockSpec((tm, ROUTE_LANES), lambda i: (i, 0)),
                   pl.BlockSpec((tm, ROUTE_LANES), lambda i: (i, 0)),
                   _full((V7X_SUBLANES, E))],
        out_shape=[jax.ShapeDtypeStruct((T, ROUTE_LANES), jnp.int32),
                   jax.ShapeDtypeStruct((T, ROUTE_LANES), F32),
                   jax.ShapeDtypeStruct((V7X_SUBLANES, E), F32)],
        scratch_shapes=[pltpu.VMEM((V7X_SUBLANES, E), F32)],
        compiler_params=_cparams("arbitrary"),
    )(h2, router_w, router_b)


def _dest_kernel(route_ref, start_ref, dest_ref):
    route = route_ref[...]
    tm = route.shape[0]
    lane = lax.broadcasted_iota(jnp.int32, (tm, N_EXPERTS), 1)
    out_lane = lax.broadcasted_iota(jnp.int32, (tm, ROUTE_LANES), 1)
    dest = jnp.zeros((tm, ROUTE_LANES), jnp.int32)
    start = start_ref[...]
    for kk in range(TOP_K):
        idx = route[:, kk:kk + 1]
        base = jnp.sum(jnp.where(lane == idx, start, 0), axis=-1, keepdims=True)
        dest = jnp.where(out_lane == kk, base + route[:, TOP_K + kk:TOP_K + kk + 1], dest)
    dest_ref[...] = dest


def _dest_slots(route, pad_start):
    T = route.shape[0]
    tm = TOKEN_TILE * 2
    return pl.pallas_call(
        _dest_kernel,
        grid=(T // tm,),
        in_specs=[pl.BlockSpec((tm, ROUTE_LANES), lambda i: (i, 0)), _full((1, N_EXPERTS))],
        out_specs=pl.BlockSpec((tm, ROUTE_LANES), lambda i: (i, 0)),
        out_shape=jax.ShapeDtypeStruct((T, ROUTE_LANES), jnp.int32),
        compiler_params=_cparams("parallel"),
    )(route, pad_start.reshape(1, N_EXPERTS))


def _slot_source_kernel(fill_lo_ref, fill_hi_ref, dest_hbm, row_ref, dest_s, sem, *, n_tok):
    i = pl.program_id(0)
    n = dest_s.shape[0]

    @pl.when(i == 0)
    def _():
        def fill(s, carry):
            parity = (s // MOE_TILE + 1) % 2
            row_ref[s] = n_tok * TOP_K + parity * MOE_TILE + s % MOE_TILE
            return carry

        lax.fori_loop(0, MOE_TILE, fill, 0)
        for e in range(fill_lo_ref.shape[0]):
            lax.fori_loop(MOE_TILE + fill_lo_ref[e], MOE_TILE + fill_hi_ref[e], fill, 0)

    idx_copy = pltpu.make_async_copy(dest_hbm.at[i], dest_s, sem)
    idx_copy.start()
    idx_copy.wait()
    slots = row_ref.at[pl.ds(MOE_TILE, row_ref.shape[0] - MOE_TILE)]
    first_tok = i * (n // TOP_K)

    def body(t, carry):
        for kk in range(TOP_K):
            slots[dest_s[t * TOP_K + kk]] = (kk * n_tok + first_tok) + t
        return carry

    lax.fori_loop(0, n // TOP_K, body, 0, unroll=8)


def _slot_sources(fill_lo, fill_hi, dest_tiles, n_slots):
    n_tok_tiles, n = dest_tiles.shape
    return pl.pallas_call(
        functools.partial(_slot_source_kernel, n_tok=n_tok_tiles * n // TOP_K),
        grid_spec=pltpu.PrefetchScalarGridSpec(
            num_scalar_prefetch=2,
            grid=(n_tok_tiles,),
            in_specs=[pl.BlockSpec(memory_space=pl.ANY)],
            out_specs=pl.BlockSpec(memory_space=pltpu.SMEM),
            scratch_shapes=[pltpu.SMEM((n,), jnp.int32), pltpu.SemaphoreType.DMA]),
        out_shape=jax.ShapeDtypeStruct((MOE_TILE + n_slots,), jnp.int32),
        compiler_params=_cparams("arbitrary"),
    )(fill_lo, fill_hi, dest_tiles)


GU_BLOCK = 2 * V7X_LANES


def _unzip_kernel(w_ref, p_ref, o_ref):
    for c in range(w_ref.shape[1] // GU_BLOCK):
        cols = slice(c * GU_BLOCK, (c + 1) * GU_BLOCK)
        o_ref[:, cols] = _dot(w_ref[:, cols].astype(BF16), p_ref[...]).astype(o_ref.dtype)


def _unzip_gate_up(w1_layers, layer):
    _, E, D, F2 = w1_layers.shape
    tr = TOKEN_TILE
    j = jnp.arange(GU_BLOCK)
    perm = (j[:, None] == jnp.where(j < V7X_LANES, 2 * j, 2 * (j - V7X_LANES) + 1)[None, :])
    return pl.pallas_call(
        _unzip_kernel,
        grid=(E, D // tr),
        in_specs=[pl.BlockSpec((None, None, tr, F2), lambda e, r: (layer, e, r, 0)),
                  _full((GU_BLOCK, GU_BLOCK))],
        out_specs=pl.BlockSpec((None, tr, F2), lambda e, r: (e, r, 0)),
        out_shape=jax.ShapeDtypeStruct((E, D, F2), BF16),
        compiler_params=_cparams("parallel", "parallel"),
    )(w1_layers, perm.astype(BF16))


def _unzip_bias(b1):
    E, F2 = b1.shape
    return b1.reshape(E, F2 // GU_BLOCK, V7X_LANES, 2).transpose(0, 1, 3, 2).reshape(E, 1, F2)


def _expert_ffn(x, w1_ref, b1_ref, w2_ref, b2_ref):
    hid = _dot(x, w1_ref[...]) + b1_ref[...]
    acts = []
    for c in range(hid.shape[1] // GU_BLOCK):
        gate = jnp.minimum(hid[:, c * GU_BLOCK:c * GU_BLOCK + V7X_LANES], SWIGLU_LIMIT)
        up = jnp.clip(hid[:, c * GU_BLOCK + V7X_LANES:(c + 1) * GU_BLOCK],
                      -SWIGLU_LIMIT, SWIGLU_LIMIT)
        acts.append(((up + 1.0) * gate * jax.nn.sigmoid(SWIGLU_ALPHA * gate)).astype(BF16))
    return _dot(jnp.concatenate(acts, axis=-1), w2_ref[...].astype(BF16)) + b2_ref[...]


def _expert_kernel(blk_e_ref, n_used_ref, src_ref, h_hbm, w1_ref, b1_ref, w2_ref, b2_ref, yk_hbm,
                   xbuf0, xbuf1, ybuf0, ybuf1, gsem, ssem):
    del blk_e_ref
    i = pl.program_id(0)
    n_used = n_used_ref[0]
    xbufs = (xbuf0, xbuf1)
    ybufs = (ybuf0, ybuf1)
    R = ROW_TILE
    tm = xbuf0.shape[0] // R
    T = h_hbm.shape[0] // R
    n_assign = T * TOP_K

    def tile_rows(r):
        return pl.ds(r * R if isinstance(r, int) else pl.multiple_of(r * R, R), R)

    def queue_of(j):
        return j % V7X_DMA_PRIORITIES if isinstance(j, int) else 0

    def gather_row(tile, b, j):
        row = src_ref[(tile + 1) * tm + j]
        tok = row & (T - 1) if T & (T - 1) == 0 else lax.rem(row, T)
        pltpu.make_async_copy(h_hbm.at[tile_rows(tok)], xbufs[b].at[tile_rows(j)],
                              gsem.at[b]).start(priority=queue_of(j))

    def scatter_row(tile, b, j):
        row = src_ref[(tile + 1) * tm + j]
        pltpu.make_async_copy(ybufs[b].at[tile_rows(j)], yk_hbm.at[tile_rows(row)],
                              ssem.at[b]).start(priority=queue_of(j))

    def wait_gather(b):
        pltpu.make_async_copy(h_hbm.at[pl.ds(0, tm * R)], xbufs[b], gsem.at[b]).wait()

    def wait_scatter(b):
        pltpu.make_async_copy(ybufs[b], yk_hbm.at[pl.ds(0, tm * R)], ssem.at[b]).wait()

    def rows_loop(fn):
        def body(j, carry):
            fn(j)
            return carry
        lax.fori_loop(0, tm, body, 0)

    @pl.when(i == 0)
    def _():
        rows_loop(lambda j: gather_row(0, 0, j))
        ybuf0[...] = jnp.zeros_like(ybuf0)
        ybuf1[...] = jnp.zeros_like(ybuf1)
        pltpu.make_async_copy(ybuf0, yk_hbm.at[pl.ds(n_assign * R, tm * R)], ssem.at[0]).start()

    for b in range(2):
        @pl.when((i < n_used) & (i % 2 == b))
        def _(b=b):
            wait_gather(b)
            wait_scatter(b)
            for j in range(tm):
                gather_row(i + 1, 1 - b, j)
                scatter_row(i - 1, 1 - b, j)
            x = _load_row_tiles(xbufs[b]).astype(BF16)
            _store_row_tiles(ybufs[b], _expert_ffn(x, w1_ref, b1_ref, w2_ref, b2_ref))

        @pl.when((i == n_used) & (i % 2 == b))
        def _(b=b):
            wait_gather(b)
            wait_scatter(b)
            rows_loop(lambda j: scatter_row(i - 1, 1 - b, j))
            wait_scatter(1 - b)


def _experts(blk_e, n_used, src, h_tiles, w1, b1, w2_layers, layer, b2):
    R = ROW_TILE
    T = h_tiles.shape[0] // R
    D = D_MODEL
    tm = MOE_TILE
    F = D_FF
    n_tiles = src.shape[0] // tm - 1
    per_e = lambda shape: pl.BlockSpec((None,) + shape, lambda i, be, nu, sr: (be[i], 0, 0))
    w2_spec = pl.BlockSpec((None, None, F, D), lambda i, be, nu, sr: (layer, be[i], 0, 0))
    return pl.pallas_call(
        _expert_kernel,
        grid_spec=pltpu.PrefetchScalarGridSpec(
            num_scalar_prefetch=3,
            grid=(n_tiles,),
            in_specs=[pl.BlockSpec(memory_space=pl.ANY),
                      per_e((D, 2 * F)), per_e((1, 2 * F)), w2_spec, per_e((1, D))],
            out_specs=pl.BlockSpec(memory_space=pl.ANY),
            scratch_shapes=[pltpu.VMEM((tm * R, V7X_LANES), F32)] * 4
                           + [pltpu.SemaphoreType.DMA((2,)), pltpu.SemaphoreType.DMA((2,))]),
        out_shape=jax.ShapeDtypeStruct(((T * TOP_K + 2 * tm) * R, V7X_LANES), F32),
        compiler_params=_cparams("arbitrary"),
    )(blk_e, n_used, src, h_tiles, w1, b1, w2_layers, b2)


def _combine_kernel(y0_ref, y1_ref, y2_ref, y3_ref, h_ref, rw_ref, g_ref, b_ref, o_ref):
    rw = rw_ref[...]
    moe = rw[:, 0:1] * _load_row_tiles(y0_ref)
    for kk, y_ref in enumerate((y1_ref, y2_ref, y3_ref), start=1):
        moe = moe + rw[:, kk:kk + 1] * _load_row_tiles(y_ref)
    o_ref[...] = _layer_norm(DN_ALPHA * h_ref[...] + moe, g_ref[...], b_ref[...])


def _combine(yk, h2, rw, g, b):
    T, D = h2.shape
    tm = TOKEN_TILE
    n = T // tm
    choice = lambda kk: pl.BlockSpec((tm * ROW_TILE, V7X_LANES), lambda i: (kk * n + i, 0))
    return pl.pallas_call(
        _combine_kernel,
        grid=(n,),
        in_specs=[choice(kk) for kk in range(TOP_K)]
                 + [pl.BlockSpec((tm, D), lambda i: (i, 0)),
                    pl.BlockSpec((tm, ROUTE_LANES), lambda i: (i, 0)),
                    _full((1, D)), _full((1, D))],
        out_specs=pl.BlockSpec((tm, D), lambda i: (i, 0)),
        out_shape=jax.ShapeDtypeStruct((T, D), F32),
        compiler_params=_cparams("parallel"),
    )(yk, yk, yk, yk, h2, rw, g, b)


def _s5_discretise(a_re, a_im, b_re, b_im, c_re, c_im, log_dt):
    G, P, C = S5_GROUPS, S5_STATE, S5_CH
    dt = jnp.exp(log_dt)[:, None]
    mag = jnp.exp(a_re * dt)
    abar_re = mag * jnp.cos(a_im * dt)
    abar_im = mag * jnp.sin(a_im * dt)
    den = a_re * a_re + a_im * a_im
    num_re = abar_re - 1.0
    coef_re = (num_re * a_re + abar_im * a_im) / den
    coef_im = (abar_im * a_re - num_re * a_im) / den
    bbar_re = coef_re[..., None] * b_re - coef_im[..., None] * b_im
    bbar_im = coef_re[..., None] * b_im + coef_im[..., None] * b_re
    eye = jnp.eye(G, dtype=F32)
    to_in = lambda m: jnp.einsum('gpc,gh->gchp', m, eye).reshape(G * C, G * P)
    to_out = lambda m: jnp.einsum('gcp,gh->gphc', m, eye).reshape(G * P, G * C)
    return (to_in(bbar_re).astype(BF16), to_in(bbar_im).astype(BF16),
            abar_re.reshape(1, G * P), abar_im.reshape(1, G * P),
            to_out(c_re).astype(BF16), to_out(c_im).astype(BF16))


def _moe_layer(h2, h_tiles, layer, router_w, router_b, w1_layers, b1, w2_layers, b2, g, b):
    T, D = h2.shape
    E = N_EXPERTS
    tm = TOKEN_TILE
    route, rw, counts = _router(h2, router_w, router_b.reshape(1, E))
    counts = counts[0].astype(jnp.int32)
    padded = (counts + MOE_TILE - 1) // MOE_TILE * MOE_TILE
    pad_end = jnp.cumsum(padded)
    pad_start = pad_end - padded
    n_tiles = T * TOP_K // MOE_TILE + E
    n_slots = n_tiles * MOE_TILE
    tile_start = jnp.arange(n_tiles, dtype=jnp.int32) * MOE_TILE
    blk_e = jnp.minimum(jnp.sum(tile_start[:, None] >= pad_end[None, :], axis=1), E - 1).astype(jnp.int32)
    n_used = (pad_end[E - 1:] // MOE_TILE).astype(jnp.int32)
    dest = _dest_slots(route, pad_start.astype(jnp.int32))
    chunk = min(SLOT_MAP_CHUNK, T * TOP_K)
    dest_tiles = dest[:, :TOP_K].reshape(T * TOP_K // chunk, chunk)
    fill_lo = jnp.concatenate([pad_start + counts, pad_end[E - 1:]]).astype(jnp.int32)
    fill_hi = jnp.concatenate([pad_end, jnp.full((1,), n_slots)]).astype(jnp.int32)
    src = _slot_sources(fill_lo, fill_hi, dest_tiles, n_slots)
    yk = _experts(blk_e, n_used, src, h_tiles, _unzip_gate_up(w1_layers, layer), _unzip_bias(b1),
                  w2_layers, layer, b2[:, None, :])
    return _combine(yk, h2, rw, g.reshape(1, D), b.reshape(1, D))


def kernel(x, mem, ln_in_g, ln_in_b, w_in, conv_w, sg_norm_g, sg_norm_b, sg_w, sg_b, rw_mu, rw_w0, rw_w_up, rw_a0, rw_a_up, rw_g_up, rw_k_k, rw_k_a, rw_r_k, rw_ln_g, rw_ln_b, s5_a_re, s5_a_im, s5_b_re, s5_b_im, s5_c_re, s5_c_im, s5_d, s5_log_dt, s5_glu_w, s5_glu_b, br_proj, gate_b, w_out, ln1_g, ln1_b, xa_wq, xa_wk, xa_wv, xa_wo, ln2_g, ln2_b, router_w, router_b, ex_w1, ex_b1, ex_w2, ex_b2, ln3_g, ln3_b):
    B, S, D = x.shape
    M = mem.shape[1]
    T = B * S
    W = BRANCH_W
    row = lambda v: v.reshape(1, -1)
    pos = jnp.arange(SG_BLOCK)
    sg_mask = (pos[None, :] // CHUNK) <= (pos[:, None] // CHUNK)
    head_ones = jnp.kron(jnp.eye(RW_HEADS, dtype=F32), jnp.ones((RW_HEAD_DIM, RW_HEAD_DIM), F32))
    head_ones = jnp.concatenate([head_ones, head_ones], axis=0).astype(BF16)
    mem2 = mem.reshape(B * M, D)

    h = _input_ln(x.reshape(T, D), ln_in_g, ln_in_b)
    for l in range(DEPTH):
        w_l = w_in[l].astype(BF16)
        h3 = h.reshape(B, S, D)
        sg_wm = jnp.where(sg_mask[None], sg_w[l], 0.0).astype(BF16)
        sg_bias = jnp.repeat(sg_b[l].T, W // SG_GROUPS, axis=1)
        s5p = _s5_discretise(s5_a_re[l], s5_a_im[l], s5_b_re[l], s5_b_im[l], s5_c_re[l],
                             s5_c_im[l], s5_log_dt[l])
        o_a, o_b, o_d = _mixers_abd(
            h3,
            (w_l[:, :OFF_C], conv_w[l], row(sg_norm_g[l]), row(sg_norm_b[l]), sg_wm, sg_bias),
            (w_l[:, OFF_D:OFF_G], *s5p, row(s5_d[l]), s5_glu_w[l].astype(BF16),
             row(s5_glu_b[l])))
        zero = jnp.zeros((RW_LORA, W), F32)
        lora_w = jnp.concatenate([jnp.concatenate([rw_w_up[l], zero], axis=1),
                                  jnp.concatenate([zero, rw_a_up[l]], axis=1)], axis=0)
        o_c = _mixer_rwkv(h3, w_l[:, OFF_C:OFF_D], row(rw_mu[l]), row(rw_w0[l]), row(rw_a0[l]),
                          lora_w, rw_g_up[l].astype(BF16), row(rw_k_k[l]), row(rw_k_a[l]),
                          row(rw_r_k[l]),
                          row(rw_ln_g[l]), row(rw_ln_b[l]), head_ones)
        flat = lambda o: o.reshape(T, W)
        h = _merge(h, flat(o_a), flat(o_b), flat(o_c), flat(o_d), w_l[:, OFF_G:], gate_b[l],
                   br_proj[l].astype(BF16), w_out[l].astype(BF16), row(ln1_g[l]), row(ln1_b[l]))
        kmem, vmem = _kv_proj(mem2, xa_wk[l].astype(BF16), xa_wv[l].astype(BF16))
        h, h_tiles = _cross_attention(h.reshape(B, S, D), kmem.reshape(B, M, D),
                                      vmem.reshape(B, M, D), xa_wq[l].astype(BF16),
                                      xa_wo[l].astype(BF16), row(ln2_g[l]), row(ln2_b[l]))
        h = _moe_layer(h.reshape(T, D), h_tiles, l, router_w[l], router_b[l], ex_w1, ex_b1[l],
                       ex_w2, ex_b2[l], ln3_g[l], ln3_b[l])
    return h.reshape(B, S, D)
```
